```python
import jax, jax.numpy as jnp
from jax import lax
import numpy as np

D_MODEL = 1024
BATCH = 8
SEQ = 8192
DEPTH = 4

CHUNK = 64
P_DIM = 256
G_WIDTH = 1024
G_GROUPS = 8
G_GROUP_DIM = G_WIDTH // G_GROUPS
G_BLOCK = 128
M_HEADS = 4
M_QK_DIM = 128
M_V_DIM = 256
M_QK = M_HEADS * M_QK_DIM
M_V = M_HEADS * M_V_DIM
M_CHUNK = CHUNK
FORGET_BIAS = 3.0
N_EXPERTS = 64
TOP_K = 8
N_GROUPS = 8
TOPK_GROUPS = 4
D_EXPERT = 256
D_SHARED = 256
ROUTE_SCALE = 2.5
MOE_POS_BLOCK = 128
ALPHA = (2 * DEPTH) ** 0.25
BETA = (8 * DEPTH) ** -0.25
LN_EPS = 1e-5
RMS_EPS = 1e-6
D_IN = 2 * G_WIDTH + 2 * M_QK + 2 * M_V + 2 * M_HEADS + 2 * D_MODEL
SPLIT_POINTS = (
    G_WIDTH,
    2 * G_WIDTH,
    2 * G_WIDTH + M_QK,
    2 * G_WIDTH + 2 * M_QK,
    2 * G_WIDTH + 2 * M_QK + M_V,
    2 * G_WIDTH + 2 * M_QK + 2 * M_V,
    2 * G_WIDTH + 2 * M_QK + 2 * M_V + M_HEADS,
    2 * G_WIDTH + 2 * M_QK + 2 * M_V + 2 * M_HEADS,
    2 * G_WIDTH + 2 * M_QK + 2 * M_V + 2 * M_HEADS + D_MODEL,
)
F_OFF = 2 * G_WIDTH + 2 * M_QK + 2 * M_V + M_HEADS

kernel_name = "hybrid_gmlp_mlstm_moe_deepnorm"


def layer_norm(x, g, b):
    xf = x.astype(jnp.float32)
    mu = jnp.mean(xf, -1, keepdims=True)
    var = jnp.mean(jnp.square(xf - mu), -1, keepdims=True)
    return ((xf - mu) * lax.rsqrt(var + LN_EPS) * g + b).astype(x.dtype)


def spatial_gating(u, v, ln_g, ln_b, w_s, b_s):
    B, S, _ = v.shape
    v = layer_norm(v, ln_g, ln_b)
    v = v.reshape(B, S // G_BLOCK, G_BLOCK, G_GROUPS, G_GROUP_DIM)
    pos = jnp.arange(G_BLOCK)
    mask = (pos[None, :] // CHUNK) <= (pos[:, None] // CHUNK)
    w = jnp.where(mask[None], w_s, jnp.zeros((), w_s.dtype))
    mixed = jnp.einsum('gts,bnsgc->bntgc', w, v) + b_s.T[None, None, :, :, None]
    return u * mixed.reshape(B, S, G_WIDTH)


def mlstm_chunkwise(q, k, v, i_pre, f_pre):
    B, S, H, _ = q.shape
    L = M_CHUNK
    NC = S // L

    def to_chunks(t):
        t = t.reshape((B, NC, L, H) + t.shape[3:])
        return jnp.moveaxis(t, (1, 3), (0, 2))

    q = q * (M_QK_DIM ** -0.5)
    lf = jax.nn.log_sigmoid(f_pre)
    xs = (to_chunks(q), to_chunks(k), to_chunks(v), to_chunks(i_pre), to_chunks(lf))
    tri = jnp.tril(jnp.ones((L, L), dtype=bool))

    def step(carry, inp):
        C, n, m = carry
        qb, kb, vb, li, lfb = inp
        b = jnp.cumsum(lfb, axis=-1)
        d = b[..., :, None] - b[..., None, :] + li[..., None, :]
        d = jnp.where(tri, d, -jnp.inf)
        inter = b + m[..., None]
        m_t = jnp.maximum(inter, jnp.max(d, -1))
        w_intra = jnp.exp(d - m_t[..., None])
        w_inter = jnp.exp(inter - m_t)
        s = jnp.einsum('bhtd,bhsd->bhts', qb, kb) * w_intra
        num = (jnp.einsum('bhts,bhsv->bhtv', s, vb)
               + w_inter[..., None] * jnp.einsum('bhtd,bhdv->bhtv', qb, C))
        den = jnp.sum(s, -1) + w_inter * jnp.einsum('bhtd,bhd->bht', qb, n)
        h = num / jnp.maximum(jnp.abs(den), jnp.exp(-m_t))[..., None]
        b_last = b[..., -1]
        g = b_last[..., None] - b + li
        m_new = jnp.maximum(b_last + m, jnp.max(g, -1))
        decay = jnp.exp(b_last + m - m_new)
        wk = jnp.exp(g - m_new[..., None])[..., None] * kb
        C = decay[..., None, None] * C + jnp.einsum('bhsd,bhsv->bhdv', wk, vb)
        n = decay[..., None] * n + jnp.sum(wk, -2)
        return (C, n, m_new), h

    init = (jnp.zeros((B, H, M_QK_DIM, M_V_DIM), jnp.float32),
            jnp.zeros((B, H, M_QK_DIM), jnp.float32),
            jnp.zeros((B, H), jnp.float32))
    _, h = lax.scan(step, init, xs)
    return jnp.moveaxis(h, (0, 2), (1, 3)).reshape(B, S, H, M_V_DIM)


def token_mixer(x, w_in, b_in, sg_ln_g, sg_ln_b, w_s, b_s, mh_g, w_pa, w_pb, w_out):
    B, S, _ = x.shape
    f32 = jnp.float32
    z = jnp.einsum('bsd,de->bse', x, w_in) + b_in
    u, v, q, k, vm, o, ig, fg, ga, gb = jnp.split(z, SPLIT_POINTS, axis=-1)
    a = spatial_gating(jax.nn.gelu(u, approximate=False), jax.nn.gelu(v, approximate=False),
                       sg_ln_g, sg_ln_b, w_s, b_s)
    h = mlstm_chunkwise(q.reshape(B, S, M_HEADS, M_QK_DIM).astype(f32),
                        k.reshape(B, S, M_HEADS, M_QK_DIM).astype(f32),
                        vm.reshape(B, S, M_HEADS, M_V_DIM).astype(f32),
                        ig.astype(f32), fg.astype(f32))
    h = h * lax.rsqrt(jnp.mean(h * h, -1, keepdims=True) + RMS_EPS)
    mb = (jax.nn.sigmoid(o.astype(f32)) * (h.reshape(B, S, M_V) * mh_g)).astype(x.dtype)
    y = jax.nn.sigmoid(ga) * (a @ w_pa) + jax.nn.sigmoid(gb) * (mb @ w_pb)
    return y @ w_out


def moe(x, w_router, b_router, w_gate_e, w_up_e, w_down_e, w_gate_s, w_up_s, w_down_s):
    B, S, D = x.shape
    nblk = S // MOE_POS_BLOCK
    T = B * MOE_POS_BLOCK
    xt = x.reshape(B, nblk, MOE_POS_BLOCK, D).transpose(1, 0, 2, 3).reshape(nblk, T, D)
    per_group = N_EXPERTS // N_GROUPS

    def block(xb):
        scores = jax.nn.sigmoid(jnp.einsum('td,de->te', xb, w_router).astype(jnp.float32))
        sel = scores + b_router.astype(jnp.float32)
        grp_score = jnp.sum(lax.top_k(sel.reshape(T, N_GROUPS, per_group), 2)[0], -1)
        _, gidx = lax.top_k(grp_score, TOPK_GROUPS)
        gmask = jnp.any(gidx[..., None] == jnp.arange(N_GROUPS), axis=-2)
        emask = jnp.repeat(gmask, per_group, axis=-1)
        _, eidx = lax.top_k(jnp.where(emask, sel, -jnp.inf), TOP_K)
        wsel = jnp.take_along_axis(scores, eidx, -1)
        wsel = wsel / jnp.sum(wsel, -1, keepdims=True) * ROUTE_SCALE
        combine = jnp.zeros((T, N_EXPERTS), jnp.float32).at[jnp.arange(T)[:, None], eidx].set(wsel)
        hg = jnp.einsum('td,edf->tef', xb, w_gate_e)
        hu = jnp.einsum('td,edf->tef', xb, w_up_e)
        hid = jax.nn.silu(hg) * hu * combine[..., None].astype(xb.dtype)
        routed = jnp.einsum('tef,efd->td', hid, w_down_e)
        shared = (jax.nn.silu(xb @ w_gate_s) * (xb @ w_up_s)) @ w_down_s
        return routed + shared

    out = lax.map(block, xt)
    return out.reshape(nblk, B, MOE_POS_BLOCK, D).transpose(1, 0, 2, 3).reshape(B, S, D)


def setup_inputs(seed: int = 0) -> dict:
    key = jax.random.key(seed)
    ks = jax.random.split(key, 32)
    L = DEPTH

    def nrm(k, shape, scale):
        return jax.random.normal(k, shape, jnp.float32) * scale

    forget_off = jnp.zeros((D_IN,), jnp.float32).at[F_OFF:F_OFF + M_HEADS].set(FORGET_BIAS)
    return {
        "x": nrm(ks[0], (BATCH, SEQ, D_MODEL), 1.0),
        "p": nrm(ks[1], (DEPTH, BATCH, SEQ, P_DIM), 1.0),
        "w_in": nrm(ks[2], (L, D_MODEL, D_IN), D_MODEL ** -0.5),
        "b_in": nrm(ks[3], (L, D_IN), 0.02) + forget_off,
        "sg_ln_g": 1.0 + nrm(ks[4], (L, G_WIDTH), 0.02),
        "sg_ln_b": nrm(ks[5], (L, G_WIDTH), 0.02),
        "w_s": nrm(ks[6], (L, G_GROUPS, G_BLOCK, G_BLOCK), G_BLOCK ** -0.5),
        "b_s": 1.0 + nrm(ks[7], (L, G_GROUPS, G_BLOCK), 0.02),
        "mh_g": 1.0 + nrm(ks[8], (L, M_V), 0.02),
        "w_pa": nrm(ks[9], (L, G_WIDTH, D_MODEL), G_WIDTH ** -0.5 * BETA),
        "w_pb": nrm(ks[10], (L, M_V, D_MODEL), M_V ** -0.5 * BETA),
        "w_out": nrm(ks[11], (L, D_MODEL, D_MODEL), D_MODEL ** -0.5 * BETA),
        "ln1_g": 1.0 + nrm(ks[12], (L, D_MODEL), 0.02),
        "ln1_b": nrm(ks[13], (L, D_MODEL), 0.02),
        "w_router": nrm(ks[14], (L, D_MODEL, N_EXPERTS), D_MODEL ** -0.5),
        "b_router": nrm(ks[15], (L, N_EXPERTS), 0.01),
        "w_gate_e": nrm(ks[16], (L, N_EXPERTS, D_MODEL, D_EXPERT), D_MODEL ** -0.5),
        "w_up_e": nrm(ks[17], (L, N_EXPERTS, D_MODEL, D_EXPERT), D_MODEL ** -0.5),
        "w_down_e": nrm(ks[18], (L, N_EXPERTS, D_EXPERT, D_MODEL), D_EXPERT ** -0.5 * BETA),
        "w_gate_s": nrm(ks[19], (L, D_MODEL, D_SHARED), D_MODEL ** -0.5),
        "w_up_s": nrm(ks[20], (L, D_MODEL, D_SHARED), D_MODEL ** -0.5),
        "w_down_s": nrm(ks[21], (L, D_SHARED, D_MODEL), D_SHARED ** -0.5 * BETA),
        "w_pg": nrm(ks[22], (L, D_MODEL, D_MODEL), D_MODEL ** -0.5),
        "b_pg": nrm(ks[23], (L, D_MODEL), 0.02),
        "w_p": nrm(ks[24], (L, P_DIM, D_MODEL), P_DIM ** -0.5 * BETA),
        "ln2_g": 1.0 + nrm(ks[25], (L, D_MODEL), 0.02),
        "ln2_b": nrm(ks[26], (L, D_MODEL), 0.02),
    }


def reference(x, p, w_in, b_in, sg_ln_g, sg_ln_b, w_s, b_s, mh_g, w_pa, w_pb, w_out,
              ln1_g, ln1_b, w_router, b_router, w_gate_e, w_up_e, w_down_e,
              w_gate_s, w_up_s, w_down_s, w_pg, b_pg, w_p, ln2_g, ln2_b):
    for l in range(DEPTH):
        mix = token_mixer(x, w_in[l], b_in[l], sg_ln_g[l], sg_ln_b[l], w_s[l], b_s[l],
                          mh_g[l], w_pa[l], w_pb[l], w_out[l])
        x = layer_norm(ALPHA * x + mix, ln1_g[l], ln1_b[l])
        ffn = moe(x, w_router[l], b_router[l], w_gate_e[l], w_up_e[l], w_down_e[l],
                  w_gate_s[l], w_up_s[l], w_down_s[l])
        ple = jax.nn.sigmoid(x @ w_pg[l] + b_pg[l]) * (p[l] @ w_p[l])
        x = layer_norm(ALPHA * x + ffn + ple, ln2_g[l], ln2_b[l])
    return x
```

```python
import functools

import jax
import jax.numpy as jnp
from jax import lax
from jax.experimental import pallas as pl
from jax.experimental.pallas import tpu as pltpu

D_MODEL = 1024
DEPTH = 4
CHUNK = 64
P_DIM = 256
G_WIDTH = 1024
G_GROUPS = 8
G_GROUP_DIM = G_WIDTH // G_GROUPS
G_BLOCK = 128
M_HEADS = 4
M_QK_DIM = 128
M_V_DIM = 256
M_QK = M_HEADS * M_QK_DIM
M_V = M_HEADS * M_V_DIM
N_EXPERTS = 64
TOP_K = 8
N_GROUPS = 8
TOPK_GROUPS = 4
GROUP_SIZE = N_EXPERTS // N_GROUPS
D_EXPERT = 256
D_SHARED = 256
ROUTE_SCALE = 2.5
ALPHA = (2 * DEPTH) ** 0.25
LN_EPS = 1e-5
RMS_EPS = 1e-6

LANES = 128
VMEM_LIMIT = 56 * 1024 * 1024

F32 = jnp.float32
BF16 = jnp.bfloat16


def _params(*semantics):
    return pltpu.CompilerParams(dimension_semantics=semantics, vmem_limit_bytes=VMEM_LIMIT)


def _layer_norm(x, g, b):
    mu = jnp.mean(x, axis=-1, keepdims=True)
    xc = x - mu
    var = jnp.mean(xc * xc, axis=-1, keepdims=True)
    return xc * lax.rsqrt(var + LN_EPS) * g + b


def _gelu(x):
    return 0.5 * x * (1.0 + lax.erf(x * (2.0 ** -0.5)))


def _proj_kernel(x_ref, w_ref, b_ref, o_ref, *, act):
    acc = jnp.dot(x_ref[...], w_ref[...], preferred_element_type=F32) + b_ref[...]
    if act == "gelu":
        acc = _gelu(acc)
    elif act == "sigmoid":
        acc = jax.nn.sigmoid(acc)
    o_ref[...] = acc.astype(o_ref.dtype)


def _proj(x, w, b, act, out_dtype, tm=1024, tn=1024):
    n, k = x.shape
    nout = w.shape[1]
    tn = min(tn, nout)
    return pl.pallas_call(
        functools.partial(_proj_kernel, act=act),
        grid=(n // tm, nout // tn),
        in_specs=[
            pl.BlockSpec((tm, k), lambda i, j: (i, 0)),
            pl.BlockSpec((k, tn), lambda i, j: (0, j)),
            pl.BlockSpec((1, tn), lambda i, j: (0, j)),
        ],
        out_specs=pl.BlockSpec((tm, tn), lambda i, j: (i, j)),
        out_shape=jax.ShapeDtypeStruct((n, nout), out_dtype),
        name="proj_" + (act or "linear"),
        compiler_params=_params("parallel", "arbitrary"),
    )(x, w, b)


def _sgu_kernel(uv_ref, lng_ref, lnb_ref, ws_ref, bs_ref, a_ref, *, blocks):
    v = uv_ref[:, G_WIDTH:].astype(F32)
    vln = _layer_norm(v, lng_ref[...], lnb_ref[...]).astype(BF16)
    t_chunk = lax.broadcasted_iota(jnp.int32, (G_BLOCK, G_BLOCK), 0) // CHUNK
    s_chunk = lax.broadcasted_iota(jnp.int32, (G_BLOCK, G_BLOCK), 1) // CHUNK
    causal = s_chunk <= t_chunk
    for g in range(G_GROUPS):
        w = jnp.where(causal, ws_ref[g], 0.0).astype(BF16)
        bias = bs_ref[:, g:g + 1]
        cs = slice(g * G_GROUP_DIM, (g + 1) * G_GROUP_DIM)
        for blk in range(blocks):
            rs = slice(blk * G_BLOCK, (blk + 1) * G_BLOCK)
            mixed = jnp.dot(w, vln[rs, cs], preferred_element_type=F32) + bias
            a_ref[rs, cs] = (uv_ref[rs, cs].astype(F32) * mixed).astype(a_ref.dtype)


def _sgu(uv, ln_g, ln_b, w_s, b_s_t, blocks=4):
    n = uv.shape[0]
    tp = blocks * G_BLOCK
    return pl.pallas_call(
        functools.partial(_sgu_kernel, blocks=blocks),
        grid=(n // tp,),
        in_specs=[
            pl.BlockSpec((tp, 2 * G_WIDTH), lambda i: (i, 0)),
            pl.BlockSpec((1, G_WIDTH), lambda i: (0, 0)),
            pl.BlockSpec((1, G_WIDTH), lambda i: (0, 0)),
            pl.BlockSpec((G_GROUPS, G_BLOCK, G_BLOCK), lambda i: (0, 0, 0)),
            pl.BlockSpec((G_BLOCK, G_GROUPS), lambda i: (0, 0)),
        ],
        out_specs=pl.BlockSpec((tp, G_WIDTH), lambda i: (i, 0)),
        out_shape=jax.ShapeDtypeStruct((n, G_WIDTH), BF16),
        name="spatial_gating",
        compiler_params=_params("parallel"),
    )(uv, ln_g, ln_b, w_s, b_s_t)


def _segment_cumsum(x, axis, seg):
    pos = lax.broadcasted_iota(jnp.int32, x.shape, axis) % seg
    shift = 1
    while shift < seg:
        x = x + jnp.where(pos >= shift, pltpu.roll(x, shift, axis), 0.0)
        shift *= 2
    return x


def _mlstm_kernel(qkv_ref, gc_ref, gr_ref, osig_ref, mhg_ref, mb_ref, c_ref, n_ref, m_ref, *, chunks):
    @pl.when(pl.program_id(1) == 0)
    def _():
        c_ref[...] = jnp.zeros_like(c_ref)
        n_ref[...] = jnp.zeros_like(n_ref)
        m_ref[...] = jnp.zeros_like(m_ref)

    scale = M_QK_DIM ** -0.5
    gc = gc_ref[...]
    gr = gr_ref[0]
    b_cols = _segment_cumsum(jax.nn.log_sigmoid(gc), 0, CHUNK)
    b_rows = _segment_cumsum(jax.nn.log_sigmoid(gr), 1, CHUNK)
    t_idx = lax.broadcasted_iota(jnp.int32, (CHUNK, CHUNK), 0)
    s_idx = lax.broadcasted_iota(jnp.int32, (CHUNK, CHUNK), 1)
    tri = s_idx <= t_idx

    for h in range(M_HEADS):
        c_state = c_ref[h]
        n_state = n_ref[h]
        m_state = m_ref[h][:, 0:1]
        for c in range(chunks):
            rs = slice(c * CHUNK, (c + 1) * CHUNK)
            q = qkv_ref[rs, h * M_QK_DIM:(h + 1) * M_QK_DIM]
            k = qkv_ref[rs, M_QK + h * M_QK_DIM:M_QK + (h + 1) * M_QK_DIM]
            v = qkv_ref[rs, 2 * M_QK + h * M_V_DIM:2 * M_QK + (h + 1) * M_V_DIM]
            i_col = gc[rs, h:h + 1]
            b_col = b_cols[rs, M_HEADS + h:M_HEADS + h + 1]
            i_row = gr[h:h + 1, rs]
            b_row = b_rows[M_HEADS + h:M_HEADS + h + 1, rs]

            d = jnp.where(tri, b_col - (b_row - i_row), -jnp.inf)
            inter = b_col + m_state
            m_t = jnp.maximum(inter, jnp.max(d, axis=1, keepdims=True))
            w_intra = jnp.exp(d - m_t)
            w_inter = jnp.exp(inter - m_t)
            qk = lax.dot_general(q, k, (((1,), (1,)), ((), ())), preferred_element_type=F32)
            s = qk * scale * w_intra
            qc = jnp.dot(q, c_state.astype(BF16), preferred_element_type=F32) * scale
            num = jnp.dot(s.astype(BF16), v, preferred_element_type=F32) + w_inter * qc
            qn = jnp.sum(q.astype(F32) * n_state, axis=1, keepdims=True) * scale
            den = jnp.sum(s, axis=1, keepdims=True) + w_inter * qn
            hv = num * (1.0 / jnp.maximum(jnp.abs(den), jnp.exp(-m_t)))
            hv = hv * lax.rsqrt(jnp.mean(hv * hv, axis=1, keepdims=True) + RMS_EPS)
            vs = slice(h * M_V_DIM, (h + 1) * M_V_DIM)
            mb_ref[rs, vs] = (osig_ref[rs, vs].astype(F32) * (hv * mhg_ref[:, vs])).astype(mb_ref.dtype)

            b_last = b_col[CHUNK - 1:CHUNK, :]
            g_col = b_last - b_col + i_col
            m_new = jnp.maximum(b_last + m_state, jnp.max(g_col, axis=0, keepdims=True))
            decay = jnp.exp(b_last + m_state - m_new)
            wk = jnp.exp(g_col - m_new) * k.astype(F32)
            c_state = decay * c_state + jnp.dot(wk.T.astype(BF16), v, preferred_element_type=F32)
            n_state = decay * n_state + jnp.sum(wk, axis=0, keepdims=True)
            m_state = m_new
        c_ref[h] = c_state
        n_ref[h] = n_state
        m_ref[h] = jnp.broadcast_to(m_state, (1, LANES))


def _mlstm(qkv, gates_col, gates_row, osig_src, mh_g, batch, seq, chunks=4):
    ts = chunks * CHUNK
    tiles = seq // ts
    n = batch * seq
    return pl.pallas_call(
        functools.partial(_mlstm_kernel, chunks=chunks),
        grid=(batch, tiles),
        in_specs=[
            pl.BlockSpec((ts, 2 * M_QK + M_V), lambda b, t: (b * tiles + t, 0)),
            pl.BlockSpec((ts, LANES), lambda b, t: (b * tiles + t, 0)),
            pl.BlockSpec((1, 2 * M_HEADS, ts), lambda b, t: (b, 0, t)),
            pl.BlockSpec((ts, M_V), lambda b, t: (b * tiles + t, 0)),
            pl.BlockSpec((1, M_V), lambda b, t: (0, 0)),
        ],
        out_specs=pl.BlockSpec((ts, M_V), lambda b, t: (b * tiles + t, 0)),
        out_shape=jax.ShapeDtypeStruct((n, M_V), BF16),
        scratch_shapes=[
            pltpu.VMEM((M_HEADS, M_QK_DIM, M_V_DIM), F32),
            pltpu.VMEM((M_HEADS, 1, M_QK_DIM), F32),
            pltpu.VMEM((M_HEADS, 1, LANES), F32),
        ],
        name="mlstm",
        compiler_params=_params("parallel", "arbitrary"),
    )(qkv, gates_col, gates_row, osig_src, mh_g)


def _merge_kernel(x_ref, a_ref, mb_ref, ga_ref, gb_ref, wpa_ref, wpb_ref, wout_ref, g_ref, b_ref,
                  xo_ref, xob_ref):
    ya = jnp.dot(a_ref[...], wpa_ref[...], preferred_element_type=F32)
    yb = jnp.dot(mb_ref[...], wpb_ref[...], preferred_element_type=F32)
    y = ga_ref[...].astype(F32) * ya + gb_ref[...].astype(F32) * yb
    mix = jnp.dot(y.astype(BF16), wout_ref[...], preferred_element_type=F32)
    xn = _layer_norm(ALPHA * x_ref[...] + mix, g_ref[...], b_ref[...])
    xo_ref[...] = xn
    xob_ref[...] = xn.astype(BF16)


def _merge(x, a, mb, gates, w_pa, w_pb, w_out, ln_g, ln_b, tm=512):
    n = x.shape[0]
    row = lambda i: (i, 0)
    const = lambda i: (0, 0)
    wspec = pl.BlockSpec((D_MODEL, D_MODEL), const)
    vspec = pl.BlockSpec((1, D_MODEL), const)
    return pl.pallas_call(
        _merge_kernel,
        grid=(n // tm,),
        in_specs=[
            pl.BlockSpec((tm, D_MODEL), row),
            pl.BlockSpec((tm, G_WIDTH), row),
            pl.BlockSpec((tm, M_V), row),
            pl.BlockSpec((tm, D_MODEL), lambda i: (i, 1)),
            pl.BlockSpec((tm, D_MODEL), lambda i: (i, 2)),
            wspec, wspec, wspec, vspec, vspec,
        ],
        out_specs=[pl.BlockSpec((tm, D_MODEL), row), pl.BlockSpec((tm, D_MODEL), row)],
        out_shape=[jax.ShapeDtypeStruct((n, D_MODEL), F32), jax.ShapeDtypeStruct((n, D_MODEL), BF16)],
        name="merge_ln",
        compiler_params=_params("parallel"),
    )(x, a, mb, gates, gates, w_pa, w_pb, w_out, ln_g, ln_b)


def _first_max(v, idx, axes, sentinel):
    m = jnp.max(v, axis=axes, keepdims=True)
    first = jnp.min(jnp.where(v == m, idx, sentinel), axis=axes, keepdims=True)
    return m, first


def _router_kernel(x_ref, wr_ref, br_ref, comb_ref, *, tm):
    logits = lax.dot_general(wr_ref[...], x_ref[...], (((1,), (1,)), ((), ())),
                             preferred_element_type=F32, precision=lax.Precision.HIGHEST)
    scores = jax.nn.sigmoid(logits)
    sel = (scores + br_ref[...]).reshape(N_GROUPS, GROUP_SIZE, tm)
    scores3 = scores.reshape(N_GROUPS, GROUP_SIZE, tm)
    member = lax.broadcasted_iota(jnp.int32, sel.shape, 1)
    group = lax.broadcasted_iota(jnp.int32, sel.shape, 0)
    neg = -jnp.inf

    m1, f1 = _first_max(sel, member, 1, GROUP_SIZE)
    m2 = jnp.max(jnp.where(member == f1, neg, sel), axis=1, keepdims=True)
    gscore = m1 + m2
    gid = lax.broadcasted_iota(jnp.int32, gscore.shape, 0)
    gmask = jnp.zeros(gscore.shape, dtype=jnp.bool_)
    for _ in range(TOPK_GROUPS):
        _, fg = _first_max(gscore, gid, 0, N_GROUPS)
        hit = gid == fg
        gmask = gmask | hit
        gscore = jnp.where(hit, neg, gscore)

    cand = jnp.where(gmask, sel, neg)
    eid = group * GROUP_SIZE + member
    chosen = jnp.zeros(sel.shape, dtype=jnp.bool_)
    for _ in range(TOP_K):
        _, fe = _first_max(cand, eid, (0, 1), N_EXPERTS)
        hit = eid == fe
        chosen = chosen | hit
        cand = jnp.where(hit, neg, cand)

    wsel = jnp.where(chosen, scores3, 0.0)
    total = jnp.sum(wsel, axis=(0, 1), keepdims=True)
    comb = (wsel / total * ROUTE_SCALE).reshape(N_EXPERTS, tm)
    comb = jnp.concatenate([comb, jnp.zeros((LANES - N_EXPERTS, tm), F32)], axis=0)
    comb_ref[...] = comb.T


def _router(x, wr_t, br, tm=512):
    n = x.shape[0]
    return pl.pallas_call(
        functools.partial(_router_kernel, tm=tm),
        grid=(n // tm,),
        in_specs=[
            pl.BlockSpec((tm, D_MODEL), lambda i: (i, 0)),
            pl.BlockSpec((N_EXPERTS, D_MODEL), lambda i: (0, 0)),
            pl.BlockSpec((N_EXPERTS, 1), lambda i: (0, 0)),
        ],
        out_specs=pl.BlockSpec((tm, LANES), lambda i: (i, 0)),
        out_shape=jax.ShapeDtypeStruct((n, LANES), F32),
        name="router",
        compiler_params=_params("parallel"),
    )(x, wr_t, br)


def _silu(x):
    return x * jax.nn.sigmoid(x)


def _moe_kernel(x_ref, xb_ref, p_ref, comb_ref, wg_ref, wu_ref, wd_ref, wgs_ref, wus_ref, wds_ref,
                wpg_ref, bpg_ref, wp_ref, g_ref, b_ref, xo_ref, xob_ref, acc_ref):
    e = pl.program_id(1)
    xb = xb_ref[...]

    @pl.when(e == 0)
    def _():
        hs = _silu(jnp.dot(xb, wgs_ref[...], preferred_element_type=F32)) * jnp.dot(
            xb, wus_ref[...], preferred_element_type=F32)
        shared = jnp.dot(hs.astype(BF16), wds_ref[...], preferred_element_type=F32)
        gate = jax.nn.sigmoid(jnp.dot(xb, wpg_ref[...], preferred_element_type=F32) + bpg_ref[...])
        ple = gate * jnp.dot(p_ref[...].astype(BF16), wp_ref[...], preferred_element_type=F32)
        acc_ref[...] = shared + ple

    lane = lax.broadcasted_iota(jnp.int32, comb_ref.shape, 1)
    col = jnp.sum(jnp.where(lane == e, comb_ref[...], 0.0), axis=1, keepdims=True)
    hg = jnp.dot(xb, wg_ref[0], preferred_element_type=F32)
    hu = jnp.dot(xb, wu_ref[0], preferred_element_type=F32)
    hid = _silu(hg) * hu * col
    acc_ref[...] += jnp.dot(hid.astype(BF16), wd_ref[0], preferred_element_type=F32)

    @pl.when(e == N_EXPERTS - 1)
    def _():
        xn = _layer_norm(ALPHA * x_ref[...] + acc_ref[...], g_ref[...], b_ref[...])
        xo_ref[...] = xn
        xob_ref[...] = xn.astype(BF16)


def _moe(x, xb, p, comb, wg, wu, wd, wgs, wus, wds, wpg, bpg, wp, ln_g, ln_b, tm=512):
    n = x.shape[0]
    row = lambda i, e: (i, 0)
    const = lambda i, e: (0, 0)
    exp = lambda i, e: (e, 0, 0)
    return pl.pallas_call(
        _moe_kernel,
        grid=(n // tm, N_EXPERTS),
        in_specs=[
            pl.BlockSpec((tm, D_MODEL), row),
            pl.BlockSpec((tm, D_MODEL), row),
            pl.BlockSpec((tm, P_DIM), row),
            pl.BlockSpec((tm, LANES), row),
            pl.BlockSpec((1, D_MODEL, D_EXPERT), exp),
            pl.BlockSpec((1, D_MODEL, D_EXPERT), exp),
            pl.BlockSpec((1, D_EXPERT, D_MODEL), exp),
            pl.BlockSpec((D_MODEL, D_SHARED), const),
            pl.BlockSpec((D_MODEL, D_SHARED), const),
            pl.BlockSpec((D_SHARED, D_MODEL), const),
            pl.BlockSpec((D_MODEL, D_MODEL), const),
            pl.BlockSpec((1, D_MODEL), const),
            pl.BlockSpec((P_DIM, D_MODEL), const),
            pl.BlockSpec((1, D_MODEL), const),
            pl.BlockSpec((1, D_MODEL), const),
        ],
        out_specs=[pl.BlockSpec((tm, D_MODEL), row), pl.BlockSpec((tm, D_MODEL), row)],
        out_shape=[jax.ShapeDtypeStruct((n, D_MODEL), F32), jax.ShapeDtypeStruct((n, D_MODEL), BF16)],
        scratch_shapes=[pltpu.VMEM((tm, D_MODEL), F32)],
        name="moe_dense_ln",
        compiler_params=_params("parallel", "arbitrary"),
    )(x, xb, p, comb, wg, wu, wd, wgs, wus, wds, wpg, bpg, wp, ln_g, ln_b)


def kernel(x, p, w_in, b_in, sg_ln_g, sg_ln_b, w_s, b_s, mh_g, w_pa, w_pb, w_out, ln1_g, ln1_b, w_router, b_router, w_gate_e, w_up_e, w_down_e, w_gate_s, w_up_s, w_down_s, w_pg, b_pg, w_p, ln2_g, ln2_b):
    batch, seq, _ = x.shape
    n = batch * seq
    xf = x.reshape(n, D_MODEL)
    xb = xf.astype(BF16)
    pf = p.reshape(DEPTH, n, P_DIM)

    c_uv = 2 * G_WIDTH
    c_qkv = c_uv + 2 * M_QK + M_V
    c_o = c_qkv + M_V
    c_if = c_o + 2 * M_HEADS

    for l in range(DEPTH):
        w = w_in[l]
        b = b_in[l][None, :]
        w_gate = jnp.concatenate([w[:, c_qkv:c_o], w[:, c_if:]], axis=1).astype(BF16)
        b_gate = jnp.concatenate([b[:, c_qkv:c_o], b[:, c_if:]], axis=1)
        w_if = jnp.pad(w[:, c_o:c_if], ((0, 0), (0, LANES - 2 * M_HEADS))).astype(BF16)
        b_if = jnp.pad(b[:, c_o:c_if], ((0, 0), (0, LANES - 2 * M_HEADS)))

        uv = _proj(xb, w[:, :c_uv].astype(BF16), b[:, :c_uv], "gelu", BF16)
        qkv = _proj(xb, w[:, c_uv:c_qkv].astype(BF16), b[:, c_uv:c_qkv], None, BF16)
        gates = _proj(xb, w_gate, b_gate, "sigmoid", BF16)
        gif = _proj(xb, w_if, b_if, None, F32)
        gif_rows = gif[:, :2 * M_HEADS].reshape(batch, seq, 2 * M_HEADS).transpose(0, 2, 1)

        a = _sgu(uv, sg_ln_g[l][None, :], sg_ln_b[l][None, :], w_s[l], b_s[l].T)
        mb = _mlstm(qkv, gif, gif_rows, gates, mh_g[l][None, :], batch, seq)
        xf, xb = _merge(xf, a, mb, gates, w_pa[l].astype(BF16), w_pb[l].astype(BF16),
                        w_out[l].astype(BF16), ln1_g[l][None, :], ln1_b[l][None, :])

        comb = _router(xf, w_router[l].T, b_router[l][:, None])
        xf, xb = _moe(xf, xb, pf[l], comb,
                      w_gate_e[l].astype(BF16), w_up_e[l].astype(BF16), w_down_e[l].astype(BF16),
                      w_gate_s[l].astype(BF16), w_up_s[l].astype(BF16), w_down_s[l].astype(BF16),
                      w_pg[l].astype(BF16), b_pg[l][None, :], w_p[l].astype(BF16),
                      ln2_g[l][None, :], ln2_b[l][None, :])
    return xf.reshape(batch, seq, D_MODEL)
```

```python
import functools

import jax
import jax.numpy as jnp
from jax import lax
from jax.experimental import pallas as pl
from jax.experimental.pallas import tpu as pltpu
from jax.experimental.pallas import tpu_sc as plsc

D_MODEL = 1024
DEPTH = 4
CHUNK = 64
P_DIM = 256
G_WIDTH = 1024
G_GROUPS = 8
G_GROUP_DIM = G_WIDTH // G_GROUPS
G_BLOCK = 128
M_HEADS = 4
M_QK_DIM = 128
M_V_DIM = 256
M_QK = M_HEADS * M_QK_DIM
M_V = M_HEADS * M_V_DIM
N_EXPERTS = 64
TOP_K = 8
N_GROUPS = 8
TOPK_GROUPS = 4
GROUP_SIZE = N_EXPERTS // N_GROUPS
D_EXPERT = 256
D_SHARED = 256
ROUTE_SCALE = 2.5
ALPHA = (2 * DEPTH) ** 0.25
LN_EPS = 1e-5
RMS_EPS = 1e-6

LANES = 128
VMEM_LIMIT = 56 * 1024 * 1024
HALF = D_MODEL // 2
PART = HALF // 2
ROW_TILE = 512
SC_WINDOW = 128

F32 = jnp.float32
BF16 = jnp.bfloat16
I32 = jnp.int32


def _params(*semantics):
    return pltpu.CompilerParams(dimension_semantics=semantics, vmem_limit_bytes=VMEM_LIMIT)


def _layer_norm(x, g, b):
    mu = jnp.mean(x, axis=-1, keepdims=True)
    xc = x - mu
    var = jnp.mean(xc * xc, axis=-1, keepdims=True)
    return xc * lax.rsqrt(var + LN_EPS) * g + b


def _gelu(x):
    return 0.5 * x * (1.0 + lax.erf(x * (2.0 ** -0.5)))


def _silu(x):
    return x * jax.nn.sigmoid(x)


def _pack_pairs(x):
    lo = lax.bitcast_convert_type(x[:, :PART].astype(BF16).astype(F32), I32)
    hi = lax.bitcast_convert_type(x[:, PART:].astype(BF16).astype(F32), I32)
    return lax.shift_right_logical(lo, 16) | (hi & jnp.int32(-65536))


def _pack_row(x):
    return _pack_pairs(x[:, :HALF]), _pack_pairs(x[:, HALF:])


def _unpack_pairs(w):
    lo = lax.bitcast_convert_type(lax.shift_left(w, 16), F32)
    hi = lax.bitcast_convert_type(w & jnp.int32(-65536), F32)
    return jnp.concatenate([lo, hi], axis=1)


def _proj_kernel(x_ref, w_ref, b_ref, o_ref, *, act):
    acc = jnp.dot(x_ref[...], w_ref[...], preferred_element_type=F32) + b_ref[...]
    if act == "gelu":
        acc = _gelu(acc)
    elif act == "sigmoid":
        acc = jax.nn.sigmoid(acc)
    o_ref[...] = acc.astype(o_ref.dtype)


def _proj(x, w, b, act, out_dtype, tm=1024, tn=1024):
    n, k = x.shape
    nout = w.shape[1]
    tn = min(tn, nout)
    return pl.pallas_call(
        functools.partial(_proj_kernel, act=act),
        grid=(n // tm, nout // tn),
        in_specs=[
            pl.BlockSpec((tm, k), lambda i, j: (i, 0)),
            pl.BlockSpec((k, tn), lambda i, j: (0, j)),
            pl.BlockSpec((1, tn), lambda i, j: (0, j)),
        ],
        out_specs=pl.BlockSpec((tm, tn), lambda i, j: (i, j)),
        out_shape=jax.ShapeDtypeStruct((n, nout), out_dtype),
        name="proj_" + (act or "linear"),
        compiler_params=_params("parallel", "arbitrary"),
    )(x, w, b)


def _sgu_kernel(uv_ref, lng_ref, lnb_ref, ws_ref, bs_ref, a_ref, *, blocks):
    v = uv_ref[:, G_WIDTH:].astype(F32)
    vln = _layer_norm(v, lng_ref[...], lnb_ref[...]).astype(BF16)
    t_chunk = lax.broadcasted_iota(jnp.int32, (G_BLOCK, G_BLOCK), 0) // CHUNK
    s_chunk = lax.broadcasted_iota(jnp.int32, (G_BLOCK, G_BLOCK), 1) // CHUNK
    causal = s_chunk <= t_chunk
    for g in range(G_GROUPS):
        w = jnp.where(causal, ws_ref[g], 0.0).astype(BF16)
        bias = bs_ref[:, g:g + 1]
        cs = slice(g * G_GROUP_DIM, (g + 1) * G_GROUP_DIM)
        for blk in range(blocks):
            rs = slice(blk * G_BLOCK, (blk + 1) * G_BLOCK)
            mixed = jnp.dot(w, vln[rs, cs], preferred_element_type=F32) + bias
            a_ref[rs, cs] = (uv_ref[rs, cs].astype(F32) * mixed).astype(a_ref.dtype)


def _sgu(uv, ln_g, ln_b, w_s, b_s_t, blocks=4):
    n = uv.shape[0]
    tp = blocks * G_BLOCK
    return pl.pallas_call(
        functools.partial(_sgu_kernel, blocks=blocks),
        grid=(n // tp,),
        in_specs=[
            pl.BlockSpec((tp, 2 * G_WIDTH), lambda i: (i, 0)),
            pl.BlockSpec((1, G_WIDTH), lambda i: (0, 0)),
            pl.BlockSpec((1, G_WIDTH), lambda i: (0, 0)),
            pl.BlockSpec((G_GROUPS, G_BLOCK, G_BLOCK), lambda i: (0, 0, 0)),
            pl.BlockSpec((G_BLOCK, G_GROUPS), lambda i: (0, 0)),
        ],
        out_specs=pl.BlockSpec((tp, G_WIDTH), lambda i: (i, 0)),
        out_shape=jax.ShapeDtypeStruct((n, G_WIDTH), BF16),
        name="spatial_gating",
        compiler_params=_params("parallel"),
    )(uv, ln_g, ln_b, w_s, b_s_t)


def _segment_cumsum(x, axis, seg):
    pos = lax.broadcasted_iota(jnp.int32, x.shape, axis) % seg
    shift = 1
    while shift < seg:
        x = x + jnp.where(pos >= shift, pltpu.roll(x, shift, axis), 0.0)
        shift *= 2
    return x


def _mlstm_kernel(qkv_ref, gc_ref, gr_ref, osig_ref, mhg_ref, mb_ref, c_ref, n_ref, m_ref, *, chunks):
    @pl.when(pl.program_id(1) == 0)
    def _():
        c_ref[...] = jnp.zeros_like(c_ref)
        n_ref[...] = jnp.zeros_like(n_ref)
        m_ref[...] = jnp.zeros_like(m_ref)

    scale = M_QK_DIM ** -0.5
    gc = gc_ref[...]
    gr = gr_ref[0]
    b_cols = _segment_cumsum(jax.nn.log_sigmoid(gc), 0, CHUNK)
    b_rows = _segment_cumsum(jax.nn.log_sigmoid(gr), 1, CHUNK)
    t_idx = lax.broadcasted_iota(jnp.int32, (CHUNK, CHUNK), 0)
    s_idx = lax.broadcasted_iota(jnp.int32, (CHUNK, CHUNK), 1)
    tri = s_idx <= t_idx

    for h in range(M_HEADS):
        c_state = c_ref[h]
        n_state = n_ref[h]
        m_state = m_ref[h][:, 0:1]
        for c in range(chunks):
            rs = slice(c * CHUNK, (c + 1) * CHUNK)
            q = qkv_ref[rs, h * M_QK_DIM:(h + 1) * M_QK_DIM]
            k = qkv_ref[rs, M_QK + h * M_QK_DIM:M_QK + (h + 1) * M_QK_DIM]
            v = qkv_ref[rs, 2 * M_QK + h * M_V_DIM:2 * M_QK + (h + 1) * M_V_DIM]
            i_col = gc[rs, h:h + 1]
            b_col = b_cols[rs, M_HEADS + h:M_HEADS + h + 1]
            i_row = gr[h:h + 1, rs]
            b_row = b_rows[M_HEADS + h:M_HEADS + h + 1, rs]

            d = jnp.where(tri, b_col - (b_row - i_row), -jnp.inf)
            inter = b_col + m_state
            m_t = jnp.maximum(inter, jnp.max(d, axis=1, keepdims=True))
            w_intra = jnp.exp(d - m_t)
            w_inter = jnp.exp(inter - m_t)
            qk = lax.dot_general(q, k, (((1,), (1,)), ((), ())), preferred_element_type=F32)
            s = qk * scale * w_intra
            qc = jnp.dot(q, c_state.astype(BF16), preferred_element_type=F32) * scale
            num = jnp.dot(s.astype(BF16), v, preferred_element_type=F32) + w_inter * qc
            qn = jnp.sum(q.astype(F32) * n_state, axis=1, keepdims=True) * scale
            den = jnp.sum(s, axis=1, keepdims=True) + w_inter * qn
            hv = num * (1.0 / jnp.maximum(jnp.abs(den), jnp.exp(-m_t)))
            hv = hv * lax.rsqrt(jnp.mean(hv * hv, axis=1, keepdims=True) + RMS_EPS)
            vs = slice(h * M_V_DIM, (h + 1) * M_V_DIM)
            mb_ref[rs, vs] = (osig_ref[rs, vs].astype(F32) * (hv * mhg_ref[:, vs])).astype(mb_ref.dtype)

            b_last = b_col[CHUNK - 1:CHUNK, :]
            g_col = b_last - b_col + i_col
            m_new = jnp.maximum(b_last + m_state, jnp.max(g_col, axis=0, keepdims=True))
            decay = jnp.exp(b_last + m_state - m_new)
            wk = jnp.exp(g_col - m_new) * k.astype(F32)
            c_state = decay * c_state + jnp.dot(wk.T.astype(BF16), v, preferred_element_type=F32)
            n_state = decay * n_state + jnp.sum(wk, axis=0, keepdims=True)
            m_state = m_new
        c_ref[h] = c_state
        n_ref[h] = n_state
        m_ref[h] = jnp.broadcast_to(m_state, (1, LANES))


def _mlstm(qkv, gates_col, gates_row, osig_src, mh_g, batch, seq, chunks=4):
    ts = chunks * CHUNK
    tiles = seq // ts
    n = batch * seq
    return pl.pallas_call(
        functools.partial(_mlstm_kernel, chunks=chunks),
        grid=(batch, tiles),
        in_specs=[
            pl.BlockSpec((ts, 2 * M_QK + M_V), lambda b, t: (b * tiles + t, 0)),
            pl.BlockSpec((ts, LANES), lambda b, t: (b * tiles + t, 0)),
            pl.BlockSpec((1, 2 * M_HEADS, ts), lambda b, t: (b, 0, t)),
            pl.BlockSpec((ts, M_V), lambda b, t: (b * tiles + t, 0)),
            pl.BlockSpec((1, M_V), lambda b, t: (0, 0)),
        ],
        out_specs=pl.BlockSpec((ts, M_V), lambda b, t: (b * tiles + t, 0)),
        out_shape=jax.ShapeDtypeStruct((n, M_V), BF16),
        scratch_shapes=[
            pltpu.VMEM((M_HEADS, M_QK_DIM, M_V_DIM), F32),
            pltpu.VMEM((M_HEADS, 1, M_QK_DIM), F32),
            pltpu.VMEM((M_HEADS, 1, LANES), F32),
        ],
        name="mlstm",
        compiler_params=_params("parallel", "arbitrary"),
    )(qkv, gates_col, gates_row, osig_src, mh_g)


def _merge_kernel(x_ref, a_ref, mb_ref, ga_ref, gb_ref, wpa_ref, wpb_ref, wout_ref, g_ref, b_ref,
                  xo_ref, xob_ref, xpa_ref, xpb_ref):
    ya = jnp.dot(a_ref[...], wpa_ref[...], preferred_element_type=F32)
    yb = jnp.dot(mb_ref[...], wpb_ref[...], preferred_element_type=F32)
    y = ga_ref[...].astype(F32) * ya + gb_ref[...].astype(F32) * yb
    mix = jnp.dot(y.astype(BF16), wout_ref[...], preferred_element_type=F32)
    xn = _layer_norm(ALPHA * x_ref[...] + mix, g_ref[...], b_ref[...])
    xo_ref[...] = xn
    xob_ref[...] = xn.astype(BF16)
    xpa_ref[...], xpb_ref[...] = _pack_row(xn)


def _merge(x, a, mb, gates, w_pa, w_pb, w_out, ln_g, ln_b, tm=512):
    n = x.shape[0]
    row = lambda i: (i, 0)
    const = lambda i: (0, 0)
    wspec = pl.BlockSpec((D_MODEL, D_MODEL), const)
    vspec = pl.BlockSpec((1, D_MODEL), const)
    return pl.pallas_call(
        _merge_kernel,
        grid=(n // tm,),
        in_specs=[
            pl.BlockSpec((tm, D_MODEL), row),
            pl.BlockSpec((tm, G_WIDTH), row),
            pl.BlockSpec((tm, M_V), row),
            pl.BlockSpec((tm, D_MODEL), lambda i: (i, 1)),
            pl.BlockSpec((tm, D_MODEL), lambda i: (i, 2)),
            wspec, wspec, wspec, vspec, vspec,
        ],
        out_specs=[pl.BlockSpec((tm, D_MODEL), row), pl.BlockSpec((tm, D_MODEL), row),
                   pl.BlockSpec((tm, PART), row), pl.BlockSpec((tm, PART), row)],
        out_shape=[jax.ShapeDtypeStruct((n, D_MODEL), F32), jax.ShapeDtypeStruct((n, D_MODEL), BF16),
                   jax.ShapeDtypeStruct((n, PART), I32), jax.ShapeDtypeStruct((n, PART), I32)],
        name="merge_ln",
        compiler_params=_params("parallel"),
    )(x, a, mb, gates, gates, w_pa, w_pb, w_out, ln_g, ln_b)


def _first_max(v, idx, axes, sentinel):
    m = jnp.max(v, axis=axes, keepdims=True)
    first = jnp.min(jnp.where(v == m, idx, sentinel), axis=axes, keepdims=True)
    return m, first


def _router_kernel(x_ref, wr_ref, br_ref, eidx_ref, rank_ref, wtok_ref, cnt_ref, run_ref, *, tm):
    @pl.when(pl.program_id(0) == 0)
    def _():
        run_ref[...] = jnp.zeros_like(run_ref)

    logits = lax.dot_general(wr_ref[...], x_ref[...], (((1,), (1,)), ((), ())),
                             preferred_element_type=F32, precision=lax.Precision.HIGHEST)
    scores = jax.nn.sigmoid(logits)
    sel = (scores + br_ref[...]).reshape(N_GROUPS, GROUP_SIZE, tm)
    scores3 = scores.reshape(N_GROUPS, GROUP_SIZE, tm)
    member = lax.broadcasted_iota(jnp.int32, sel.shape, 1)
    group = lax.broadcasted_iota(jnp.int32, sel.shape, 0)
    neg = -jnp.inf

    m1, f1 = _first_max(sel, member, 1, GROUP_SIZE)
    m2 = jnp.max(jnp.where(member == f1, neg, sel), axis=1, keepdims=True)
    gscore = m1 + m2
    gid = lax.broadcasted_iota(jnp.int32, gscore.shape, 0)
    gmask = jnp.zeros(gscore.shape, dtype=jnp.bool_)
    for _ in range(TOPK_GROUPS):
        _, fg = _first_max(gscore, gid, 0, N_GROUPS)
        hit = gid == fg
        gmask = gmask | hit
        gscore = jnp.where(hit, neg, gscore)

    cand = jnp.where(gmask, sel, neg)
    eid = group * GROUP_SIZE + member
    chosen = jnp.zeros(sel.shape, dtype=jnp.bool_)
    picks = []
    for _ in range(TOP_K):
        _, fe = _first_max(cand, eid, (0, 1), N_EXPERTS)
        hit = eid == fe
        chosen = chosen | hit
        cand = jnp.where(hit, neg, cand)
        picks.append((fe[0], hit, jnp.sum(jnp.where(hit, scores3, 0.0), axis=(0, 1), keepdims=True)[0]))

    chosen2 = jnp.where(chosen, 1.0, 0.0).reshape(N_EXPERTS, tm)
    r_idx = lax.broadcasted_iota(jnp.int32, (tm, tm), 0)
    c_idx = lax.broadcasted_iota(jnp.int32, (tm, tm), 1)
    upper = jnp.where(r_idx < c_idx, 1.0, 0.0).astype(BF16)
    prefix = jnp.dot(chosen2.astype(BF16), upper, preferred_element_type=F32) + run_ref[:, 0:1]
    prefix3 = prefix.reshape(N_GROUPS, GROUP_SIZE, tm)

    total = picks[0][2]
    for _, _, wk in picks[1:]:
        total = total + wk
    eidx_ref[...] = jnp.concatenate([fe for fe, _, _ in picks], axis=0)
    rank_ref[...] = jnp.concatenate(
        [jnp.sum(jnp.where(hit, prefix3, 0.0), axis=(0, 1), keepdims=True)[0] for _, hit, _ in picks],
        axis=0).astype(I32)
    w_rows = jnp.concatenate([wk / total * ROUTE_SCALE for _, _, wk in picks]
                             + [jnp.zeros((LANES - TOP_K, tm), F32)], axis=0)
    wtok_ref[...] = w_rows.T

    run = run_ref[...] + jnp.sum(chosen2, axis=1, keepdims=True)
    run_ref[...] = run
    cnt_ref[...] = run


def _router(x, wr_t, br, tm=512):
    n = x.shape[0]
    tok = lambda i: (0, i)
    return pl.pallas_call(
        functools.partial(_router_kernel, tm=tm),
        grid=(n // tm,),
        in_specs=[
            pl.BlockSpec((tm, D_MODEL), lambda i: (i, 0)),
            pl.BlockSpec((N_EXPERTS, D_MODEL), lambda i: (0, 0)),
            pl.BlockSpec((N_EXPERTS, 1), lambda i: (0, 0)),
        ],
        out_specs=[
            pl.BlockSpec((TOP_K, tm), tok),
            pl.BlockSpec((TOP_K, tm), tok),
            pl.BlockSpec((tm, LANES), lambda i: (i, 0)),
            pl.BlockSpec((N_EXPERTS, LANES), lambda i: (0, 0)),
        ],
        out_shape=[
            jax.ShapeDtypeStruct((TOP_K, n), I32),
            jax.ShapeDtypeStruct((TOP_K, n), I32),
            jax.ShapeDtypeStruct((n, LANES), F32),
            jax.ShapeDtypeStruct((N_EXPERTS, LANES), F32),
        ],
        scratch_shapes=[pltpu.VMEM((N_EXPERTS, LANES), F32)],
        name="router",
        compiler_params=_params("arbitrary"),
    )(x, wr_t, br)


def _slots_kernel(starts_ref, eidx_ref, rank_ref, slot_ref):
    eidx = eidx_ref[...]
    slot = rank_ref[...]
    for e in range(N_EXPERTS):
        slot = slot + jnp.where(eidx == e, starts_ref[e], 0)
    slot_ref[...] = slot


def _slots(starts, eidx, rank, tn=2048):
    n = eidx.shape[1]
    tn = min(tn, n)
    spec = pl.BlockSpec((TOP_K, tn), lambda i, s: (0, i))
    return pl.pallas_call(
        _slots_kernel,
        grid_spec=pltpu.PrefetchScalarGridSpec(
            num_scalar_prefetch=1, grid=(n // tn,), in_specs=[spec, spec], out_specs=spec),
        out_shape=jax.ShapeDtypeStruct((TOP_K, n), I32),
        name="slots",
        compiler_params=_params("parallel"),
    )(starts, eidx, rank)


def _sc_mesh():
    return plsc.VectorSubcoreMesh(core_axis_name="c", subcore_axis_name="s")


def _sc_dispatch(parts, slots_flat, rows):
    n, width = parts[0].shape
    blocks = n // SC_WINDOW
    out_type = [jax.ShapeDtypeStruct((rows, width), part.dtype) for part in parts]

    @functools.partial(pl.kernel, out_type=out_type, mesh=_sc_mesh(), scratch_types=[], name="sc_dispatch")
    def run(*refs):
        i_hbm = refs[len(parts)]
        for x_hbm, o_hbm in zip(refs[:len(parts)], refs[len(parts) + 1:]):
            def body(x_vmem, i_vmem, o_hbm=o_hbm):
                pltpu.sync_copy(x_vmem, o_hbm.at[i_vmem.at[0]])

            pltpu.emit_pipeline(
                body,
                grid=(blocks, TOP_K),
                in_specs=[pl.BlockSpec((SC_WINDOW, width), lambda i, k: (i, 0)),
                          pl.BlockSpec((1, SC_WINDOW), lambda i, k: (0, k * blocks + i))],
                out_specs=[],
                core_axis_name=("c", "s"),
                dimension_semantics=(pltpu.PARALLEL, pltpu.ARBITRARY),
            )(x_hbm, i_hbm)

    return run(*parts, slots_flat)


def _sc_gather(tables, idx_flat):
    count = idx_flat.shape[1]
    width = tables[0].shape[1]
    out_type = [jax.ShapeDtypeStruct((count, width), table.dtype) for table in tables]

    @functools.partial(pl.kernel, out_type=out_type, mesh=_sc_mesh(), scratch_types=[], name="sc_gather")
    def run(*refs):
        i_hbm = refs[len(tables)]
        for t_hbm, o_hbm in zip(refs[:len(tables)], refs[len(tables) + 1:]):
            def body(i_vmem, o_vmem, t_hbm=t_hbm):
                pltpu.sync_copy(t_hbm.at[i_vmem.at[0]], o_vmem)

            pltpu.emit_pipeline(
                body,
                grid=(count // SC_WINDOW,),
                in_specs=[pl.BlockSpec((1, SC_WINDOW), lambda i: (0, i))],
                out_specs=[pl.BlockSpec((SC_WINDOW, width), lambda i: (i, 0))],
                core_axis_name=("c", "s"),
                dimension_semantics=(pltpu.PARALLEL,),
            )(i_hbm, o_hbm)

    return run(*tables, idx_flat)


def _expert_kernel(te_ref, xa_ref, xb_ref, wg_ref, wu_ref, wd_ref, oa_ref, ob_ref):
    del te_ref
    x = jnp.concatenate([_unpack_pairs(xa_ref[...]), _unpack_pairs(xb_ref[...])], axis=1).astype(BF16)
    hg = jnp.dot(x, wg_ref[0], preferred_element_type=F32)
    hu = jnp.dot(x, wu_ref[0], preferred_element_type=F32)
    hid = (_silu(hg) * hu).astype(BF16)
    oa_ref[...], ob_ref[...] = _pack_row(jnp.dot(hid, wd_ref[0], preferred_element_type=F32))


def _experts(tile_expert, xa, xb, wg, wu, wd):
    rows = xa.shape[0]
    row = lambda j, te: (j, 0)
    exp = lambda j, te: (te[j], 0, 0)
    return pl.pallas_call(
        _expert_kernel,
        grid_spec=pltpu.PrefetchScalarGridSpec(
            num_scalar_prefetch=1,
            grid=(rows // ROW_TILE,),
            in_specs=[
                pl.BlockSpec((ROW_TILE, PART), row),
                pl.BlockSpec((ROW_TILE, PART), row),
                pl.BlockSpec((1, D_MODEL, D_EXPERT), exp),
                pl.BlockSpec((1, D_MODEL, D_EXPERT), exp),
                pl.BlockSpec((1, D_EXPERT, D_MODEL), exp),
            ],
            out_specs=[pl.BlockSpec((ROW_TILE, PART), row), pl.BlockSpec((ROW_TILE, PART), row)],
        ),
        out_shape=[jax.ShapeDtypeStruct((rows, PART), I32), jax.ShapeDtypeStruct((rows, PART), I32)],
        name="experts",
        compiler_params=_params("parallel"),
    )(tile_expert, xa, xb, wg, wu, wd)


def _finish_kernel(x_ref, xb_ref, p_ref, ga_ref, gb_ref, wtok_ref, wgs_ref, wus_ref, wds_ref, wpg_ref, bpg_ref, wp_ref,
                   lng_ref, lnb_ref, xo_ref, xob_ref):
    xb = xb_ref[...]
    hs = _silu(jnp.dot(xb, wgs_ref[...], preferred_element_type=F32)) * jnp.dot(
        xb, wus_ref[...], preferred_element_type=F32)
    shared = jnp.dot(hs.astype(BF16), wds_ref[...], preferred_element_type=F32)
    gate = jax.nn.sigmoid(jnp.dot(xb, wpg_ref[...], preferred_element_type=F32) + bpg_ref[...])
    ple = gate * jnp.dot(p_ref[...].astype(BF16), wp_ref[...], preferred_element_type=F32)

    wtok = wtok_ref[...]
    r_a = None
    r_b = None
    for k in range(TOP_K):
        wk = wtok[:, k:k + 1]
        a = wk * _unpack_pairs(ga_ref[k])
        b = wk * _unpack_pairs(gb_ref[k])
        r_a = a if r_a is None else r_a + a
        r_b = b if r_b is None else r_b + b
    routed = jnp.concatenate([r_a, r_b], axis=1)

    xn = _layer_norm(ALPHA * x_ref[...] + (routed + shared + ple), lng_ref[...], lnb_ref[...])
    xo_ref[...] = xn
    xob_ref[...] = xn.astype(BF16)


def _finish(x, xb, p, ga, gb, wtok, wgs, wus, wds, wpg, bpg, wp, ln_g, ln_b, tm=256):
    n = x.shape[0]
    row = lambda i: (i, 0)
    const = lambda i: (0, 0)
    return pl.pallas_call(
        _finish_kernel,
        grid=(n // tm,),
        in_specs=[
            pl.BlockSpec((tm, D_MODEL), row),
            pl.BlockSpec((tm, D_MODEL), row),
            pl.BlockSpec((tm, P_DIM), row),
            pl.BlockSpec((TOP_K, tm, PART), lambda i: (0, i, 0)),
            pl.BlockSpec((TOP_K, tm, PART), lambda i: (0, i, 0)),
            pl.BlockSpec((tm, LANES), row),
            pl.BlockSpec((D_MODEL, D_SHARED), const),
            pl.BlockSpec((D_MODEL, D_SHARED), const),
            pl.BlockSpec((D_SHARED, D_MODEL), const),
            pl.BlockSpec((D_MODEL, D_MODEL), const),
            pl.BlockSpec((1, D_MODEL), const),
            pl.BlockSpec((P_DIM, D_MODEL), const),
            pl.BlockSpec((1, D_MODEL), const),
            pl.BlockSpec((1, D_MODEL), const),
        ],
        out_specs=[pl.BlockSpec((tm, D_MODEL), row), pl.BlockSpec((tm, D_MODEL), row)],
        out_shape=[jax.ShapeDtypeStruct((n, D_MODEL), F32), jax.ShapeDtypeStruct((n, D_MODEL), BF16)],
        name="moe_finish_ln",
        compiler_params=_params("parallel"),
    )(x, xb, p, ga, gb, wtok, wgs, wus, wds, wpg, bpg, wp, ln_g, ln_b)


def _moe(x, xb, xpa, xpb, p, w_router_t, b_router, wg, wu, wd, wgs, wus, wds, wpg, bpg, wp, ln_g, ln_b):
    n = x.shape[0]
    pairs = n * TOP_K
    tiles = pairs // ROW_TILE + N_EXPERTS
    eidx, rank, wtok, counts = _router(x, w_router_t, b_router)

    cnt = counts[:, 0].astype(I32)
    group_tiles = (cnt + ROW_TILE - 1) // ROW_TILE
    tile_end = jnp.cumsum(group_tiles)
    starts = (tile_end - group_tiles) * ROW_TILE
    tile_expert = jnp.minimum(
        jnp.searchsorted(tile_end, jnp.arange(tiles, dtype=I32), side="right"), N_EXPERTS - 1).astype(I32)

    slots_flat = _slots(starts, eidx, rank).reshape(1, pairs)
    xsa, xsb = _sc_dispatch([xpa, xpb], slots_flat, tiles * ROW_TILE)
    ysa, ysb = _experts(tile_expert, xsa, xsb, wg, wu, wd)
    ga, gb = _sc_gather([ysa, ysb], slots_flat)
    return _finish(x, xb, p, ga.reshape(TOP_K, n, PART), gb.reshape(TOP_K, n, PART), wtok, wgs, wus, wds, wpg, bpg, wp, ln_g, ln_b)


def kernel(x, p, w_in, b_in, sg_ln_g, sg_ln_b, w_s, b_s, mh_g, w_pa, w_pb, w_out, ln1_g, ln1_b, w_router, b_router, w_gate_e, w_up_e, w_down_e, w_gate_s, w_up_s, w_down_s, w_pg, b_pg, w_p, ln2_g, ln2_b):
    batch, seq, _ = x.shape
    n = batch * seq
    xf = x.reshape(n, D_MODEL)
    xb = xf.astype(BF16)
    pf = p.reshape(DEPTH, n, P_DIM)

    c_uv = 2 * G_WIDTH
    c_qkv = c_uv + 2 * M_QK + M_V
    c_o = c_qkv + M_V
    c_if = c_o + 2 * M_HEADS

    for l in range(DEPTH):
        w = w_in[l]
        b = b_in[l][None, :]
        w_gate = jnp.concatenate([w[:, c_qkv:c_o], w[:, c_if:]], axis=1).astype(BF16)
        b_gate = jnp.concatenate([b[:, c_qkv:c_o], b[:, c_if:]], axis=1)
        w_if = jnp.pad(w[:, c_o:c_if], ((0, 0), (0, LANES - 2 * M_HEADS))).astype(BF16)
        b_if = jnp.pad(b[:, c_o:c_if], ((0, 0), (0, LANES - 2 * M_HEADS)))

        uv = _proj(xb, w[:, :c_uv].astype(BF16), b[:, :c_uv], "gelu", BF16)
        qkv = _proj(xb, w[:, c_uv:c_qkv].astype(BF16), b[:, c_uv:c_qkv], None, BF16)
        gates = _proj(xb, w_gate, b_gate, "sigmoid", BF16)
        gif = _proj(xb, w_if, b_if, None, F32)
        gif_rows = gif[:, :2 * M_HEADS].reshape(batch, seq, 2 * M_HEADS).transpose(0, 2, 1)

        a = _sgu(uv, sg_ln_g[l][None, :], sg_ln_b[l][None, :], w_s[l], b_s[l].T)
        mb = _mlstm(qkv, gif, gif_rows, gates, mh_g[l][None, :], batch, seq)
        xf, xb, xpa, xpb = _merge(xf, a, mb, gates, w_pa[l].astype(BF16), w_pb[l].astype(BF16),
                            w_out[l].astype(BF16), ln1_g[l][None, :], ln1_b[l][None, :])

        xf, xb = _moe(xf, xb, xpa, xpb, pf[l], w_router[l].T, b_router[l][:, None],
                      w_gate_e[l].astype(BF16), w_up_e[l].astype(BF16), w_down_e[l].astype(BF16),
                      w_gate_s[l].astype(BF16), w_up_s[l].astype(BF16), w_down_s[l].astype(BF16),
                      w_pg[l].astype(BF16), b_pg[l][None, :], w_p[l].astype(BF16),
                      ln2_g[l][None, :], ln2_b[l][None, :])
    return xf.reshape(batch, seq, D_MODEL)
```

```python
import functools

import jax
import jax.numpy as jnp
from jax import lax
from jax.experimental import pallas as pl
from jax.experimental.pallas import tpu as pltpu
from jax.experimental.pallas import tpu_sc as plsc

D_MODEL = 1024
DEPTH = 4
CHUNK = 64
P_DIM = 256
G_WIDTH = 1024
G_GROUPS = 8
G_GROUP_DIM = G_WIDTH // G_GROUPS
G_BLOCK = 128
M_HEADS = 4
M_QK_DIM = 128
M_V_DIM = 256
M_QK = M_HEADS * M_QK_DIM
M_V = M_HEADS * M_V_DIM
N_EXPERTS = 64
TOP_K = 8
N_GROUPS = 8
TOPK_GROUPS = 4
GROUP_SIZE = N_EXPERTS // N_GROUPS
D_EXPERT = 256
D_SHARED = 256
ROUTE_SCALE = 2.5
ALPHA = (2 * DEPTH) ** 0.25
LN_EPS = 1e-5
RMS_EPS = 1e-6

LANES = 128
VMEM_LIMIT = 56 * 1024 * 1024
HALF = D_MODEL // 2
PART = HALF // 2
ROW_TILE = 512
SC_WINDOW = 128

F32 = jnp.float32
BF16 = jnp.bfloat16
I32 = jnp.int32


def _params(*semantics):
    return pltpu.CompilerParams(dimension_semantics=semantics, vmem_limit_bytes=VMEM_LIMIT)


def _layer_norm(x, g, b):
    mu = jnp.mean(x, axis=-1, keepdims=True)
    xc = x - mu
    var = jnp.mean(xc * xc, axis=-1, keepdims=True)
    return xc * lax.rsqrt(var + LN_EPS) * g + b


def _gelu(x):
    return 0.5 * x * (1.0 + lax.erf(x * (2.0 ** -0.5)))


def _silu(x):
    return x * jax.nn.sigmoid(x)


def _pack_pairs(x):
    lo = lax.bitcast_convert_type(x[:, :PART].astype(BF16).astype(F32), I32)
    hi = lax.bitcast_convert_type(x[:, PART:].astype(BF16).astype(F32), I32)
    return lax.shift_right_logical(lo, 16) | (hi & jnp.int32(-65536))


def _pack_row(x):
    return _pack_pairs(x[:, :HALF]), _pack_pairs(x[:, HALF:])


def _unpack_pairs(w):
    lo = lax.bitcast_convert_type(lax.shift_left(w, 16), F32)
    hi = lax.bitcast_convert_type(w & jnp.int32(-65536), F32)
    return jnp.concatenate([lo, hi], axis=1)


def _proj_kernel(x_ref, w_ref, b_ref, o_ref, *, act):
    acc = jnp.dot(x_ref[...], w_ref[...], preferred_element_type=F32) + b_ref[...]
    if act == "gelu":
        acc = _gelu(acc)
    elif act == "sigmoid":
        acc = jax.nn.sigmoid(acc)
    o_ref[...] = acc.astype(o_ref.dtype)


def _proj(x, w, b, act, out_dtype, tm=1024, tn=1024):
    n, k = x.shape
    nout = w.shape[1]
    tn = min(tn, nout)
    return pl.pallas_call(
        functools.partial(_proj_kernel, act=act),
        grid=(n // tm, nout // tn),
        in_specs=[
            pl.BlockSpec((tm, k), lambda i, j: (i, 0)),
            pl.BlockSpec((k, tn), lambda i, j: (0, j)),
            pl.BlockSpec((1, tn), lambda i, j: (0, j)),
        ],
        out_specs=pl.BlockSpec((tm, tn), lambda i, j: (i, j)),
        out_shape=jax.ShapeDtypeStruct((n, nout), out_dtype),
        name="proj_" + (act or "linear"),
        compiler_params=_params("parallel", "arbitrary"),
    )(x, w, b)


def _sgu_kernel(uv_ref, lng_ref, lnb_ref, ws_ref, bs_ref, a_ref, *, blocks):
    v = uv_ref[:, G_WIDTH:].astype(F32)
    vln = _layer_norm(v, lng_ref[...], lnb_ref[...]).astype(BF16)
    t_chunk = lax.broadcasted_iota(jnp.int32, (G_BLOCK, G_BLOCK), 0) // CHUNK
    s_chunk = lax.broadcasted_iota(jnp.int32, (G_BLOCK, G_BLOCK), 1) // CHUNK
    causal = s_chunk <= t_chunk
    for g in range(G_GROUPS):
        w = jnp.where(causal, ws_ref[g], 0.0).astype(BF16)
        bias = bs_ref[:, g:g + 1]
        cs = slice(g * G_GROUP_DIM, (g + 1) * G_GROUP_DIM)
        for blk in range(blocks):
            rs = slice(blk * G_BLOCK, (blk + 1) * G_BLOCK)
            mixed = jnp.dot(w, vln[rs, cs], preferred_element_type=F32) + bias
            a_ref[rs, cs] = (uv_ref[rs, cs].astype(F32) * mixed).astype(a_ref.dtype)


def _sgu(uv, ln_g, ln_b, w_s, b_s_t, blocks=4):
    n = uv.shape[0]
    tp = blocks * G_BLOCK
    return pl.pallas_call(
        functools.partial(_sgu_kernel, blocks=blocks),
        grid=(n // tp,),
        in_specs=[
            pl.BlockSpec((tp, 2 * G_WIDTH), lambda i: (i, 0)),
            pl.BlockSpec((1, G_WIDTH), lambda i: (0, 0)),
            pl.BlockSpec((1, G_WIDTH), lambda i: (0, 0)),
            pl.BlockSpec((G_GROUPS, G_BLOCK, G_BLOCK), lambda i: (0, 0, 0)),
            pl.BlockSpec((G_BLOCK, G_GROUPS), lambda i: (0, 0)),
        ],
        out_specs=pl.BlockSpec((tp, G_WIDTH), lambda i: (i, 0)),
        out_shape=jax.ShapeDtypeStruct((n, G_WIDTH), BF16),
        name="spatial_gating",
        compiler_params=_params("parallel"),
    )(uv, ln_g, ln_b, w_s, b_s_t)


def _segment_cumsum(x, axis, seg):
    pos = lax.broadcasted_iota(jnp.int32, x.shape, axis) % seg
    shift = 1
    while shift < seg:
        x = x + jnp.where(pos >= shift, pltpu.roll(x, shift, axis), 0.0)
        shift *= 2
    return x


def _mlstm_kernel(qkv_ref, gc_ref, gr_ref, osig_ref, mhg_ref, mb_ref, c_ref, n_ref, m_ref, *, chunks):
    @pl.when(pl.program_id(1) == 0)
    def _():
        c_ref[...] = jnp.zeros_like(c_ref)
        n_ref[...] = jnp.zeros_like(n_ref)
        m_ref[...] = jnp.zeros_like(m_ref)

    scale = M_QK_DIM ** -0.5
    gc = gc_ref[...]
    gr = gr_ref[0]
    b_cols = _segment_cumsum(jax.nn.log_sigmoid(gc), 0, CHUNK)
    b_rows = _segment_cumsum(jax.nn.log_sigmoid(gr), 1, CHUNK)
    t_idx = lax.broadcasted_iota(jnp.int32, (CHUNK, CHUNK), 0)
    s_idx = lax.broadcasted_iota(jnp.int32, (CHUNK, CHUNK), 1)
    tri = s_idx <= t_idx

    heads = range(M_HEADS)
    units = [(c, h) for c in range(chunks) for h in heads]
    rows = lambda c: slice(c * CHUNK, (c + 1) * CHUNK)
    q = {u: qkv_ref[rows(u[0]), u[1] * M_QK_DIM:(u[1] + 1) * M_QK_DIM] for u in units}
    k = {u: qkv_ref[rows(u[0]), M_QK + u[1] * M_QK_DIM:M_QK + (u[1] + 1) * M_QK_DIM] for u in units}
    v = {u: qkv_ref[rows(u[0]), 2 * M_QK + u[1] * M_V_DIM:2 * M_QK + (u[1] + 1) * M_V_DIM] for u in units}
    i_col = {u: gc[rows(u[0]), u[1]:u[1] + 1] for u in units}
    b_col = {u: b_cols[rows(u[0]), M_HEADS + u[1]:M_HEADS + u[1] + 1] for u in units}
    ib_row = {u: (b_rows[M_HEADS + u[1]:M_HEADS + u[1] + 1, rows(u[0])]
                  - gr[u[1]:u[1] + 1, rows(u[0])]) for u in units}
    d = {u: jnp.where(tri, b_col[u] - ib_row[u], -jnp.inf) for u in units}
    d_max = {u: jnp.max(d[u], axis=1, keepdims=True) for u in units}
    qk = {u: lax.dot_general(q[u], k[u], (((1,), (1,)), ((), ())), preferred_element_type=F32) for u in units}
    s_loc = {u: qk[u] * scale * jnp.exp(d[u] - d_max[u]) for u in units}
    a_loc = {u: jnp.dot(s_loc[u].astype(BF16), v[u], preferred_element_type=F32) for u in units}
    r_loc = {u: jnp.sum(s_loc[u], axis=1, keepdims=True) for u in units}
    b_last = {u: b_col[u][CHUNK - 1:CHUNK, :] for u in units}
    g_col = {u: b_last[u] - b_col[u] + i_col[u] for u in units}
    g_max = {u: jnp.max(g_col[u], axis=0, keepdims=True) for u in units}
    wk = {u: jnp.exp(g_col[u] - g_max[u]) * k[u].astype(F32) for u in units}
    u_loc = {u: jnp.dot(wk[u].T.astype(BF16), v[u], preferred_element_type=F32) for u in units}
    nk_loc = {u: jnp.sum(wk[u], axis=0, keepdims=True) for u in units}

    c_state = [c_ref[h] for h in heads]
    n_state = [n_ref[h] for h in heads]
    m_state = [m_ref[h][:, 0:1] for h in heads]
    for c in range(chunks):
        us = [(c, h) for h in heads]
        inter = [b_col[u] + m_state[u[1]] for u in us]
        m_t = [jnp.maximum(inter[h], d_max[(c, h)]) for h in heads]
        w_intra = [jnp.exp(d_max[(c, h)] - m_t[h]) for h in heads]
        w_inter = [jnp.exp(inter[h] - m_t[h]) for h in heads]
        qc = [jnp.dot(q[(c, h)], c_state[h].astype(BF16), preferred_element_type=F32) * scale for h in heads]
        qn = [jnp.sum(q[(c, h)].astype(F32) * n_state[h], axis=1, keepdims=True) * scale for h in heads]
        m_new = [jnp.maximum(b_last[(c, h)] + m_state[h], g_max[(c, h)]) for h in heads]
        decay = [jnp.exp(b_last[(c, h)] + m_state[h] - m_new[h]) for h in heads]
        beta = [jnp.exp(g_max[(c, h)] - m_new[h]) for h in heads]
        c_state = [decay[h] * c_state[h] + beta[h] * u_loc[(c, h)] for h in heads]
        n_state = [decay[h] * n_state[h] + beta[h] * nk_loc[(c, h)] for h in heads]
        m_state = m_new
        num = [w_intra[h] * a_loc[(c, h)] + w_inter[h] * qc[h] for h in heads]
        den = [w_intra[h] * r_loc[(c, h)] + w_inter[h] * qn[h] for h in heads]
        hv = [num[h] * (1.0 / jnp.maximum(jnp.abs(den[h]), jnp.exp(-m_t[h]))) for h in heads]
        hv = [hv[h] * lax.rsqrt(jnp.mean(hv[h] * hv[h], axis=1, keepdims=True) + RMS_EPS) for h in heads]
        for h in heads:
            vs = slice(h * M_V_DIM, (h + 1) * M_V_DIM)
            mb_ref[rows(c), vs] = (osig_ref[rows(c), vs].astype(F32) * (hv[h] * mhg_ref[:, vs])).astype(mb_ref.dtype)

    for h in heads:
        c_ref[h], n_ref[h] = c_state[h], n_state[h]
        m_ref[h] = jnp.broadcast_to(m_state[h], (1, LANES))


def _mlstm(qkv, gates_col, gates_row, osig_src, mh_g, batch, seq, chunks=8):
    ts = chunks * CHUNK
    tiles = seq // ts
    n = batch * seq
    return pl.pallas_call(
        functools.partial(_mlstm_kernel, chunks=chunks),
        grid=(batch, tiles),
        in_specs=[
            pl.BlockSpec((ts, 2 * M_QK + M_V), lambda b, t: (b * tiles + t, 0)),
            pl.BlockSpec((ts, LANES), lambda b, t: (b * tiles + t, 0)),
            pl.BlockSpec((1, 2 * M_HEADS, ts), lambda b, t: (b, 0, t)),
            pl.BlockSpec((ts, M_V), lambda b, t: (b * tiles + t, 0)),
            pl.BlockSpec((1, M_V), lambda b, t: (0, 0)),
        ],
        out_specs=pl.BlockSpec((ts, M_V), lambda b, t: (b * tiles + t, 0)),
        out_shape=jax.ShapeDtypeStruct((n, M_V), BF16),
        scratch_shapes=[
            pltpu.VMEM((M_HEADS, M_QK_DIM, M_V_DIM), F32),
            pltpu.VMEM((M_HEADS, 1, M_QK_DIM), F32),
            pltpu.VMEM((M_HEADS, 1, LANES), F32),
        ],
        name="mlstm",
        compiler_params=_params("parallel", "arbitrary"),
    )(qkv, gates_col, gates_row, osig_src, mh_g)


def _merge_kernel(x_ref, a_ref, mb_ref, ga_ref, gb_ref, wpa_ref, wpb_ref, wout_ref, g_ref, b_ref,
                  xo_ref, xob_ref, xpa_ref, xpb_ref):
    ya = jnp.dot(a_ref[...], wpa_ref[...], preferred_element_type=F32)
    yb = jnp.dot(mb_ref[...], wpb_ref[...], preferred_element_type=F32)
    y = ga_ref[...].astype(F32) * ya + gb_ref[...].astype(F32) * yb
    mix = jnp.dot(y.astype(BF16), wout_ref[...], preferred_element_type=F32)
    xn = _layer_norm(ALPHA * x_ref[...] + mix, g_ref[...], b_ref[...])
    xo_ref[...] = xn
    xob_ref[...] = xn.astype(BF16)
    xpa_ref[...], xpb_ref[...] = _pack_row(xn)


def _merge(x, a, mb, gates, w_pa, w_pb, w_out, ln_g, ln_b, tm=512):
    n = x.shape[0]
    row = lambda i: (i, 0)
    const = lambda i: (0, 0)
    wspec = pl.BlockSpec((D_MODEL, D_MODEL), const)
    vspec = pl.BlockSpec((1, D_MODEL), const)
    return pl.pallas_call(
        _merge_kernel,
        grid=(n // tm,),
        in_specs=[
            pl.BlockSpec((tm, D_MODEL), row),
            pl.BlockSpec((tm, G_WIDTH), row),
            pl.BlockSpec((tm, M_V), row),
            pl.BlockSpec((tm, D_MODEL), lambda i: (i, 1)),
            pl.BlockSpec((tm, D_MODEL), lambda i: (i, 2)),
            wspec, wspec, wspec, vspec, vspec,
        ],
        out_specs=[pl.BlockSpec((tm, D_MODEL), row), pl.BlockSpec((tm, D_MODEL), row),
                   pl.BlockSpec((tm, PART), row), pl.BlockSpec((tm, PART), row)],
        out_shape=[jax.ShapeDtypeStruct((n, D_MODEL), F32), jax.ShapeDtypeStruct((n, D_MODEL), BF16),
                   jax.ShapeDtypeStruct((n, PART), I32), jax.ShapeDtypeStruct((n, PART), I32)],
        name="merge_ln",
        compiler_params=_params("parallel"),
    )(x, a, mb, gates, gates, w_pa, w_pb, w_out, ln_g, ln_b)


def _first_max(v, idx, axes, sentinel):
    m = jnp.max(v, axis=axes, keepdims=True)
    first = jnp.min(jnp.where(v == m, idx, sentinel), axis=axes, keepdims=True)
    return m, first


def _router_kernel(x_ref, wr_ref, br_ref, eidx_ref, rank_ref, wtok_ref, cnt_ref, run_ref, *, tm):
    @pl.when(pl.program_id(0) == 0)
    def _():
        run_ref[...] = jnp.zeros_like(run_ref)

    logits = lax.dot_general(wr_ref[...], x_ref[...], (((1,), (1,)), ((), ())),
                             preferred_element_type=F32, precision=lax.Precision.HIGHEST)
    scores = jax.nn.sigmoid(logits)
    sel = (scores + br_ref[...]).reshape(N_GROUPS, GROUP_SIZE, tm)
    scores3 = scores.reshape(N_GROUPS, GROUP_SIZE, tm)
    member = lax.broadcasted_iota(jnp.int32, sel.shape, 1)
    group = lax.broadcasted_iota(jnp.int32, sel.shape, 0)
    neg = -jnp.inf

    m1, f1 = _first_max(sel, member, 1, GROUP_SIZE)
    m2 = jnp.max(jnp.where(member == f1, neg, sel), axis=1, keepdims=True)
    gscore = m1 + m2
    gid = lax.broadcasted_iota(jnp.int32, gscore.shape, 0)
    gmask = jnp.zeros(gscore.shape, dtype=jnp.bool_)
    for _ in range(TOPK_GROUPS):
        _, fg = _first_max(gscore, gid, 0, N_GROUPS)
        hit = gid == fg
        gmask = gmask | hit
        gscore = jnp.where(hit, neg, gscore)

    cand = jnp.where(gmask, sel, neg)
    eid = group * GROUP_SIZE + member
    chosen = jnp.zeros(sel.shape, dtype=jnp.bool_)
    picks = []
    for _ in range(TOP_K):
        _, fe = _first_max(cand, eid, (0, 1), N_EXPERTS)
        hit = eid == fe
        chosen = chosen | hit
        cand = jnp.where(hit, neg, cand)
        picks.append((fe[0], hit, jnp.sum(jnp.where(hit, scores3, 0.0), axis=(0, 1), keepdims=True)[0]))

    chosen2 = jnp.where(chosen, 1.0, 0.0).reshape(N_EXPERTS, tm)
    r_idx = lax.broadcasted_iota(jnp.int32, (tm, tm), 0)
    c_idx = lax.broadcasted_iota(jnp.int32, (tm, tm), 1)
    upper = jnp.where(r_idx < c_idx, 1.0, 0.0).astype(BF16)
    prefix = jnp.dot(chosen2.astype(BF16), upper, preferred_element_type=F32) + run_ref[:, 0:1]
    prefix3 = prefix.reshape(N_GROUPS, GROUP_SIZE, tm)

    total = picks[0][2]
    for _, _, wk in picks[1:]:
        total = total + wk
    eidx_ref[...] = jnp.concatenate([fe for fe, _, _ in picks], axis=0)
    rank_ref[...] = jnp.concatenate(
        [jnp.sum(jnp.where(hit, prefix3, 0.0), axis=(0, 1), keepdims=True)[0] for _, hit, _ in picks],
        axis=0).astype(I32)
    w_rows = jnp.concatenate([wk / total * ROUTE_SCALE for _, _, wk in picks]
                             + [jnp.zeros((LANES - TOP_K, tm), F32)], axis=0)
    wtok_ref[...] = w_rows.T

    run = run_ref[...] + jnp.sum(chosen2, axis=1, keepdims=True)
    run_ref[...] = run
    cnt_ref[...] = run


def _router(x, wr_t, br, tm=512):
    n = x.shape[0]
    tok = lambda i: (0, i)
    return pl.pallas_call(
        functools.partial(_router_kernel, tm=tm),
        grid=(n // tm,),
        in_specs=[
            pl.BlockSpec((tm, D_MODEL), lambda i: (i, 0)),
            pl.BlockSpec((N_EXPERTS, D_MODEL), lambda i: (0, 0)),
            pl.BlockSpec((N_EXPERTS, 1), lambda i: (0, 0)),
        ],
        out_specs=[
            pl.BlockSpec((TOP_K, tm), tok),
            pl.BlockSpec((TOP_K, tm), tok),
            pl.BlockSpec((tm, LANES), lambda i: (i, 0)),
            pl.BlockSpec((N_EXPERTS, LANES), lambda i: (0, 0)),
        ],
        out_shape=[
            jax.ShapeDtypeStruct((TOP_K, n), I32),
            jax.ShapeDtypeStruct((TOP_K, n), I32),
            jax.ShapeDtypeStruct((n, LANES), F32),
            jax.ShapeDtypeStruct((N_EXPERTS, LANES), F32),
        ],
        scratch_shapes=[pltpu.VMEM((N_EXPERTS, LANES), F32)],
        name="router",
        compiler_params=_params("arbitrary"),
    )(x, wr_t, br)


def _slots_kernel(starts_ref, eidx_ref, rank_ref, slot_ref):
    eidx = eidx_ref[...]
    slot = rank_ref[...]
    for e in range(N_EXPERTS):
        slot = slot + jnp.where(eidx == e, starts_ref[e], 0)
    slot_ref[...] = slot


def _slots(starts, eidx, rank, tn=2048):
    n = eidx.shape[1]
    tn = min(tn, n)
    spec = pl.BlockSpec((TOP_K, tn), lambda i, s: (0, i))
    return pl.pallas_call(
        _slots_kernel,
        grid_spec=pltpu.PrefetchScalarGridSpec(
            num_scalar_prefetch=1, grid=(n // tn,), in_specs=[spec, spec], out_specs=spec),
        out_shape=jax.ShapeDtypeStruct((TOP_K, n), I32),
        name="slots",
        compiler_params=_params("parallel"),
    )(starts, eidx, rank)


def _sc_mesh():
    return plsc.VectorSubcoreMesh(core_axis_name="c", subcore_axis_name="s")


def _sc_dispatch(parts, slots_flat, rows):
    n, width = parts[0].shape
    blocks = n // SC_WINDOW
    out_type = [jax.ShapeDtypeStruct((rows, width), part.dtype) for part in parts]

    @functools.partial(pl.kernel, out_type=out_type, mesh=_sc_mesh(), scratch_types=[], name="sc_dispatch")
    def run(*refs):
        i_hbm = refs[len(parts)]
        for x_hbm, o_hbm in zip(refs[:len(parts)], refs[len(parts) + 1:]):
            def body(x_vmem, i_vmem, o_hbm=o_hbm):
                pltpu.sync_copy(x_vmem, o_hbm.at[i_vmem.at[0]])

            pltpu.emit_pipeline(
                body,
                grid=(blocks, TOP_K),
                in_specs=[pl.BlockSpec((SC_WINDOW, width), lambda i, k: (i, 0)),
                          pl.BlockSpec((1, SC_WINDOW), lambda i, k: (0, k * blocks + i))],
                out_specs=[],
                core_axis_name=("c", "s"),
                dimension_semantics=(pltpu.PARALLEL, pltpu.ARBITRARY),
                trace_scopes=False,
            )(x_hbm, i_hbm)

    return run(*parts, slots_flat)


def _sc_gather(tables, idx_flat):
    count = idx_flat.shape[1]
    width = tables[0].shape[1]
    out_type = [jax.ShapeDtypeStruct((count, width), table.dtype) for table in tables]

    @functools.partial(pl.kernel, out_type=out_type, mesh=_sc_mesh(), scratch_types=[], name="sc_gather")
    def run(*refs):
        i_hbm = refs[len(tables)]
        for t_hbm, o_hbm in zip(refs[:len(tables)], refs[len(tables) + 1:]):
            def body(i_vmem, o_vmem, t_hbm=t_hbm):
                pltpu.sync_copy(t_hbm.at[i_vmem.at[0]], o_vmem)

            pltpu.emit_pipeline(
                body,
                grid=(count // SC_WINDOW,),
                in_specs=[pl.BlockSpec((1, SC_WINDOW), lambda i: (0, i))],
                out_specs=[pl.BlockSpec((SC_WINDOW, width), lambda i: (i, 0))],
                core_axis_name=("c", "s"),
                dimension_semantics=(pltpu.PARALLEL,),
                trace_scopes=False,
            )(i_hbm, o_hbm)

    return run(*tables, idx_flat)


def _expert_kernel(te_ref, xa_ref, xb_ref, wg_ref, wu_ref, wd_ref, oa_ref, ob_ref):
    del te_ref
    x = jnp.concatenate([_unpack_pairs(xa_ref[...]), _unpack_pairs(xb_ref[...])], axis=1).astype(BF16)
    hg = jnp.dot(x, wg_ref[0], preferred_element_type=F32)
    hu = jnp.dot(x, wu_ref[0], preferred_element_type=F32)
    hid = (_silu(hg) * hu).astype(BF16)
    oa_ref[...], ob_ref[...] = _pack_row(jnp.dot(hid, wd_ref[0], preferred_element_type=F32))


def _experts(tile_expert, xa, xb, wg, wu, wd):
    rows = xa.shape[0]
    row = lambda j, te: (j, 0)
    exp = lambda j, te: (te[j], 0, 0)
    return pl.pallas_call(
        _expert_kernel,
        grid_spec=pltpu.PrefetchScalarGridSpec(
            num_scalar_prefetch=1,
            grid=(rows // ROW_TILE,),
            in_specs=[
                pl.BlockSpec((ROW_TILE, PART), row),
                pl.BlockSpec((ROW_TILE, PART), row),
                pl.BlockSpec((1, D_MODEL, D_EXPERT), exp),
                pl.BlockSpec((1, D_MODEL, D_EXPERT), exp),
                pl.BlockSpec((1, D_EXPERT, D_MODEL), exp),
            ],
            out_specs=[pl.BlockSpec((ROW_TILE, PART), row), pl.BlockSpec((ROW_TILE, PART), row)],
        ),
        out_shape=[jax.ShapeDtypeStruct((rows, PART), I32), jax.ShapeDtypeStruct((rows, PART), I32)],
        name="experts",
        compiler_params=_params("parallel"),
    )(tile_expert, xa, xb, wg, wu, wd)


def _finish_kernel(x_ref, xb_ref, p_ref, ga_ref, gb_ref, wtok_ref, wgs_ref, wus_ref, wds_ref, wpg_ref, bpg_ref, wp_ref,
                   lng_ref, lnb_ref, xo_ref, xob_ref):
    xb = xb_ref[...]
    hs = _silu(jnp.dot(xb, wgs_ref[...], preferred_element_type=F32)) * jnp.dot(
        xb, wus_ref[...], preferred_element_type=F32)
    shared = jnp.dot(hs.astype(BF16), wds_ref[...], preferred_element_type=F32)
    gate = jax.nn.sigmoid(jnp.dot(xb, wpg_ref[...], preferred_element_type=F32) + bpg_ref[...])
    ple = gate * jnp.dot(p_ref[...].astype(BF16), wp_ref[...], preferred_element_type=F32)

    wtok = wtok_ref[...]
    r_a = None
    r_b = None
    for k in range(TOP_K):
        wk = wtok[:, k:k + 1]
        a = wk * _unpack_pairs(ga_ref[k])
        b = wk * _unpack_pairs(gb_ref[k])
        r_a = a if r_a is None else r_a + a
        r_b = b if r_b is None else r_b + b
    routed = jnp.concatenate([r_a, r_b], axis=1)

    xn = _layer_norm(ALPHA * x_ref[...] + (routed + shared + ple), lng_ref[...], lnb_ref[...])
    xo_ref[...] = xn
    xob_ref[...] = xn.astype(BF16)


def _finish(x, xb, p, ga, gb, wtok, wgs, wus, wds, wpg, bpg, wp, ln_g, ln_b, tm=256):
    n = x.shape[0]
    row = lambda i: (i, 0)
    const = lambda i: (0, 0)
    return pl.pallas_call(
        _finish_kernel,
        grid=(n // tm,),
        in_specs=[
            pl.BlockSpec((tm, D_MODEL), row),
            pl.BlockSpec((tm, D_MODEL), row),
            pl.BlockSpec((tm, P_DIM), row),
            pl.BlockSpec((TOP_K, tm, PART), lambda i: (0, i, 0)),
            pl.BlockSpec((TOP_K, tm, PART), lambda i: (0, i, 0)),
            pl.BlockSpec((tm, LANES), row),
            pl.BlockSpec((D_MODEL, D_SHARED), const),
            pl.BlockSpec((D_MODEL, D_SHARED), const),
            pl.BlockSpec((D_SHARED, D_MODEL), const),
            pl.BlockSpec((D_MODEL, D_MODEL), const),
            pl.BlockSpec((1, D_MODEL), const),
            pl.BlockSpec((P_DIM, D_MODEL), const),
            pl.BlockSpec((1, D_MODEL), const),
            pl.BlockSpec((1, D_MODEL), const),
        ],
        out_specs=[pl.BlockSpec((tm, D_MODEL), row), pl.BlockSpec((tm, D_MODEL), row)],
        out_shape=[jax.ShapeDtypeStruct((n, D_MODEL), F32), jax.ShapeDtypeStruct((n, D_MODEL), BF16)],
        name="moe_finish_ln",
        compiler_params=_params("parallel"),
    )(x, xb, p, ga, gb, wtok, wgs, wus, wds, wpg, bpg, wp, ln_g, ln_b)


def _moe(x, xb, xpa, xpb, p, w_router_t, b_router, wg, wu, wd, wgs, wus, wds, wpg, bpg, wp, ln_g, ln_b):
    n = x.shape[0]
    pairs = n * TOP_K
    tiles = pairs // ROW_TILE + N_EXPERTS
    eidx, rank, wtok, counts = _router(x, w_router_t, b_router)

    cnt = counts[:, 0].astype(I32)
    group_tiles = (cnt + ROW_TILE - 1) // ROW_TILE
    tile_end = jnp.cumsum(group_tiles)
    starts = (tile_end - group_tiles) * ROW_TILE
    tile_id = jnp.arange(tiles, dtype=I32)
    tile_expert = jnp.minimum(jnp.sum((tile_end[None, :] <= tile_id[:, None]).astype(I32), axis=1), N_EXPERTS - 1)

    slots_flat = _slots(starts, eidx, rank).reshape(1, pairs)
    xsa, xsb = _sc_dispatch([xpa, xpb], slots_flat, tiles * ROW_TILE)
    ysa, ysb = _experts(tile_expert, xsa, xsb, wg, wu, wd)
    ga, gb = _sc_gather([ysa, ysb], slots_flat)
    return _finish(x, xb, p, ga.reshape(TOP_K, n, PART), gb.reshape(TOP_K, n, PART), wtok, wgs, wus, wds, wpg, bpg, wp, ln_g, ln_b)


def kernel(x, p, w_in, b_in, sg_ln_g, sg_ln_b, w_s, b_s, mh_g, w_pa, w_pb, w_out, ln1_g, ln1_b, w_router, b_router, w_gate_e, w_up_e, w_down_e, w_gate_s, w_up_s, w_down_s, w_pg, b_pg, w_p, ln2_g, ln2_b):
    batch, seq, _ = x.shape
    n = batch * seq
    xf = x.reshape(n, D_MODEL)
    xb = xf.astype(BF16)
    pf = p.reshape(DEPTH, n, P_DIM)

    c_uv = 2 * G_WIDTH
    c_qkv = c_uv + 2 * M_QK + M_V
    c_o = c_qkv + M_V
    c_if = c_o + 2 * M_HEADS

    for l in range(DEPTH):
        w = w_in[l]
        b = b_in[l][None, :]
        w_gate = jnp.concatenate([w[:, c_qkv:c_o], w[:, c_if:]], axis=1).astype(BF16)
        b_gate = jnp.concatenate([b[:, c_qkv:c_o], b[:, c_if:]], axis=1)
        w_if = jnp.pad(w[:, c_o:c_if], ((0, 0), (0, LANES - 2 * M_HEADS))).astype(BF16)
        b_if = jnp.pad(b[:, c_o:c_if], ((0, 0), (0, LANES - 2 * M_HEADS)))

        uv = _proj(xb, w[:, :c_uv].astype(BF16), b[:, :c_uv], "gelu", BF16)
        qkv = _proj(xb, w[:, c_uv:c_qkv].astype(BF16), b[:, c_uv:c_qkv], None, BF16)
        gates = _proj(xb, w_gate, b_gate, "sigmoid", BF16)
        gif = _proj(xb, w_if, b_if, None, F32)
        gif_rows = gif[:, :2 * M_HEADS].reshape(batch, seq, 2 * M_HEADS).transpose(0, 2, 1)

        a = _sgu(uv, sg_ln_g[l][None, :], sg_ln_b[l][None, :], w_s[l], b_s[l].T)
        mb = _mlstm(qkv, gif, gif_rows, gates, mh_g[l][None, :], batch, seq)
        xf, xb, xpa, xpb = _merge(xf, a, mb, gates, w_pa[l].astype(BF16), w_pb[l].astype(BF16),
                            w_out[l].astype(BF16), ln1_g[l][None, :], ln1_b[l][None, :])

        xf, xb = _moe(xf, xb, xpa, xpb, pf[l], w_router[l].T, b_router[l][:, None],
                      w_gate_e[l].astype(BF16), w_up_e[l].astype(BF16), w_down_e[l].astype(BF16),
                      w_gate_s[l].astype(BF16), w_up_s[l].astype(BF16), w_down_s[l].astype(BF16),
                      w_pg[l].astype(BF16), b_pg[l][None, :], w_p[l].astype(BF16),
                      ln2_g[l][None, :], ln2_b[l][None, :])
    return xf.reshape(batch, seq, D_MODEL)
```

```python
import functools

import jax
import jax.numpy as jnp
from jax import lax
from jax.experimental import pallas as pl
from jax.experimental.pallas import tpu as pltpu
from jax.experimental.pallas import tpu_sc as plsc

D_MODEL = 1024
DEPTH = 4
CHUNK = 64
P_DIM = 256
G_WIDTH = 1024
G_GROUPS = 8
G_GROUP_DIM = G_WIDTH // G_GROUPS
G_BLOCK = 128
M_HEADS = 4
M_QK_DIM = 128
M_V_DIM = 256
M_QK = M_HEADS * M_QK_DIM
M_V = M_HEADS * M_V_DIM
N_EXPERTS = 64
TOP_K = 8
N_GROUPS = 8
TOPK_GROUPS = 4
GROUP_SIZE = N_EXPERTS // N_GROUPS
D_EXPERT = 256
D_SHARED = 256
ROUTE_SCALE = 2.5
ALPHA = (2 * DEPTH) ** 0.25
LN_EPS = 1e-5
RMS_EPS = 1e-6

LANES = 128
VMEM_LIMIT = 56 * 1024 * 1024
HALF = D_MODEL // 2
PART = HALF // 2
ROW_TILE = 512
STREAMS = 2
SC_WINDOW = 128

F32 = jnp.float32
BF16 = jnp.bfloat16
I32 = jnp.int32


def _params(*semantics):
    return pltpu.CompilerParams(dimension_semantics=semantics, vmem_limit_bytes=VMEM_LIMIT)


def _layer_norm(x, g, b):
    mu = jnp.mean(x, axis=-1, keepdims=True)
    xc = x - mu
    var = jnp.mean(xc * xc, axis=-1, keepdims=True)
    return xc * lax.rsqrt(var + LN_EPS) * g + b


def _gelu(x):
    return 0.5 * x * (1.0 + lax.erf(x * (2.0 ** -0.5)))


def _silu(x):
    return x * jax.nn.sigmoid(x)


def _pack_pairs(x):
    lo = lax.bitcast_convert_type(x[:, :PART].astype(BF16).astype(F32), I32)
    hi = lax.bitcast_convert_type(x[:, PART:].astype(BF16).astype(F32), I32)
    return lax.shift_right_logical(lo, 16) | (hi & jnp.int32(-65536))


def _pack_row(x):
    return _pack_pairs(x[:, :HALF]), _pack_pairs(x[:, HALF:])


def _unpack_pairs(w):
    lo = lax.bitcast_convert_type(lax.shift_left(w, 16), F32)
    hi = lax.bitcast_convert_type(w & jnp.int32(-65536), F32)
    return jnp.concatenate([lo, hi], axis=1)


def _proj_kernel(x_ref, w_ref, b_ref, o_ref, *, act):
    acc = jnp.dot(x_ref[...], w_ref[...], preferred_element_type=F32) + b_ref[...]
    if act == "gelu":
        acc = _gelu(acc)
    elif act == "sigmoid":
        acc = jax.nn.sigmoid(acc)
    o_ref[...] = acc.astype(o_ref.dtype)


def _proj(x, w, b, act, out_dtype, tm=1024, tn=1024):
    n, k = x.shape
    nout = w.shape[1]
    tn = min(tn, nout)
    return pl.pallas_call(
        functools.partial(_proj_kernel, act=act),
        grid=(n // tm, nout // tn),
        in_specs=[
            pl.BlockSpec((tm, k), lambda i, j: (i, 0)),
            pl.BlockSpec((k, tn), lambda i, j: (0, j)),
            pl.BlockSpec((1, tn), lambda i, j: (0, j)),
        ],
        out_specs=pl.BlockSpec((tm, tn), lambda i, j: (i, j)),
        out_shape=jax.ShapeDtypeStruct((n, nout), out_dtype),
        name="proj_" + (act or "linear"),
        compiler_params=_params("parallel", "arbitrary"),
    )(x, w, b)


def _sgu_kernel(uv_ref, lng_ref, lnb_ref, ws_ref, bs_ref, a_ref, *, blocks):
    v = uv_ref[:, G_WIDTH:].astype(F32)
    vln = _layer_norm(v, lng_ref[...], lnb_ref[...]).astype(BF16)
    t_chunk = lax.broadcasted_iota(jnp.int32, (G_BLOCK, G_BLOCK), 0) // CHUNK
    s_chunk = lax.broadcasted_iota(jnp.int32, (G_BLOCK, G_BLOCK), 1) // CHUNK
    causal = s_chunk <= t_chunk
    for g in range(G_GROUPS):
        w = jnp.where(causal, ws_ref[g], 0.0).astype(BF16)
        bias = bs_ref[:, g:g + 1]
        cs = slice(g * G_GROUP_DIM, (g + 1) * G_GROUP_DIM)
        for blk in range(blocks):
            rs = slice(blk * G_BLOCK, (blk + 1) * G_BLOCK)
            mixed = jnp.dot(w, vln[rs, cs], preferred_element_type=F32) + bias
            a_ref[rs, cs] = (uv_ref[rs, cs].astype(F32) * mixed).astype(a_ref.dtype)


def _sgu(uv, ln_g, ln_b, w_s, b_s_t, blocks=4):
    n = uv.shape[0]
    tp = blocks * G_BLOCK
    return pl.pallas_call(
        functools.partial(_sgu_kernel, blocks=blocks),
        grid=(n // tp,),
        in_specs=[
            pl.BlockSpec((tp, 2 * G_WIDTH), lambda i: (i, 0)),
            pl.BlockSpec((1, G_WIDTH), lambda i: (0, 0)),
            pl.BlockSpec((1, G_WIDTH), lambda i: (0, 0)),
            pl.BlockSpec((G_GROUPS, G_BLOCK, G_BLOCK), lambda i: (0, 0, 0)),
            pl.BlockSpec((G_BLOCK, G_GROUPS), lambda i: (0, 0)),
        ],
        out_specs=pl.BlockSpec((tp, G_WIDTH), lambda i: (i, 0)),
        out_shape=jax.ShapeDtypeStruct((n, G_WIDTH), BF16),
        name="spatial_gating",
        compiler_params=_params("parallel"),
    )(uv, ln_g, ln_b, w_s, b_s_t)


def _segment_cumsum(x, axis, seg):
    pos = lax.broadcasted_iota(jnp.int32, x.shape, axis) % seg
    shift = 1
    while shift < seg:
        x = x + jnp.where(pos >= shift, pltpu.roll(x, shift, axis), 0.0)
        shift *= 2
    return x


def _mlstm_kernel(qkv_ref, gc_ref, gr_ref, osig_ref, mhg_ref, mb_ref, c_ref, n_ref, m_ref, *, chunks):
    @pl.when(pl.program_id(1) == 0)
    def _():
        c_ref[...] = jnp.zeros_like(c_ref)
        n_ref[...] = jnp.zeros_like(n_ref)
        m_ref[...] = jnp.zeros_like(m_ref)

    scale = M_QK_DIM ** -0.5
    gc = gc_ref[...]
    gr = gr_ref[0]
    b_cols = _segment_cumsum(jax.nn.log_sigmoid(gc), 0, CHUNK)
    b_rows = _segment_cumsum(jax.nn.log_sigmoid(gr), 1, CHUNK)
    t_idx = lax.broadcasted_iota(jnp.int32, (CHUNK, CHUNK), 0)
    s_idx = lax.broadcasted_iota(jnp.int32, (CHUNK, CHUNK), 1)
    tri = s_idx <= t_idx

    heads = range(M_HEADS)
    units = [(c, h) for c in range(chunks) for h in heads]
    rows = lambda c: slice(c * CHUNK, (c + 1) * CHUNK)
    q = {u: qkv_ref[rows(u[0]), u[1] * M_QK_DIM:(u[1] + 1) * M_QK_DIM] for u in units}
    k = {u: qkv_ref[rows(u[0]), M_QK + u[1] * M_QK_DIM:M_QK + (u[1] + 1) * M_QK_DIM] for u in units}
    v = {u: qkv_ref[rows(u[0]), 2 * M_QK + u[1] * M_V_DIM:2 * M_QK + (u[1] + 1) * M_V_DIM] for u in units}
    i_col = {u: gc[rows(u[0]), u[1]:u[1] + 1] for u in units}
    b_col = {u: b_cols[rows(u[0]), M_HEADS + u[1]:M_HEADS + u[1] + 1] for u in units}
    ib_row = {u: (b_rows[M_HEADS + u[1]:M_HEADS + u[1] + 1, rows(u[0])]
                  - gr[u[1]:u[1] + 1, rows(u[0])]) for u in units}
    d = {u: jnp.where(tri, b_col[u] - ib_row[u], -jnp.inf) for u in units}
    d_max = {u: jnp.max(d[u], axis=1, keepdims=True) for u in units}
    qk = {u: lax.dot_general(q[u], k[u], (((1,), (1,)), ((), ())), preferred_element_type=F32) for u in units}
    s_loc = {u: qk[u] * scale * jnp.exp(d[u] - d_max[u]) for u in units}
    a_loc = {u: jnp.dot(s_loc[u].astype(BF16), v[u], preferred_element_type=F32) for u in units}
    r_loc = {u: jnp.sum(s_loc[u], axis=1, keepdims=True) for u in units}
    b_last = {u: b_col[u][CHUNK - 1:CHUNK, :] for u in units}
    g_col = {u: b_last[u] - b_col[u] + i_col[u] for u in units}
    g_max = {u: jnp.max(g_col[u], axis=0, keepdims=True) for u in units}
    wk = {u: jnp.exp(g_col[u] - g_max[u]) * k[u].astype(F32) for u in units}
    u_loc = {u: jnp.dot(wk[u].T.astype(BF16), v[u], preferred_element_type=F32) for u in units}
    nk_loc = {u: jnp.sum(wk[u], axis=0, keepdims=True) for u in units}

    c_state = [c_ref[h] for h in heads]
    n_state = [n_ref[h] for h in heads]
    m_state = [m_ref[h][:, 0:1] for h in heads]
    for c in range(chunks):
        us = [(c, h) for h in heads]
        inter = [b_col[u] + m_state[u[1]] for u in us]
        m_t = [jnp.maximum(inter[h], d_max[(c, h)]) for h in heads]
        w_intra = [jnp.exp(d_max[(c, h)] - m_t[h]) for h in heads]
        w_inter = [jnp.exp(inter[h] - m_t[h]) for h in heads]
        qc = [jnp.dot(q[(c, h)], c_state[h].astype(BF16), preferred_element_type=F32) * scale for h in heads]
        qn = [jnp.sum(q[(c, h)].astype(F32) * n_state[h], axis=1, keepdims=True) * scale for h in heads]
        m_new = [jnp.maximum(b_last[(c, h)] + m_state[h], g_max[(c, h)]) for h in heads]
        decay = [jnp.exp(b_last[(c, h)] + m_state[h] - m_new[h]) for h in heads]
        beta = [jnp.exp(g_max[(c, h)] - m_new[h]) for h in heads]
        c_state = [decay[h] * c_state[h] + beta[h] * u_loc[(c, h)] for h in heads]
        n_state = [decay[h] * n_state[h] + beta[h] * nk_loc[(c, h)] for h in heads]
        m_state = m_new
        num = [w_intra[h] * a_loc[(c, h)] + w_inter[h] * qc[h] for h in heads]
        den = [w_intra[h] * r_loc[(c, h)] + w_inter[h] * qn[h] for h in heads]
        hv = [num[h] * (1.0 / jnp.maximum(jnp.abs(den[h]), jnp.exp(-m_t[h]))) for h in heads]
        hv = [hv[h] * lax.rsqrt(jnp.mean(hv[h] * hv[h], axis=1, keepdims=True) + RMS_EPS) for h in heads]
        for h in heads:
            vs = slice(h * M_V_DIM, (h + 1) * M_V_DIM)
            mb_ref[rows(c), vs] = (osig_ref[rows(c), vs].astype(F32) * (hv[h] * mhg_ref[:, vs])).astype(mb_ref.dtype)

    for h in heads:
        c_ref[h], n_ref[h] = c_state[h], n_state[h]
        m_ref[h] = jnp.broadcast_to(m_state[h], (1, LANES))


def _mlstm(qkv, gates_col, gates_row, osig_src, mh_g, batch, seq, chunks=8):
    ts = chunks * CHUNK
    tiles = seq // ts
    n = batch * seq
    return pl.pallas_call(
        functools.partial(_mlstm_kernel, chunks=chunks),
        grid=(batch, tiles),
        in_specs=[
            pl.BlockSpec((ts, 2 * M_QK + M_V), lambda b, t: (b * tiles + t, 0)),
            pl.BlockSpec((ts, LANES), lambda b, t: (b * tiles + t, 0)),
            pl.BlockSpec((1, 2 * M_HEADS, ts), lambda b, t: (b, 0, t)),
            pl.BlockSpec((ts, M_V), lambda b, t: (b * tiles + t, 0)),
            pl.BlockSpec((1, M_V), lambda b, t: (0, 0)),
        ],
        out_specs=pl.BlockSpec((ts, M_V), lambda b, t: (b * tiles + t, 0)),
        out_shape=jax.ShapeDtypeStruct((n, M_V), BF16),
        scratch_shapes=[
            pltpu.VMEM((M_HEADS, M_QK_DIM, M_V_DIM), F32),
            pltpu.VMEM((M_HEADS, 1, M_QK_DIM), F32),
            pltpu.VMEM((M_HEADS, 1, LANES), F32),
        ],
        name="mlstm",
        compiler_params=_params("parallel", "arbitrary"),
    )(qkv, gates_col, gates_row, osig_src, mh_g)


def _merge_kernel(x_ref, a_ref, mb_ref, ga_ref, gb_ref, wpa_ref, wpb_ref, wout_ref, g_ref, b_ref,
                  xo_ref, xob_ref, xpa_ref, xpb_ref):
    ya = jnp.dot(a_ref[...], wpa_ref[...], preferred_element_type=F32)
    yb = jnp.dot(mb_ref[...], wpb_ref[...], preferred_element_type=F32)
    y = ga_ref[...].astype(F32) * ya + gb_ref[...].astype(F32) * yb
    mix = jnp.dot(y.astype(BF16), wout_ref[...], preferred_element_type=F32)
    xn = _layer_norm(ALPHA * x_ref[...] + mix, g_ref[...], b_ref[...])
    xo_ref[...] = xn
    xob_ref[...] = xn.astype(BF16)
    xpa_ref[...], xpb_ref[...] = _pack_row(xn)


def _merge(x, a, mb, gates, w_pa, w_pb, w_out, ln_g, ln_b, tm=512):
    n = x.shape[0]
    row = lambda i: (i, 0)
    const = lambda i: (0, 0)
    wspec = pl.BlockSpec((D_MODEL, D_MODEL), const)
    vspec = pl.BlockSpec((1, D_MODEL), const)
    return pl.pallas_call(
        _merge_kernel,
        grid=(n // tm,),
        in_specs=[
            pl.BlockSpec((tm, D_MODEL), row),
            pl.BlockSpec((tm, G_WIDTH), row),
            pl.BlockSpec((tm, M_V), row),
            pl.BlockSpec((tm, D_MODEL), lambda i: (i, 1)),
            pl.BlockSpec((tm, D_MODEL), lambda i: (i, 2)),
            wspec, wspec, wspec, vspec, vspec,
        ],
        out_specs=[pl.BlockSpec((tm, D_MODEL), row), pl.BlockSpec((tm, D_MODEL), row),
                   pl.BlockSpec((tm, PART), row), pl.BlockSpec((tm, PART), row)],
        out_shape=[jax.ShapeDtypeStruct((n, D_MODEL), F32), jax.ShapeDtypeStruct((n, D_MODEL), BF16),
                   jax.ShapeDtypeStruct((n, PART), I32), jax.ShapeDtypeStruct((n, PART), I32)],
        name="merge_ln",
        compiler_params=_params("parallel"),
    )(x, a, mb, gates, gates, w_pa, w_pb, w_out, ln_g, ln_b)


def _first_max(v, idx, axes, sentinel):
    m = jnp.max(v, axis=axes, keepdims=True)
    first = jnp.min(jnp.where(v == m, idx, sentinel), axis=axes, keepdims=True)
    return m, first


def _router_kernel(x_ref, wr_ref, br_ref, eidx_ref, rank_ref, wtok_ref, cnt_ref, run_ref, *, tm):
    @pl.when(pl.program_id(0) == 0)
    def _():
        run_ref[...] = jnp.zeros_like(run_ref)

    logits = lax.dot_general(wr_ref[...], x_ref[...], (((1,), (1,)), ((), ())),
                             preferred_element_type=F32, precision=lax.Precision.HIGHEST)
    scores = jax.nn.sigmoid(logits)
    sel = (scores + br_ref[...]).reshape(N_GROUPS, GROUP_SIZE, tm)
    scores3 = scores.reshape(N_GROUPS, GROUP_SIZE, tm)
    member = lax.broadcasted_iota(jnp.int32, sel.shape, 1)
    group = lax.broadcasted_iota(jnp.int32, sel.shape, 0)
    neg = -jnp.inf

    m1, f1 = _first_max(sel, member, 1, GROUP_SIZE)
    m2 = jnp.max(jnp.where(member == f1, neg, sel), axis=1, keepdims=True)
    gscore = m1 + m2
    gid = lax.broadcasted_iota(jnp.int32, gscore.shape, 0)
    gmask = jnp.zeros(gscore.shape, dtype=jnp.bool_)
    for _ in range(TOPK_GROUPS):
        _, fg = _first_max(gscore, gid, 0, N_GROUPS)
        hit = gid == fg
        gmask = gmask | hit
        gscore = jnp.where(hit, neg, gscore)

    cand = jnp.where(gmask, sel, neg)
    eid = group * GROUP_SIZE + member
    chosen = jnp.zeros(sel.shape, dtype=jnp.bool_)
    picks = []
    for _ in range(TOP_K):
        _, fe = _first_max(cand, eid, (0, 1), N_EXPERTS)
        hit = eid == fe
        chosen = chosen | hit
        cand = jnp.where(hit, neg, cand)
        picks.append((fe[0], hit, jnp.sum(jnp.where(hit, scores3, 0.0), axis=(0, 1), keepdims=True)[0]))

    chosen2 = jnp.where(chosen, 1.0, 0.0).reshape(N_EXPERTS, tm)
    r_idx = lax.broadcasted_iota(jnp.int32, (tm, tm), 0)
    c_idx = lax.broadcasted_iota(jnp.int32, (tm, tm), 1)
    upper = jnp.where(r_idx < c_idx, 1.0, 0.0).astype(BF16)
    prefix = jnp.dot(chosen2.astype(BF16), upper, preferred_element_type=F32) + run_ref[:, 0:1]
    prefix3 = prefix.reshape(N_GROUPS, GROUP_SIZE, tm)

    total = picks[0][2]
    for _, _, wk in picks[1:]:
        total = total + wk
    eidx_ref[...] = jnp.concatenate([fe for fe, _, _ in picks], axis=0)
    rank_ref[...] = jnp.concatenate(
        [jnp.sum(jnp.where(hit, prefix3, 0.0), axis=(0, 1), keepdims=True)[0] for _, hit, _ in picks],
        axis=0).astype(I32)
    w_rows = jnp.concatenate([wk / total * ROUTE_SCALE for _, _, wk in picks]
                             + [jnp.zeros((LANES - TOP_K, tm), F32)], axis=0)
    wtok_ref[...] = w_rows.T

    run = run_ref[...] + jnp.sum(chosen2, axis=1, keepdims=True)
    run_ref[...] = run
    cnt_ref[...] = run


def _router(x, wr_t, br, tm=512):
    n = x.shape[0]
    tok = lambda i: (0, i)
    return pl.pallas_call(
        functools.partial(_router_kernel, tm=tm),
        grid=(n // tm,),
        in_specs=[
            pl.BlockSpec((tm, D_MODEL), lambda i: (i, 0)),
            pl.BlockSpec((N_EXPERTS, D_MODEL), lambda i: (0, 0)),
            pl.BlockSpec((N_EXPERTS, 1), lambda i: (0, 0)),
        ],
        out_specs=[
            pl.BlockSpec((TOP_K, tm), tok),
            pl.BlockSpec((TOP_K, tm), tok),
            pl.BlockSpec((tm, LANES), lambda i: (i, 0)),
            pl.BlockSpec((N_EXPERTS, LANES), lambda i: (0, 0)),
        ],
        out_shape=[
            jax.ShapeDtypeStruct((TOP_K, n), I32),
            jax.ShapeDtypeStruct((TOP_K, n), I32),
            jax.ShapeDtypeStruct((n, LANES), F32),
            jax.ShapeDtypeStruct((N_EXPERTS, LANES), F32),
        ],
        scratch_shapes=[pltpu.VMEM((N_EXPERTS, LANES), F32)],
        name="router",
        compiler_params=_params("arbitrary"),
    )(x, wr_t, br)


def _slots_kernel(starts_ref, eidx_ref, rank_ref, slot_ref):
    eidx = eidx_ref[...]
    slot = rank_ref[...]
    for e in range(N_EXPERTS):
        slot = slot + jnp.where(eidx == e, starts_ref[e], 0)
    slot_ref[...] = slot


def _slots(starts, eidx, rank, tn=2048):
    n = eidx.shape[1]
    tn = min(tn, n)
    spec = pl.BlockSpec((TOP_K, tn), lambda i, s: (0, i))
    return pl.pallas_call(
        _slots_kernel,
        grid_spec=pltpu.PrefetchScalarGridSpec(
            num_scalar_prefetch=1, grid=(n // tn,), in_specs=[spec, spec], out_specs=spec),
        out_shape=jax.ShapeDtypeStruct((TOP_K, n), I32),
        name="slots",
        compiler_params=_params("parallel"),
    )(starts, eidx, rank)


def _sc_mesh():
    return plsc.VectorSubcoreMesh(core_axis_name="c", subcore_axis_name="s")


def _sc_dispatch(parts, slots_flat, rows):
    n, width = parts[0].shape
    blocks = n // SC_WINDOW
    out_type = [jax.ShapeDtypeStruct((rows, width), part.dtype) for part in parts]

    @functools.partial(pl.kernel, out_type=out_type, mesh=_sc_mesh(), scratch_types=[], name="sc_dispatch")
    def run(*refs):
        i_hbm = refs[len(parts)]
        for x_hbm, o_hbm in zip(refs[:len(parts)], refs[len(parts) + 1:]):
            def body(x_vmem, i_vmem, o_hbm=o_hbm):
                pltpu.sync_copy(x_vmem, o_hbm.at[i_vmem.at[0]])

            pltpu.emit_pipeline(
                body,
                grid=(blocks, TOP_K),
                in_specs=[pl.BlockSpec((SC_WINDOW, width), lambda i, k: (i, 0)),
                          pl.BlockSpec((1, SC_WINDOW), lambda i, k: (0, k * blocks + i))],
                out_specs=[],
                core_axis_name=("c", "s"),
                dimension_semantics=(pltpu.PARALLEL, pltpu.ARBITRARY),
                trace_scopes=False,
            )(x_hbm, i_hbm)

    return run(*parts, slots_flat)


def _sc_gather(tables, idx_flat):
    count = idx_flat.shape[1]
    width = tables[0].shape[1]
    out_type = [jax.ShapeDtypeStruct((count, width), table.dtype) for table in tables]

    @functools.partial(pl.kernel, out_type=out_type, mesh=_sc_mesh(), scratch_types=[], name="sc_gather")
    def run(*refs):
        i_hbm = refs[len(tables)]
        for t_hbm, o_hbm in zip(refs[:len(tables)], refs[len(tables) + 1:]):
            def body(i_vmem, o_vmem, t_hbm=t_hbm):
                pltpu.sync_copy(t_hbm.at[i_vmem.at[0]], o_vmem)

            pltpu.emit_pipeline(
                body,
                grid=(count // SC_WINDOW,),
                in_specs=[pl.BlockSpec((1, SC_WINDOW), lambda i: (0, i))],
                out_specs=[pl.BlockSpec((SC_WINDOW, width), lambda i: (i, 0))],
                core_axis_name=("c", "s"),
                dimension_semantics=(pltpu.PARALLEL,),
                trace_scopes=False,
            )(i_hbm, o_hbm)

    return run(*tables, idx_flat)


def _expert_kernel(te_ref, used_ref, xa_ref, xb_ref, wg_ref, wu_ref, wd_ref, oa_ref, ob_ref):
    del te_ref
    @pl.when(pl.program_id(0) < used_ref[0])
    def _():
        x = jnp.concatenate([_unpack_pairs(xa_ref[...]), _unpack_pairs(xb_ref[...])], axis=1).astype(BF16)
        hg = jnp.dot(x, wg_ref[0], preferred_element_type=F32)
        hu = jnp.dot(x, wu_ref[0], preferred_element_type=F32)
        hid = (_silu(hg) * hu).astype(BF16)
        oa_ref[...], ob_ref[...] = _pack_row(jnp.dot(hid, wd_ref[0], preferred_element_type=F32))


def _experts(tile_expert, tiles_used, xa, xb, wg, wu, wd):
    rows = xa.shape[0]
    row = lambda j, te, used: (j, 0)
    exp = lambda j, te, used: (te[j], 0, 0)
    return pl.pallas_call(
        _expert_kernel,
        grid_spec=pltpu.PrefetchScalarGridSpec(
            num_scalar_prefetch=2,
            grid=(rows // ROW_TILE,),
            in_specs=[
                pl.BlockSpec((ROW_TILE, PART), row),
                pl.BlockSpec((ROW_TILE, PART), row),
                pl.BlockSpec((1, D_MODEL, D_EXPERT), exp),
                pl.BlockSpec((1, D_MODEL, D_EXPERT), exp),
                pl.BlockSpec((1, D_EXPERT, D_MODEL), exp),
            ],
            out_specs=[pl.BlockSpec((ROW_TILE, PART), row), pl.BlockSpec((ROW_TILE, PART), row)],
        ),
        out_shape=[jax.ShapeDtypeStruct((rows, PART), I32), jax.ShapeDtypeStruct((rows, PART), I32)],
        name="experts",
        compiler_params=_params("parallel"),
    )(tile_expert, tiles_used, xa, xb, wg, wu, wd)


def _finish_kernel(x_ref, xb_ref, p_ref, ga_ref, gb_ref, wtok_ref, wgs_ref, wus_ref, wds_ref, wpg_ref, bpg_ref, wp_ref,
                   lng_ref, lnb_ref, xo_ref, xob_ref):
    xb = xb_ref[...]
    hs = _silu(jnp.dot(xb, wgs_ref[...], preferred_element_type=F32)) * jnp.dot(
        xb, wus_ref[...], preferred_element_type=F32)
    shared = jnp.dot(hs.astype(BF16), wds_ref[...], preferred_element_type=F32)
    gate = jax.nn.sigmoid(jnp.dot(xb, wpg_ref[...], preferred_element_type=F32) + bpg_ref[...])
    ple = gate * jnp.dot(p_ref[...].astype(BF16), wp_ref[...], preferred_element_type=F32)

    wtok = wtok_ref[...]
    r_a = None
    r_b = None
    for k in range(TOP_K):
        wk = wtok[:, k:k + 1]
        a = wk * _unpack_pairs(ga_ref[k])
        b = wk * _unpack_pairs(gb_ref[k])
        r_a = a if r_a is None else r_a + a
        r_b = b if r_b is None else r_b + b
    routed = jnp.concatenate([r_a, r_b], axis=1)

    xn = _layer_norm(ALPHA * x_ref[...] + (routed + shared + ple), lng_ref[...], lnb_ref[...])
    xo_ref[...] = xn
    xob_ref[...] = xn.astype(BF16)


def _finish(x, xb, p, ga, gb, wtok, wgs, wus, wds, wpg, bpg, wp, ln_g, ln_b, tm=256):
    n = x.shape[0]
    row = lambda i: (i, 0)
    const = lambda i: (0, 0)
    return pl.pallas_call(
        _finish_kernel,
        grid=(n // tm,),
        in_specs=[
            pl.BlockSpec((tm, D_MODEL), row),
            pl.BlockSpec((tm, D_MODEL), row),
            pl.BlockSpec((tm, P_DIM), row),
            pl.BlockSpec((TOP_K, tm, PART), lambda i: (0, i, 0)),
            pl.BlockSpec((TOP_K, tm, PART), lambda i: (0, i, 0)),
            pl.BlockSpec((tm, LANES), row),
            pl.BlockSpec((D_MODEL, D_SHARED), const),
            pl.BlockSpec((D_MODEL, D_SHARED), const),
            pl.BlockSpec((D_SHARED, D_MODEL), const),
            pl.BlockSpec((D_MODEL, D_MODEL), const),
            pl.BlockSpec((1, D_MODEL), const),
            pl.BlockSpec((P_DIM, D_MODEL), const),
            pl.BlockSpec((1, D_MODEL), const),
            pl.BlockSpec((1, D_MODEL), const),
        ],
        out_specs=[pl.BlockSpec((tm, D_MODEL), row), pl.BlockSpec((tm, D_MODEL), row)],
        out_shape=[jax.ShapeDtypeStruct((n, D_MODEL), F32), jax.ShapeDtypeStruct((n, D_MODEL), BF16)],
        name="moe_finish_ln",
        compiler_params=_params("parallel"),
    )(x, xb, p, ga, gb, wtok, wgs, wus, wds, wpg, bpg, wp, ln_g, ln_b)


def _moe(x, xb, xpa, xpb, p, w_router_t, b_router, wg, wu, wd, wgs, wus, wds, wpg, bpg, wp, ln_g, ln_b):
    n = x.shape[0]
    pairs = n * TOP_K
    tiles = pairs // ROW_TILE + N_EXPERTS
    eidx, rank, wtok, counts = _router(x, w_router_t, b_router)

    cnt = counts[:, 0].astype(I32)
    group_tiles = (cnt + ROW_TILE - 1) // ROW_TILE
    tile_end = jnp.cumsum(group_tiles)
    starts = (tile_end - group_tiles) * ROW_TILE
    tile_id = jnp.arange(tiles, dtype=I32)
    tile_expert = jnp.minimum(jnp.sum((tile_end[None, :] <= tile_id[:, None]).astype(I32), axis=1), N_EXPERTS - 1)

    slots_flat = _slots(starts, eidx, rank).reshape(1, pairs)
    xsa, xsb = _sc_dispatch([xpa, xpb], slots_flat, tiles * ROW_TILE)
    ysa, ysb = _experts(tile_expert, tile_end[N_EXPERTS - 1:], xsa, xsb, wg, wu, wd)
    ga, gb = _sc_gather([ysa, ysb], slots_flat)
    return _finish(x, xb, p, ga.reshape(TOP_K, n, PART), gb.reshape(TOP_K, n, PART), wtok, wgs, wus, wds, wpg, bpg, wp, ln_g, ln_b)


def kernel(x, p, w_in, b_in, sg_ln_g, sg_ln_b, w_s, b_s, mh_g, w_pa, w_pb, w_out, ln1_g, ln1_b, w_router, b_router, w_gate_e, w_up_e, w_down_e, w_gate_s, w_up_s, w_down_s, w_pg, b_pg, w_p, ln2_g, ln2_b):
    batch, seq, _ = x.shape
    streams = STREAMS if batch % STREAMS == 0 else 1
    sb = batch // streams
    n = sb * seq
    xf = [x[s * sb:(s + 1) * sb].reshape(n, D_MODEL) for s in range(streams)]
    xb = [xs.astype(BF16) for xs in xf]
    pf = [p[:, s * sb:(s + 1) * sb].reshape(DEPTH, n, P_DIM) for s in range(streams)]

    c_uv = 2 * G_WIDTH
    c_qkv = c_uv + 2 * M_QK + M_V
    c_o = c_qkv + M_V
    c_if = c_o + 2 * M_HEADS

    for l in range(DEPTH):
        w = w_in[l]
        b = b_in[l][None, :]
        w_uv, b_uv = w[:, :c_uv].astype(BF16), b[:, :c_uv]
        w_qkv, b_qkv = w[:, c_uv:c_qkv].astype(BF16), b[:, c_uv:c_qkv]
        w_gate = jnp.concatenate([w[:, c_qkv:c_o], w[:, c_if:]], axis=1).astype(BF16)
        b_gate = jnp.concatenate([b[:, c_qkv:c_o], b[:, c_if:]], axis=1)
        w_if = jnp.pad(w[:, c_o:c_if], ((0, 0), (0, LANES - 2 * M_HEADS))).astype(BF16)
        b_if = jnp.pad(b[:, c_o:c_if], ((0, 0), (0, LANES - 2 * M_HEADS)))
        mix_w = (w_pa[l].astype(BF16), w_pb[l].astype(BF16), w_out[l].astype(BF16),
                 ln1_g[l][None, :], ln1_b[l][None, :])
        moe_w = (w_router[l].T, b_router[l][:, None],
                 w_gate_e[l].astype(BF16), w_up_e[l].astype(BF16), w_down_e[l].astype(BF16),
                 w_gate_s[l].astype(BF16), w_up_s[l].astype(BF16), w_down_s[l].astype(BF16),
                 w_pg[l].astype(BF16), b_pg[l][None, :], w_p[l].astype(BF16),
                 ln2_g[l][None, :], ln2_b[l][None, :])

        packed = []
        for s in range(streams):
            uv = _proj(xb[s], w_uv, b_uv, "gelu", BF16)
            qkv = _proj(xb[s], w_qkv, b_qkv, None, BF16)
            gates = _proj(xb[s], w_gate, b_gate, "sigmoid", BF16)
            gif = _proj(xb[s], w_if, b_if, None, F32)
            gif_rows = gif[:, :2 * M_HEADS].reshape(sb, seq, 2 * M_HEADS).transpose(0, 2, 1)
            a = _sgu(uv, sg_ln_g[l][None, :], sg_ln_b[l][None, :], w_s[l], b_s[l].T)
            mb = _mlstm(qkv, gif, gif_rows, gates, mh_g[l][None, :], sb, seq)
            xf[s], xb[s], xpa, xpb = _merge(xf[s], a, mb, gates, *mix_w)
            packed.append((xpa, xpb))
        for s in range(streams):
            xf[s], xb[s] = _moe(xf[s], xb[s], *packed[s], pf[s][l], *moe_w)
    return jnp.concatenate([xs.reshape(sb, seq, D_MODEL) for xs in xf], axis=0)
```

```python
import functools

import jax
import jax.numpy as jnp
from jax import lax
from jax.experimental import pallas as pl
from jax.experimental.pallas import tpu as pltpu
from jax.experimental.pallas import tpu_sc as plsc

D_MODEL = 1024
DEPTH = 4
CHUNK = 64
P_DIM = 256
G_WIDTH = 1024
G_GROUPS = 8
G_GROUP_DIM = G_WIDTH // G_GROUPS
G_BLOCK = 128
M_HEADS = 4
M_QK_DIM = 128
M_V_DIM = 256
M_QK = M_HEADS * M_QK_DIM
M_V = M_HEADS * M_V_DIM
N_EXPERTS = 64
TOP_K = 8
N_GROUPS = 8
TOPK_GROUPS = 4
GROUP_SIZE = N_EXPERTS // N_GROUPS
D_EXPERT = 256
D_SHARED = 256
ROUTE_SCALE = 2.5
ALPHA = (2 * DEPTH) ** 0.25
LN_EPS = 1e-5
RMS_EPS = 1e-6

LANES = 128
VMEM_LIMIT = 56 * 1024 * 1024
HALF = D_MODEL // 2
PART = HALF // 2
ROW_TILE = 512
STREAMS = 2
SC_WINDOW = 128

F32 = jnp.float32
BF16 = jnp.bfloat16
I32 = jnp.int32


def _params(*semantics):
    return pltpu.CompilerParams(dimension_semantics=semantics, vmem_limit_bytes=VMEM_LIMIT)


def _layer_norm(x, g, b):
    mu = jnp.mean(x, axis=-1, keepdims=True)
    xc = x - mu
    var = jnp.mean(xc * xc, axis=-1, keepdims=True)
    return xc * lax.rsqrt(var + LN_EPS) * g + b


def _gelu(x):
    return 0.5 * x * (1.0 + lax.erf(x * (2.0 ** -0.5)))


def _silu(x):
    return x * jax.nn.sigmoid(x)


def _pack_pairs(x):
    lo = lax.bitcast_convert_type(x[:, :PART].astype(BF16).astype(F32), I32)
    hi = lax.bitcast_convert_type(x[:, PART:].astype(BF16).astype(F32), I32)
    return lax.shift_right_logical(lo, 16) | (hi & jnp.int32(-65536))


def _pack_row(x):
    return _pack_pairs(x[:, :HALF]), _pack_pairs(x[:, HALF:])


def _unpack_pairs(w):
    lo = lax.bitcast_convert_type(lax.shift_left(w, 16), F32)
    hi = lax.bitcast_convert_type(w & jnp.int32(-65536), F32)
    return jnp.concatenate([lo, hi], axis=1)


def _proj_kernel(x_ref, w_ref, b_ref, o_ref, *, act):
    acc = jnp.dot(x_ref[...], w_ref[...], preferred_element_type=F32) + b_ref[...]
    if act == "gelu":
        acc = _gelu(acc)
    elif act == "sigmoid":
        acc = jax.nn.sigmoid(acc)
    o_ref[...] = acc.astype(o_ref.dtype)


def _proj(x, w, b, act, out_dtype, tm=1024, tn=1024):
    n, k = x.shape
    nout = w.shape[1]
    tn = min(tn, nout)
    return pl.pallas_call(
        functools.partial(_proj_kernel, act=act),
        grid=(n // tm, nout // tn),
        in_specs=[
            pl.BlockSpec((tm, k), lambda i, j: (i, 0)),
            pl.BlockSpec((k, tn), lambda i, j: (0, j)),
            pl.BlockSpec((1, tn), lambda i, j: (0, j)),
        ],
        out_specs=pl.BlockSpec((tm, tn), lambda i, j: (i, j)),
        out_shape=jax.ShapeDtypeStruct((n, nout), out_dtype),
        name="proj_" + (act or "linear"),
        compiler_params=_params("parallel", "arbitrary"),
    )(x, w, b)


def _sgu_kernel(x_ref, wu_ref, bu_ref, wv_ref, bv_ref, lng_ref, lnb_ref, ws_ref, bs_ref, a_ref, *, blocks):
    x = x_ref[...]
    v = _gelu(jnp.dot(x, wv_ref[...], preferred_element_type=F32) + bv_ref[...])
    vln = _layer_norm(v, lng_ref[...], lnb_ref[...]).astype(BF16)
    u = _gelu(jnp.dot(x, wu_ref[...], preferred_element_type=F32) + bu_ref[...])
    t_chunk = lax.broadcasted_iota(jnp.int32, (G_BLOCK, G_BLOCK), 0) // CHUNK
    s_chunk = lax.broadcasted_iota(jnp.int32, (G_BLOCK, G_BLOCK), 1) // CHUNK
    causal = s_chunk <= t_chunk
    for g in range(G_GROUPS):
        w = jnp.where(causal, ws_ref[g], 0.0).astype(BF16)
        bias = bs_ref[:, g:g + 1]
        cs = slice(g * G_GROUP_DIM, (g + 1) * G_GROUP_DIM)
        for blk in range(blocks):
            rs = slice(blk * G_BLOCK, (blk + 1) * G_BLOCK)
            mixed = jnp.dot(w, vln[rs, cs], preferred_element_type=F32) + bias
            a_ref[rs, cs] = (u[rs, cs] * mixed).astype(a_ref.dtype)


def _sgu(x, w_u, b_u, w_v, b_v, ln_g, ln_b, w_s, b_s_t, blocks=4):
    n = x.shape[0]
    tp = blocks * G_BLOCK
    const = lambda i: (0, 0)
    wspec = pl.BlockSpec((D_MODEL, G_WIDTH), const)
    vspec = pl.BlockSpec((1, G_WIDTH), const)
    return pl.pallas_call(
        functools.partial(_sgu_kernel, blocks=blocks),
        grid=(n // tp,),
        in_specs=[
            pl.BlockSpec((tp, D_MODEL), lambda i: (i, 0)),
            wspec, vspec, wspec, vspec, vspec, vspec,
            pl.BlockSpec((G_GROUPS, G_BLOCK, G_BLOCK), lambda i: (0, 0, 0)),
            pl.BlockSpec((G_BLOCK, G_GROUPS), const),
        ],
        out_specs=pl.BlockSpec((tp, G_WIDTH), lambda i: (i, 0)),
        out_shape=jax.ShapeDtypeStruct((n, G_WIDTH), BF16),
        name="spatial_gating",
        compiler_params=_params("parallel"),
    )(x, w_u, b_u, w_v, b_v, ln_g, ln_b, w_s, b_s_t)


def _segment_cumsum(x, axis, seg):
    pos = lax.broadcasted_iota(jnp.int32, x.shape, axis) % seg
    shift = 1
    while shift < seg:
        x = x + jnp.where(pos >= shift, pltpu.roll(x, shift, axis), 0.0)
        shift *= 2
    return x


def _mlstm_kernel(qkv_ref, gc_ref, gr_ref, osig_ref, mhg_ref, mb_ref, c_ref, n_ref, m_ref, *, chunks):
    @pl.when(pl.program_id(1) == 0)
    def _():
        c_ref[...] = jnp.zeros_like(c_ref)
        n_ref[...] = jnp.zeros_like(n_ref)
        m_ref[...] = jnp.zeros_like(m_ref)

    scale = M_QK_DIM ** -0.5
    gc = gc_ref[...]
    gr = gr_ref[0]
    b_cols = _segment_cumsum(jax.nn.log_sigmoid(gc), 0, CHUNK)
    b_rows = _segment_cumsum(jax.nn.log_sigmoid(gr), 1, CHUNK)
    t_idx = lax.broadcasted_iota(jnp.int32, (CHUNK, CHUNK), 0)
    s_idx = lax.broadcasted_iota(jnp.int32, (CHUNK, CHUNK), 1)
    tri = s_idx <= t_idx

    heads = range(M_HEADS)
    units = [(c, h) for c in range(chunks) for h in heads]
    rows = lambda c: slice(c * CHUNK, (c + 1) * CHUNK)
    q = {u: qkv_ref[rows(u[0]), u[1] * M_QK_DIM:(u[1] + 1) * M_QK_DIM] for u in units}
    k = {u: qkv_ref[rows(u[0]), M_QK + u[1] * M_QK_DIM:M_QK + (u[1] + 1) * M_QK_DIM] for u in units}
    v = {u: qkv_ref[rows(u[0]), 2 * M_QK + u[1] * M_V_DIM:2 * M_QK + (u[1] + 1) * M_V_DIM] for u in units}
    i_col = {u: gc[rows(u[0]), u[1]:u[1] + 1] for u in units}
    b_col = {u: b_cols[rows(u[0]), M_HEADS + u[1]:M_HEADS + u[1] + 1] for u in units}
    ib_row = {u: (b_rows[M_HEADS + u[1]:M_HEADS + u[1] + 1, rows(u[0])]
                  - gr[u[1]:u[1] + 1, rows(u[0])]) for u in units}
    d = {u: jnp.where(tri, b_col[u] - ib_row[u], -jnp.inf) for u in units}
    d_max = {u: jnp.max(d[u], axis=1, keepdims=True) for u in units}
    qk = {u: lax.dot_general(q[u], k[u], (((1,), (1,)), ((), ())), preferred_element_type=F32) for u in units}
    s_loc = {u: qk[u] * scale * jnp.exp(d[u] - d_max[u]) for u in units}
    a_loc = {u: jnp.dot(s_loc[u].astype(BF16), v[u], preferred_element_type=F32) for u in units}
    r_loc = {u: jnp.sum(s_loc[u], axis=1, keepdims=True) for u in units}
    b_last = {u: b_col[u][CHUNK - 1:CHUNK, :] for u in units}
    g_col = {u: b_last[u] - b_col[u] + i_col[u] for u in units}
    g_max = {u: jnp.max(g_col[u], axis=0, keepdims=True) for u in units}
    wk = {u: jnp.exp(g_col[u] - g_max[u]) * k[u].astype(F32) for u in units}
    u_loc = {u: jnp.dot(wk[u].T.astype(BF16), v[u], preferred_element_type=F32) for u in units}
    nk_loc = {u: jnp.sum(wk[u], axis=0, keepdims=True) for u in units}

    c_state = [c_ref[h] for h in heads]
    n_state = [n_ref[h] for h in heads]
    m_state = [m_ref[h][:, 0:1] for h in heads]
    for c in range(chunks):
        us = [(c, h) for h in heads]
        inter = [b_col[u] + m_state[u[1]] for u in us]
        m_t = [jnp.maximum(inter[h], d_max[(c, h)]) for h in heads]
        w_intra = [jnp.exp(d_max[(c, h)] - m_t[h]) for h in heads]
        w_inter = [jnp.exp(inter[h] - m_t[h]) for h in heads]
        qc = [jnp.dot(q[(c, h)], c_state[h].astype(BF16), preferred_element_type=F32) * scale for h in heads]
        qn = [jnp.sum(q[(c, h)].astype(F32) * n_state[h], axis=1, keepdims=True) * scale for h in heads]
        m_new = [jnp.maximum(b_last[(c, h)] + m_state[h], g_max[(c, h)]) for h in heads]
        decay = [jnp.exp(b_last[(c, h)] + m_state[h] - m_new[h]) for h in heads]
        beta = [jnp.exp(g_max[(c, h)] - m_new[h]) for h in heads]
        c_state = [decay[h] * c_state[h] + beta[h] * u_loc[(c, h)] for h in heads]
        n_state = [decay[h] * n_state[h] + beta[h] * nk_loc[(c, h)] for h in heads]
        m_state = m_new
        num = [w_intra[h] * a_loc[(c, h)] + w_inter[h] * qc[h] for h in heads]
        den = [w_intra[h] * r_loc[(c, h)] + w_inter[h] * qn[h] for h in heads]
        hv = [num[h] * (1.0 / jnp.maximum(jnp.abs(den[h]), jnp.exp(-m_t[h]))) for h in heads]
        hv = [hv[h] * lax.rsqrt(jnp.mean(hv[h] * hv[h], axis=1, keepdims=True) + RMS_EPS) for h in heads]
        for h in heads:
            vs = slice(h * M_V_DIM, (h + 1) * M_V_DIM)
            mb_ref[rows(c), vs] = (osig_ref[rows(c), vs].astype(F32) * (hv[h] * mhg_ref[:, vs])).astype(mb_ref.dtype)

    for h in heads:
        c_ref[h], n_ref[h] = c_state[h], n_state[h]
        m_ref[h] = jnp.broadcast_to(m_state[h], (1, LANES))


def _mlstm(qkv, gates_col, gates_row, osig_src, mh_g, batch, seq, chunks=8):
    ts = chunks * CHUNK
    tiles = seq // ts
    n = batch * seq
    return pl.pallas_call(
        functools.partial(_mlstm_kernel, chunks=chunks),
        grid=(batch, tiles),
        in_specs=[
            pl.BlockSpec((ts, 2 * M_QK + M_V), lambda b, t: (b * tiles + t, 0)),
            pl.BlockSpec((ts, LANES), lambda b, t: (b * tiles + t, 0)),
            pl.BlockSpec((1, 2 * M_HEADS, ts), lambda b, t: (b, 0, t)),
            pl.BlockSpec((ts, M_V), lambda b, t: (b * tiles + t, 0)),
            pl.BlockSpec((1, M_V), lambda b, t: (0, 0)),
        ],
        out_specs=pl.BlockSpec((ts, M_V), lambda b, t: (b * tiles + t, 0)),
        out_shape=jax.ShapeDtypeStruct((n, M_V), BF16),
        scratch_shapes=[
            pltpu.VMEM((M_HEADS, M_QK_DIM, M_V_DIM), F32),
            pltpu.VMEM((M_HEADS, 1, M_QK_DIM), F32),
            pltpu.VMEM((M_HEADS, 1, LANES), F32),
        ],
        name="mlstm",
        compiler_params=_params("parallel", "arbitrary"),
    )(qkv, gates_col, gates_row, osig_src, mh_g)


def _merge_kernel(x_ref, xin_ref, a_ref, mb_ref, wga_ref, bga_ref, wgb_ref, bgb_ref, wpa_ref, wpb_ref, wout_ref,
                  g_ref, b_ref, xo_ref, xob_ref, xpa_ref, xpb_ref):
    xin = xin_ref[...]
    ga = jax.nn.sigmoid(jnp.dot(xin, wga_ref[...], preferred_element_type=F32) + bga_ref[...])
    gb = jax.nn.sigmoid(jnp.dot(xin, wgb_ref[...], preferred_element_type=F32) + bgb_ref[...])
    ya = jnp.dot(a_ref[...], wpa_ref[...], preferred_element_type=F32)
    yb = jnp.dot(mb_ref[...], wpb_ref[...], preferred_element_type=F32)
    y = ga * ya + gb * yb
    mix = jnp.dot(y.astype(BF16), wout_ref[...], preferred_element_type=F32)
    xn = _layer_norm(ALPHA * x_ref[...] + mix, g_ref[...], b_ref[...])
    xo_ref[...] = xn
    xob_ref[...] = xn.astype(BF16)
    xpa_ref[...], xpb_ref[...] = _pack_row(xn)


def _merge(x, xin, a, mb, w_ga, b_ga, w_gb, b_gb, w_pa, w_pb, w_out, ln_g, ln_b, tm=512):
    n = x.shape[0]
    row = lambda i: (i, 0)
    const = lambda i: (0, 0)
    wspec = pl.BlockSpec((D_MODEL, D_MODEL), const)
    vspec = pl.BlockSpec((1, D_MODEL), const)
    return pl.pallas_call(
        _merge_kernel,
        grid=(n // tm,),
        in_specs=[
            pl.BlockSpec((tm, D_MODEL), row),
            pl.BlockSpec((tm, D_MODEL), row),
            pl.BlockSpec((tm, G_WIDTH), row),
            pl.BlockSpec((tm, M_V), row),
            wspec, vspec, wspec, vspec, wspec, wspec, wspec, vspec, vspec,
        ],
        out_specs=[pl.BlockSpec((tm, D_MODEL), row), pl.BlockSpec((tm, D_MODEL), row),
                   pl.BlockSpec((tm, PART), row), pl.BlockSpec((tm, PART), row)],
        out_shape=[jax.ShapeDtypeStruct((n, D_MODEL), F32), jax.ShapeDtypeStruct((n, D_MODEL), BF16),
                   jax.ShapeDtypeStruct((n, PART), I32), jax.ShapeDtypeStruct((n, PART), I32)],
        name="merge_ln",
        compiler_params=_params("parallel"),
    )(x, xin, a, mb, w_ga, b_ga, w_gb, b_gb, w_pa, w_pb, w_out, ln_g, ln_b)


def _first_max(v, idx, axes, sentinel):
    m = jnp.max(v, axis=axes, keepdims=True)
    first = jnp.min(jnp.where(v == m, idx, sentinel), axis=axes, keepdims=True)
    return m, first


def _router_kernel(x_ref, wr_ref, br_ref, upper_ref, eidx_ref, rank_ref, wtok_ref, cnt_ref, run_ref, *, tm):
    @pl.when(pl.program_id(0) == 0)
    def _():
        run_ref[...] = jnp.zeros_like(run_ref)

    logits = lax.dot_general(wr_ref[...], x_ref[...], (((1,), (1,)), ((), ())),
                             preferred_element_type=F32, precision=lax.Precision.HIGHEST)
    scores = jax.nn.sigmoid(logits)
    sel = (scores + br_ref[...]).reshape(N_GROUPS, GROUP_SIZE, tm)
    scores3 = scores.reshape(N_GROUPS, GROUP_SIZE, tm)
    member = lax.broadcasted_iota(jnp.int32, sel.shape, 1)
    group = lax.broadcasted_iota(jnp.int32, sel.shape, 0)
    neg = -jnp.inf

    m1, f1 = _first_max(sel, member, 1, GROUP_SIZE)
    m2 = jnp.max(jnp.where(member == f1, neg, sel), axis=1, keepdims=True)
    gscore = m1 + m2
    gid = lax.broadcasted_iota(jnp.int32, gscore.shape, 0)
    gmask = jnp.zeros(gscore.shape, dtype=jnp.bool_)
    for _ in range(TOPK_GROUPS):
        _, fg = _first_max(gscore, gid, 0, N_GROUPS)
        hit = gid == fg
        gmask = gmask | hit
        gscore = jnp.where(hit, neg, gscore)

    cand = jnp.where(gmask, sel, neg)
    eid = group * GROUP_SIZE + member
    chosen = jnp.zeros(sel.shape, dtype=jnp.bool_)
    picks = []
    for _ in range(TOP_K):
        _, fe = _first_max(cand, eid, (0, 1), N_EXPERTS)
        hit = eid == fe
        chosen = chosen | hit
        cand = jnp.where(hit, neg, cand)
        picks.append((fe[0], hit, jnp.sum(jnp.where(hit, scores3, 0.0), axis=(0, 1), keepdims=True)[0]))

    chosen2 = jnp.where(chosen, 1.0, 0.0).reshape(N_EXPERTS, tm)
    prefix = jnp.dot(chosen2.astype(BF16), upper_ref[...], preferred_element_type=F32) + run_ref[:, 0:1]
    prefix3 = prefix.reshape(N_GROUPS, GROUP_SIZE, tm)

    total = picks[0][2]
    for _, _, wk in picks[1:]:
        total = total + wk
    eidx_ref[...] = jnp.concatenate([fe for fe, _, _ in picks], axis=0)
    rank_ref[...] = jnp.concatenate(
        [jnp.sum(jnp.where(hit, prefix3, 0.0), axis=(0, 1), keepdims=True)[0] for _, hit, _ in picks],
        axis=0).astype(I32)
    w_rows = jnp.concatenate([wk / total * ROUTE_SCALE for _, _, wk in picks]
                             + [jnp.zeros((LANES - TOP_K, tm), F32)], axis=0)
    wtok_ref[...] = w_rows.T

    run = run_ref[...] + jnp.sum(chosen2, axis=1, keepdims=True)
    run_ref[...] = run
    cnt_ref[...] = run


def _router(x, wr_t, br, tm=512):
    n = x.shape[0]
    tok = lambda i: (0, i)
    upper = (jnp.arange(tm)[:, None] < jnp.arange(tm)[None, :]).astype(BF16)
    return pl.pallas_call(
        functools.partial(_router_kernel, tm=tm),
        grid=(n // tm,),
        in_specs=[
            pl.BlockSpec((tm, D_MODEL), lambda i: (i, 0)),
            pl.BlockSpec((N_EXPERTS, D_MODEL), lambda i: (0, 0)),
            pl.BlockSpec((N_EXPERTS, 1), lambda i: (0, 0)),
            pl.BlockSpec((tm, tm), lambda i: (0, 0)),
        ],
        out_specs=[
            pl.BlockSpec((TOP_K, tm), tok),
            pl.BlockSpec((TOP_K, tm), tok),
            pl.BlockSpec((tm, LANES), lambda i: (i, 0)),
            pl.BlockSpec((N_EXPERTS, LANES), lambda i: (0, 0)),
        ],
        out_shape=[
            jax.ShapeDtypeStruct((TOP_K, n), I32),
            jax.ShapeDtypeStruct((TOP_K, n), I32),
            jax.ShapeDtypeStruct((n, LANES), F32),
            jax.ShapeDtypeStruct((N_EXPERTS, LANES), F32),
        ],
        scratch_shapes=[pltpu.VMEM((N_EXPERTS, LANES), F32)],
        name="router",
        compiler_params=_params("arbitrary"),
    )(x, wr_t, br, upper)


def _slots_kernel(starts_ref, eidx_ref, rank_ref, slot_ref):
    eidx = eidx_ref[...]
    slot = rank_ref[...]
    for e in range(N_EXPERTS):
        slot = slot + jnp.where(eidx == e, starts_ref[e], 0)
    slot_ref[...] = slot


def _slots(starts, eidx, rank, tn=2048):
    n = eidx.shape[1]
    tn = min(tn, n)
    spec = pl.BlockSpec((TOP_K, tn), lambda i, s: (0, i))
    return pl.pallas_call(
        _slots_kernel,
        grid_spec=pltpu.PrefetchScalarGridSpec(
            num_scalar_prefetch=1, grid=(n // tn,), in_specs=[spec, spec], out_specs=spec),
        out_shape=jax.ShapeDtypeStruct((TOP_K, n), I32),
        name="slots",
        compiler_params=_params("parallel"),
    )(starts, eidx, rank)


def _sc_mesh():
    return plsc.VectorSubcoreMesh(core_axis_name="c", subcore_axis_name="s")


def _sc_dispatch(parts, slots_flat, rows):
    n, width = parts[0].shape
    blocks = n // SC_WINDOW
    out_type = [jax.ShapeDtypeStruct((rows, width), part.dtype) for part in parts]

    @functools.partial(pl.kernel, out_type=out_type, mesh=_sc_mesh(), scratch_types=[], name="sc_dispatch")
    def run(*refs):
        i_hbm = refs[len(parts)]
        for x_hbm, o_hbm in zip(refs[:len(parts)], refs[len(parts) + 1:]):
            def body(x_vmem, i_vmem, o_hbm=o_hbm):
                pltpu.sync_copy(x_vmem, o_hbm.at[i_vmem.at[0]])

            pltpu.emit_pipeline(
                body,
                grid=(blocks, TOP_K),
                in_specs=[pl.BlockSpec((SC_WINDOW, width), lambda i, k: (i, 0)),
                          pl.BlockSpec((1, SC_WINDOW), lambda i, k: (0, k * blocks + i))],
                out_specs=[],
                core_axis_name=("c", "s"),
                dimension_semantics=(pltpu.PARALLEL, pltpu.ARBITRARY),
                trace_scopes=False,
            )(x_hbm, i_hbm)

    return run(*parts, slots_flat)


def _sc_gather(tables, idx_flat):
    count = idx_flat.shape[1]
    width = tables[0].shape[1]
    out_type = [jax.ShapeDtypeStruct((count, width), table.dtype) for table in tables]

    @functools.partial(pl.kernel, out_type=out_type, mesh=_sc_mesh(), scratch_types=[], name="sc_gather")
    def run(*refs):
        i_hbm = refs[len(tables)]
        for t_hbm, o_hbm in zip(refs[:len(tables)], refs[len(tables) + 1:]):
            def body(i_vmem, o_vmem, t_hbm=t_hbm):
                pltpu.sync_copy(t_hbm.at[i_vmem.at[0]], o_vmem)

            pltpu.emit_pipeline(
                body,
                grid=(count // SC_WINDOW,),
                in_specs=[pl.BlockSpec((1, SC_WINDOW), lambda i: (0, i))],
                out_specs=[pl.BlockSpec((SC_WINDOW, width), lambda i: (i, 0))],
                core_axis_name=("c", "s"),
                dimension_semantics=(pltpu.PARALLEL,),
                trace_scopes=False,
            )(i_hbm, o_hbm)

    return run(*tables, idx_flat)


def _expert_kernel(te_ref, used_ref, xa_ref, xb_ref, wg_ref, wu_ref, wd_ref, oa_ref, ob_ref):
    del te_ref
    @pl.when(pl.program_id(0) < used_ref[0])
    def _():
        xa = _unpack_pairs(xa_ref[...]).astype(BF16)
        hg = jnp.dot(xa, wg_ref[0, :HALF, :], preferred_element_type=F32)
        hu = jnp.dot(xa, wu_ref[0, :HALF, :], preferred_element_type=F32)
        xb = _unpack_pairs(xb_ref[...]).astype(BF16)
        hg = hg + jnp.dot(xb, wg_ref[0, HALF:, :], preferred_element_type=F32)
        hu = hu + jnp.dot(xb, wu_ref[0, HALF:, :], preferred_element_type=F32)
        hid = (_silu(hg) * hu).astype(BF16)
        oa_ref[...] = _pack_pairs(jnp.dot(hid, wd_ref[0, :, :HALF], preferred_element_type=F32))
        ob_ref[...] = _pack_pairs(jnp.dot(hid, wd_ref[0, :, HALF:], preferred_element_type=F32))


def _experts(tile_expert, tiles_used, xa, xb, wg, wu, wd):
    rows = xa.shape[0]
    row = lambda j, te, used: (j, 0)
    exp = lambda j, te, used: (te[j], 0, 0)
    return pl.pallas_call(
        _expert_kernel,
        grid_spec=pltpu.PrefetchScalarGridSpec(
            num_scalar_prefetch=2,
            grid=(rows // ROW_TILE,),
            in_specs=[
                pl.BlockSpec((ROW_TILE, PART), row),
                pl.BlockSpec((ROW_TILE, PART), row),
                pl.BlockSpec((1, D_MODEL, D_EXPERT), exp),
                pl.BlockSpec((1, D_MODEL, D_EXPERT), exp),
                pl.BlockSpec((1, D_EXPERT, D_MODEL), exp),
            ],
            out_specs=[pl.BlockSpec((ROW_TILE, PART), row), pl.BlockSpec((ROW_TILE, PART), row)],
        ),
        out_shape=[jax.ShapeDtypeStruct((rows, PART), I32), jax.ShapeDtypeStruct((rows, PART), I32)],
        name="experts",
        compiler_params=_params("parallel"),
    )(tile_expert, tiles_used, xa, xb, wg, wu, wd)


def _finish_kernel(x_ref, xb_ref, p_ref, ga_ref, gb_ref, wtok_ref, wgs_ref, wus_ref, wds_ref, wpg_ref, bpg_ref, wp_ref,
                   lng_ref, lnb_ref, xo_ref, xob_ref):
    xb = xb_ref[...]
    hs = _silu(jnp.dot(xb, wgs_ref[...], preferred_element_type=F32)) * jnp.dot(
        xb, wus_ref[...], preferred_element_type=F32)
    shared = jnp.dot(hs.astype(BF16), wds_ref[...], preferred_element_type=F32)
    gate = jax.nn.sigmoid(jnp.dot(xb, wpg_ref[...], preferred_element_type=F32) + bpg_ref[...])
    ple = gate * jnp.dot(p_ref[...].astype(BF16), wp_ref[...], preferred_element_type=F32)

    wtok = wtok_ref[...]
    r_a = None
    r_b = None
    for k in range(TOP_K):
        wk = wtok[:, k:k + 1]
        a = wk * _unpack_pairs(ga_ref[k])
        b = wk * _unpack_pairs(gb_ref[k])
        r_a = a if r_a is None else r_a + a
        r_b = b if r_b is None else r_b + b
    routed = jnp.concatenate([r_a, r_b], axis=1)

    xn = _layer_norm(ALPHA * x_ref[...] + (routed + shared + ple), lng_ref[...], lnb_ref[...])
    xo_ref[...] = xn
    xob_ref[...] = xn.astype(BF16)


def _finish(x, xb, p, ga, gb, wtok, wgs, wus, wds, wpg, bpg, wp, ln_g, ln_b, tm=256):
    n = x.shape[0]
    row = lambda i: (i, 0)
    const = lambda i: (0, 0)
    return pl.pallas_call(
        _finish_kernel,
        grid=(n // tm,),
        in_specs=[
            pl.BlockSpec((tm, D_MODEL), row),
            pl.BlockSpec((tm, D_MODEL), row),
            pl.BlockSpec((tm, P_DIM), row),
            pl.BlockSpec((TOP_K, tm, PART), lambda i: (0, i, 0)),
            pl.BlockSpec((TOP_K, tm, PART), lambda i: (0, i, 0)),
            pl.BlockSpec((tm, LANES), row),
            pl.BlockSpec((D_MODEL, D_SHARED), const),
            pl.BlockSpec((D_MODEL, D_SHARED), const),
            pl.BlockSpec((D_SHARED, D_MODEL), const),
            pl.BlockSpec((D_MODEL, D_MODEL), const),
            pl.BlockSpec((1, D_MODEL), const),
            pl.BlockSpec((P_DIM, D_MODEL), const),
            pl.BlockSpec((1, D_MODEL), const),
            pl.BlockSpec((1, D_MODEL), const),
        ],
        out_specs=[pl.BlockSpec((tm, D_MODEL), row), pl.BlockSpec((tm, D_MODEL), row)],
        out_shape=[jax.ShapeDtypeStruct((n, D_MODEL), F32), jax.ShapeDtypeStruct((n, D_MODEL), BF16)],
        name="moe_finish_ln",
        compiler_params=_params("parallel"),
    )(x, xb, p, ga, gb, wtok, wgs, wus, wds, wpg, bpg, wp, ln_g, ln_b)


def _moe(x, xb, xpa, xpb, p, w_router_t, b_router, wg, wu, wd, wgs, wus, wds, wpg, bpg, wp, ln_g, ln_b):
    n = x.shape[0]
    pairs = n * TOP_K
    tiles = pairs // ROW_TILE + N_EXPERTS
    eidx, rank, wtok, counts = _router(x, w_router_t, b_router)

    cnt = counts[:, 0].astype(I32)
    group_tiles = (cnt + ROW_TILE - 1) // ROW_TILE
    tile_end = jnp.cumsum(group_tiles)
    starts = (tile_end - group_tiles) * ROW_TILE
    tile_id = jnp.arange(tiles, dtype=I32)
    tile_expert = jnp.minimum(jnp.sum((tile_end[None, :] <= tile_id[:, None]).astype(I32), axis=1), N_EXPERTS - 1)

    slots_flat = _slots(starts, eidx, rank).reshape(1, pairs)
    xsa, xsb = _sc_dispatch([xpa, xpb], slots_flat, tiles * ROW_TILE)
    ysa, ysb = _experts(tile_expert, tile_end[N_EXPERTS - 1:], xsa, xsb, wg, wu, wd)
    ga, gb = _sc_gather([ysa, ysb], slots_flat)
    return _finish(x, xb, p, ga.reshape(TOP_K, n, PART), gb.reshape(TOP_K, n, PART), wtok, wgs, wus, wds, wpg, bpg, wp, ln_g, ln_b)


def kernel(x, p, w_in, b_in, sg_ln_g, sg_ln_b, w_s, b_s, mh_g, w_pa, w_pb, w_out, ln1_g, ln1_b, w_router, b_router, w_gate_e, w_up_e, w_down_e, w_gate_s, w_up_s, w_down_s, w_pg, b_pg, w_p, ln2_g, ln2_b):
    batch, seq, _ = x.shape
    streams = STREAMS if batch % STREAMS == 0 else 1
    sb = batch // streams
    n = sb * seq
    xf = [x[s * sb:(s + 1) * sb].reshape(n, D_MODEL) for s in range(streams)]
    xb = [xs.astype(BF16) for xs in xf]
    pf = [p[:, s * sb:(s + 1) * sb].reshape(DEPTH, n, P_DIM) for s in range(streams)]

    c_uv = 2 * G_WIDTH
    c_qkv = c_uv + 2 * M_QK + M_V
    c_o = c_qkv + M_V
    c_if = c_o + 2 * M_HEADS

    for l in range(DEPTH):
        w = w_in[l]
        b = b_in[l][None, :]
        sgu_w = (w[:, :G_WIDTH].astype(BF16), b[:, :G_WIDTH], w[:, G_WIDTH:c_uv].astype(BF16), b[:, G_WIDTH:c_uv],
                 sg_ln_g[l][None, :], sg_ln_b[l][None, :], w_s[l], b_s[l].T)
        w_qkv, b_qkv = w[:, c_uv:c_qkv].astype(BF16), b[:, c_uv:c_qkv]
        w_o, b_o = w[:, c_qkv:c_o].astype(BF16), b[:, c_qkv:c_o]
        w_if = jnp.pad(w[:, c_o:c_if], ((0, 0), (0, LANES - 2 * M_HEADS))).astype(BF16)
        b_if = jnp.pad(b[:, c_o:c_if], ((0, 0), (0, LANES - 2 * M_HEADS)))
        c_gb = c_if + D_MODEL
        mix_w = (w[:, c_if:c_gb].astype(BF16), b[:, c_if:c_gb], w[:, c_gb:].astype(BF16), b[:, c_gb:],
                 w_pa[l].astype(BF16), w_pb[l].astype(BF16), w_out[l].astype(BF16),
                 ln1_g[l][None, :], ln1_b[l][None, :])
        moe_w = (w_router[l].T, b_router[l][:, None],
                 w_gate_e[l].astype(BF16), w_up_e[l].astype(BF16), w_down_e[l].astype(BF16),
                 w_gate_s[l].astype(BF16), w_up_s[l].astype(BF16), w_down_s[l].astype(BF16),
                 w_pg[l].astype(BF16), b_pg[l][None, :], w_p[l].astype(BF16),
                 ln2_g[l][None, :], ln2_b[l][None, :])

        packed = []
        for s in range(streams):
            qkv = _proj(xb[s], w_qkv, b_qkv, None, BF16)
            osig = _proj(xb[s], w_o, b_o, "sigmoid", BF16)
            gif = _proj(xb[s], w_if, b_if, None, F32)
            gif_rows = gif[:, :2 * M_HEADS].reshape(sb, seq, 2 * M_HEADS).transpose(0, 2, 1)
            a = _sgu(xb[s], *sgu_w)
            mb = _mlstm(qkv, gif, gif_rows, osig, mh_g[l][None, :], sb, seq)
            xf[s], xb[s], xpa, xpb = _merge(xf[s], xb[s], a, mb, *mix_w)
            packed.append((xpa, xpb))
        for s in range(streams):
            xf[s], xb[s] = _moe(xf[s], xb[s], *packed[s], pf[s][l], *moe_w)
    return jnp.concatenate([xs.reshape(sb, seq, D_MODEL) for xs in xf], axis=0)
```

```python
import functools

import jax
import jax.numpy as jnp
from jax import lax
from jax.experimental import pallas as pl
from jax.experimental.pallas import tpu as pltpu
from jax.experimental.pallas import tpu_sc as plsc

D_MODEL = 1024
DEPTH = 4
CHUNK = 64
P_DIM = 256
G_WIDTH = 1024
G_GROUPS = 8
G_GROUP_DIM = G_WIDTH // G_GROUPS
G_BLOCK = 128
M_HEADS = 4
M_QK_DIM = 128
M_V_DIM = 256
M_QK = M_HEADS * M_QK_DIM
M_V = M_HEADS * M_V_DIM
N_EXPERTS = 64
TOP_K = 8
N_GROUPS = 8
TOPK_GROUPS = 4
GROUP_SIZE = N_EXPERTS // N_GROUPS
D_EXPERT = 256
D_SHARED = 256
ROUTE_SCALE = 2.5
ALPHA = (2 * DEPTH) ** 0.25
LN_EPS = 1e-5
RMS_EPS = 1e-6

LANES = 128
VMEM_LIMIT = 56 * 1024 * 1024
HALF = D_MODEL // 2
PART = HALF // 2
ROW_TILE = 512
STREAMS = 2
SC_WINDOW = 128

F32 = jnp.float32
BF16 = jnp.bfloat16
I32 = jnp.int32


def _params(*semantics):
    return pltpu.CompilerParams(dimension_semantics=semantics, vmem_limit_bytes=VMEM_LIMIT)


def _layer_norm(x, g, b):
    mu = jnp.mean(x, axis=-1, keepdims=True)
    xc = x - mu
    var = jnp.mean(xc * xc, axis=-1, keepdims=True)
    return xc * lax.rsqrt(var + LN_EPS) * g + b


def _gelu(x):
    return 0.5 * x * (1.0 + lax.erf(x * (2.0 ** -0.5)))


def _silu(x):
    return x * jax.nn.sigmoid(x)


def _pack_pairs(x):
    lo = lax.bitcast_convert_type(x[:, :PART].astype(BF16).astype(F32), I32)
    hi = lax.bitcast_convert_type(x[:, PART:].astype(BF16).astype(F32), I32)
    return lax.shift_right_logical(lo, 16) | (hi & jnp.int32(-65536))


def _pack_row(x):
    return _pack_pairs(x[:, :HALF]), _pack_pairs(x[:, HALF:])


def _unpack_pairs(w):
    lo = lax.bitcast_convert_type(lax.shift_left(w, 16), F32)
    hi = lax.bitcast_convert_type(w & jnp.int32(-65536), F32)
    return jnp.concatenate([lo, hi], axis=1)


def _mproj_kernel(x_ref, wqkv_ref, bqkv_ref, wo_ref, bo_ref, wif_ref, bif_ref, qkv_ref, osig_ref, gif_ref):
    x = x_ref[...]
    qkv_ref[...] = (jnp.dot(x, wqkv_ref[...], preferred_element_type=F32) + bqkv_ref[...]).astype(qkv_ref.dtype)
    osig_ref[...] = jax.nn.sigmoid(
        jnp.dot(x, wo_ref[...], preferred_element_type=F32) + bo_ref[...]).astype(osig_ref.dtype)
    gif_ref[...] = jnp.dot(x, wif_ref[...], preferred_element_type=F32) + bif_ref[...]


def _mproj(x, w_qkv, b_qkv, w_o, b_o, w_if, b_if, tm=512):
    n, k = x.shape
    row = lambda i: (i, 0)
    const = lambda i: (0, 0)
    widths = (w_qkv.shape[1], w_o.shape[1], w_if.shape[1])
    in_specs = [pl.BlockSpec((tm, k), row)]
    for width in widths:
        in_specs += [pl.BlockSpec((k, width), const), pl.BlockSpec((1, width), const)]
    return pl.pallas_call(
        _mproj_kernel,
        grid=(n // tm,),
        in_specs=in_specs,
        out_specs=[pl.BlockSpec((tm, width), row) for width in widths],
        out_shape=[jax.ShapeDtypeStruct((n, widths[0]), BF16), jax.ShapeDtypeStruct((n, widths[1]), BF16),
                   jax.ShapeDtypeStruct((n, widths[2]), F32)],
        name="mlstm_proj",
        compiler_params=_params("parallel"),
    )(x, w_qkv, b_qkv, w_o, b_o, w_if, b_if)


def _sgu_kernel(x_ref, wu_ref, bu_ref, wv_ref, bv_ref, lng_ref, lnb_ref, ws_ref, bs_ref, a_ref, *, blocks):
    x = x_ref[...]
    v = _gelu(jnp.dot(x, wv_ref[...], preferred_element_type=F32) + bv_ref[...])
    vln = _layer_norm(v, lng_ref[...], lnb_ref[...]).astype(BF16)
    u = _gelu(jnp.dot(x, wu_ref[...], preferred_element_type=F32) + bu_ref[...])
    t_chunk = lax.broadcasted_iota(jnp.int32, (G_BLOCK, G_BLOCK), 0) // CHUNK
    s_chunk = lax.broadcasted_iota(jnp.int32, (G_BLOCK, G_BLOCK), 1) // CHUNK
    causal = s_chunk <= t_chunk
    for g in range(G_GROUPS):
        w = jnp.where(causal, ws_ref[g], 0.0).astype(BF16)
        bias = bs_ref[:, g:g + 1]
        cs = slice(g * G_GROUP_DIM, (g + 1) * G_GROUP_DIM)
        for blk in range(blocks):
            rs = slice(blk * G_BLOCK, (blk + 1) * G_BLOCK)
            mixed = jnp.dot(w, vln[rs, cs], preferred_element_type=F32) + bias
            a_ref[rs, cs] = (u[rs, cs] * mixed).astype(a_ref.dtype)


def _sgu(x, w_u, b_u, w_v, b_v, ln_g, ln_b, w_s, b_s_t, blocks=4):
    n = x.shape[0]
    tp = blocks * G_BLOCK
    const = lambda i: (0, 0)
    wspec = pl.BlockSpec((D_MODEL, G_WIDTH), const)
    vspec = pl.BlockSpec((1, G_WIDTH), const)
    return pl.pallas_call(
        functools.partial(_sgu_kernel, blocks=blocks),
        grid=(n // tp,),
        in_specs=[
            pl.BlockSpec((tp, D_MODEL), lambda i: (i, 0)),
            wspec, vspec, wspec, vspec, vspec, vspec,
            pl.BlockSpec((G_GROUPS, G_BLOCK, G_BLOCK), lambda i: (0, 0, 0)),
            pl.BlockSpec((G_BLOCK, G_GROUPS), const),
        ],
        out_specs=pl.BlockSpec((tp, G_WIDTH), lambda i: (i, 0)),
        out_shape=jax.ShapeDtypeStruct((n, G_WIDTH), BF16),
        name="spatial_gating",
        compiler_params=_params("parallel"),
    )(x, w_u, b_u, w_v, b_v, ln_g, ln_b, w_s, b_s_t)


def _segment_cumsum(x, axis, seg):
    pos = lax.broadcasted_iota(jnp.int32, x.shape, axis) % seg
    shift = 1
    while shift < seg:
        x = x + jnp.where(pos >= shift, pltpu.roll(x, shift, axis), 0.0)
        shift *= 2
    return x


def _mlstm_kernel(qkv_ref, gc_ref, gr_ref, osig_ref, mhg_ref, mb_ref, c_ref, n_ref, m_ref, *, chunks):
    @pl.when(pl.program_id(1) == 0)
    def _():
        c_ref[...] = jnp.zeros_like(c_ref)
        n_ref[...] = jnp.zeros_like(n_ref)
        m_ref[...] = jnp.zeros_like(m_ref)

    scale = M_QK_DIM ** -0.5
    gc = gc_ref[...]
    gr = gr_ref[0]
    b_cols = _segment_cumsum(jax.nn.log_sigmoid(gc), 0, CHUNK)
    b_rows = _segment_cumsum(jax.nn.log_sigmoid(gr), 1, CHUNK)
    t_idx = lax.broadcasted_iota(jnp.int32, (CHUNK, CHUNK), 0)
    s_idx = lax.broadcasted_iota(jnp.int32, (CHUNK, CHUNK), 1)
    tri = s_idx <= t_idx

    heads = range(M_HEADS)
    units = [(c, h) for c in range(chunks) for h in heads]
    rows = lambda c: slice(c * CHUNK, (c + 1) * CHUNK)
    q = {u: qkv_ref[rows(u[0]), u[1] * M_QK_DIM:(u[1] + 1) * M_QK_DIM] for u in units}
    k = {u: qkv_ref[rows(u[0]), M_QK + u[1] * M_QK_DIM:M_QK + (u[1] + 1) * M_QK_DIM] for u in units}
    v = {u: qkv_ref[rows(u[0]), 2 * M_QK + u[1] * M_V_DIM:2 * M_QK + (u[1] + 1) * M_V_DIM] for u in units}
    i_col = {u: gc[rows(u[0]), u[1]:u[1] + 1] for u in units}
    b_col = {u: b_cols[rows(u[0]), M_HEADS + u[1]:M_HEADS + u[1] + 1] for u in units}
    ib_row = {u: (b_rows[M_HEADS + u[1]:M_HEADS + u[1] + 1, rows(u[0])]
                  - gr[u[1]:u[1] + 1, rows(u[0])]) for u in units}
    d = {u: jnp.where(tri, b_col[u] - ib_row[u], -jnp.inf) for u in units}
    d_max = {u: jnp.max(d[u], axis=1, keepdims=True) for u in units}
    qk = {u: lax.dot_general(q[u], k[u], (((1,), (1,)), ((), ())), preferred_element_type=F32) for u in units}
    s_loc = {u: qk[u] * scale * jnp.exp(d[u] - d_max[u]) for u in units}
    a_loc = {u: jnp.dot(s_loc[u].astype(BF16), v[u], preferred_element_type=F32) for u in units}
    r_loc = {u: jnp.sum(s_loc[u], axis=1, keepdims=True) for u in units}
    b_last = {u: b_col[u][CHUNK - 1:CHUNK, :] for u in units}
    g_col = {u: b_last[u] - b_col[u] + i_col[u] for u in units}
    g_max = {u: jnp.max(g_col[u], axis=0, keepdims=True) for u in units}
    wk = {u: jnp.exp(g_col[u] - g_max[u]) * k[u].astype(F32) for u in units}
    u_loc = {u: jnp.dot(wk[u].T.astype(BF16), v[u], preferred_element_type=F32) for u in units}
    nk_loc = {u: jnp.sum(wk[u], axis=0, keepdims=True) for u in units}

    c_state = [c_ref[h] for h in heads]
    n_state = [n_ref[h] for h in heads]
    m_state = [m_ref[h][:, 0:1] for h in heads]
    for c in range(chunks):
        us = [(c, h) for h in heads]
        inter = [b_col[u] + m_state[u[1]] for u in us]
        m_t = [jnp.maximum(inter[h], d_max[(c, h)]) for h in heads]
        w_intra = [jnp.exp(d_max[(c, h)] - m_t[h]) for h in heads]
        w_inter = [jnp.exp(inter[h] - m_t[h]) for h in heads]
        qc = [jnp.dot(q[(c, h)], c_state[h].astype(BF16), preferred_element_type=F32) * scale for h in heads]
        qn = [jnp.sum(q[(c, h)].astype(F32) * n_state[h], axis=1, keepdims=True) * scale for h in heads]
        m_new = [jnp.maximum(b_last[(c, h)] + m_state[h], g_max[(c, h)]) for h in heads]
        decay = [jnp.exp(b_last[(c, h)] + m_state[h] - m_new[h]) for h in heads]
        beta = [jnp.exp(g_max[(c, h)] - m_new[h]) for h in heads]
        c_state = [decay[h] * c_state[h] + beta[h] * u_loc[(c, h)] for h in heads]
        n_state = [decay[h] * n_state[h] + beta[h] * nk_loc[(c, h)] for h in heads]
        m_state = m_new
        num = [w_intra[h] * a_loc[(c, h)] + w_inter[h] * qc[h] for h in heads]
        den = [w_intra[h] * r_loc[(c, h)] + w_inter[h] * qn[h] for h in heads]
        hv = [num[h] * (1.0 / jnp.maximum(jnp.abs(den[h]), jnp.exp(-m_t[h]))) for h in heads]
        hv = [hv[h] * lax.rsqrt(jnp.mean(hv[h] * hv[h], axis=1, keepdims=True) + RMS_EPS) for h in heads]
        for h in heads:
            vs = slice(h * M_V_DIM, (h + 1) * M_V_DIM)
            mb_ref[rows(c), vs] = (osig_ref[rows(c), vs].astype(F32) * (hv[h] * mhg_ref[:, vs])).astype(mb_ref.dtype)

    for h in heads:
        c_ref[h], n_ref[h] = c_state[h], n_state[h]
        m_ref[h] = jnp.broadcast_to(m_state[h], (1, LANES))


def _mlstm(qkv, gates_col, gates_row, osig_src, mh_g, batch, seq, chunks=8):
    ts = chunks * CHUNK
    tiles = seq // ts
    n = batch * seq
    return pl.pallas_call(
        functools.partial(_mlstm_kernel, chunks=chunks),
        grid=(batch, tiles),
        in_specs=[
            pl.BlockSpec((ts, 2 * M_QK + M_V), lambda b, t: (b * tiles + t, 0)),
            pl.BlockSpec((ts, LANES), lambda b, t: (b * tiles + t, 0)),
            pl.BlockSpec((1, 2 * M_HEADS, ts), lambda b, t: (b, 0, t)),
            pl.BlockSpec((ts, M_V), lambda b, t: (b * tiles + t, 0)),
            pl.BlockSpec((1, M_V), lambda b, t: (0, 0)),
        ],
        out_specs=pl.BlockSpec((ts, M_V), lambda b, t: (b * tiles + t, 0)),
        out_shape=jax.ShapeDtypeStruct((n, M_V), BF16),
        scratch_shapes=[
            pltpu.VMEM((M_HEADS, M_QK_DIM, M_V_DIM), F32),
            pltpu.VMEM((M_HEADS, 1, M_QK_DIM), F32),
            pltpu.VMEM((M_HEADS, 1, LANES), F32),
        ],
        name="mlstm",
        compiler_params=_params("parallel", "arbitrary"),
    )(qkv, gates_col, gates_row, osig_src, mh_g)


def _merge_kernel(x_ref, xin_ref, a_ref, mb_ref, wga_ref, bga_ref, wgb_ref, bgb_ref, wpa_ref, wpb_ref, wout_ref,
                  g_ref, b_ref, xo_ref, xob_ref, xpa_ref, xpb_ref):
    xin = xin_ref[...]
    ga = jax.nn.sigmoid(jnp.dot(xin, wga_ref[...], preferred_element_type=F32) + bga_ref[...])
    gb = jax.nn.sigmoid(jnp.dot(xin, wgb_ref[...], preferred_element_type=F32) + bgb_ref[...])
    ya = jnp.dot(a_ref[...], wpa_ref[...], preferred_element_type=F32)
    yb = jnp.dot(mb_ref[...], wpb_ref[...], preferred_element_type=F32)
    y = ga * ya + gb * yb
    mix = jnp.dot(y.astype(BF16), wout_ref[...], preferred_element_type=F32)
    xn = _layer_norm(ALPHA * x_ref[...] + mix, g_ref[...], b_ref[...])
    xo_ref[...] = xn
    xob_ref[...] = xn.astype(BF16)
    xpa_ref[...], xpb_ref[...] = _pack_row(xn)


def _merge(x, xin, a, mb, w_ga, b_ga, w_gb, b_gb, w_pa, w_pb, w_out, ln_g, ln_b, tm=512):
    n = x.shape[0]
    row = lambda i: (i, 0)
    const = lambda i: (0, 0)
    wspec = pl.BlockSpec((D_MODEL, D_MODEL), const)
    vspec = pl.BlockSpec((1, D_MODEL), const)
    return pl.pallas_call(
        _merge_kernel,
        grid=(n // tm,),
        in_specs=[
            pl.BlockSpec((tm, D_MODEL), row),
            pl.BlockSpec((tm, D_MODEL), row),
            pl.BlockSpec((tm, G_WIDTH), row),
            pl.BlockSpec((tm, M_V), row),
            wspec, vspec, wspec, vspec, wspec, wspec, wspec, vspec, vspec,
        ],
        out_specs=[pl.BlockSpec((tm, D_MODEL), row), pl.BlockSpec((tm, D_MODEL), row),
                   pl.BlockSpec((tm, PART), row), pl.BlockSpec((tm, PART), row)],
        out_shape=[jax.ShapeDtypeStruct((n, D_MODEL), F32), jax.ShapeDtypeStruct((n, D_MODEL), BF16),
                   jax.ShapeDtypeStruct((n, PART), I32), jax.ShapeDtypeStruct((n, PART), I32)],
        name="merge_ln",
        compiler_params=_params("parallel"),
    )(x, xin, a, mb, w_ga, b_ga, w_gb, b_gb, w_pa, w_pb, w_out, ln_g, ln_b)


def _first_max(v, idx, axes, sentinel):
    m = jnp.max(v, axis=axes, keepdims=True)
    first = jnp.min(jnp.where(v == m, idx, sentinel), axis=axes, keepdims=True)
    return m, first


def _router_kernel(x_ref, wr_ref, br_ref, upper_ref, eidx_ref, rank_ref, wtok_ref, cnt_ref, run_ref, *, tm):
    @pl.when(pl.program_id(0) == 0)
    def _():
        run_ref[...] = jnp.zeros_like(run_ref)

    def split(v):
        hi = v.astype(BF16)
        return hi, (v - hi.astype(F32)).astype(BF16)

    nt = lambda a, b: lax.dot_general(a, b, (((1,), (1,)), ((), ())), preferred_element_type=F32)
    w_hi, w_lo = split(wr_ref[...])
    x_hi, x_lo = split(x_ref[...])
    logits = nt(w_hi, x_hi) + (nt(w_hi, x_lo) + nt(w_lo, x_hi))
    scores = jax.nn.sigmoid(logits)
    sel = (scores + br_ref[...]).reshape(N_GROUPS, GROUP_SIZE, tm)
    scores3 = scores.reshape(N_GROUPS, GROUP_SIZE, tm)
    member = lax.broadcasted_iota(jnp.int32, sel.shape, 1)
    group = lax.broadcasted_iota(jnp.int32, sel.shape, 0)
    neg = -jnp.inf

    m1, f1 = _first_max(sel, member, 1, GROUP_SIZE)
    m2 = jnp.max(jnp.where(member == f1, neg, sel), axis=1, keepdims=True)
    gscore = m1 + m2
    gid = lax.broadcasted_iota(jnp.int32, gscore.shape, 0)
    gmask = jnp.zeros(gscore.shape, dtype=jnp.bool_)
    for _ in range(TOPK_GROUPS):
        _, fg = _first_max(gscore, gid, 0, N_GROUPS)
        hit = gid == fg
        gmask = gmask | hit
        gscore = jnp.where(hit, neg, gscore)

    cand = jnp.where(gmask, sel, neg)
    eid = group * GROUP_SIZE + member
    chosen = jnp.zeros(sel.shape, dtype=jnp.bool_)
    picks = []
    for _ in range(TOP_K):
        _, fe = _first_max(cand, eid, (0, 1), N_EXPERTS)
        hit = eid == fe
        chosen = chosen | hit
        cand = jnp.where(hit, neg, cand)
        picks.append((fe[0], hit, jnp.sum(jnp.where(hit, scores3, 0.0), axis=(0, 1), keepdims=True)[0]))

    chosen2 = jnp.where(chosen, 1.0, 0.0).reshape(N_EXPERTS, tm)
    prefix = jnp.dot(chosen2.astype(BF16), upper_ref[...], preferred_element_type=F32) + run_ref[:, 0:1]
    prefix3 = prefix.reshape(N_GROUPS, GROUP_SIZE, tm)

    total = picks[0][2]
    for _, _, wk in picks[1:]:
        total = total + wk
    eidx_ref[...] = jnp.concatenate([fe for fe, _, _ in picks], axis=0)
    rank_ref[...] = jnp.concatenate(
        [jnp.sum(jnp.where(hit, prefix3, 0.0), axis=(0, 1), keepdims=True)[0] for _, hit, _ in picks],
        axis=0).astype(I32)
    w_rows = jnp.concatenate([wk / total * ROUTE_SCALE for _, _, wk in picks]
                             + [jnp.zeros((LANES - TOP_K, tm), F32)], axis=0)
    wtok_ref[...] = w_rows.T

    run = run_ref[...] + jnp.sum(chosen2, axis=1, keepdims=True)
    run_ref[...] = run
    cnt_ref[...] = run


def _router(x, wr_t, br, tm=512):
    n = x.shape[0]
    tok = lambda i: (0, i)
    upper = (jnp.arange(tm)[:, None] < jnp.arange(tm)[None, :]).astype(BF16)
    return pl.pallas_call(
        functools.partial(_router_kernel, tm=tm),
        grid=(n // tm,),
        in_specs=[
            pl.BlockSpec((tm, D_MODEL), lambda i: (i, 0)),
            pl.BlockSpec((N_EXPERTS, D_MODEL), lambda i: (0, 0)),
            pl.BlockSpec((N_EXPERTS, 1), lambda i: (0, 0)),
            pl.BlockSpec((tm, tm), lambda i: (0, 0)),
        ],
        out_specs=[
            pl.BlockSpec((TOP_K, tm), tok),
            pl.BlockSpec((TOP_K, tm), tok),
            pl.BlockSpec((tm, LANES), lambda i: (i, 0)),
            pl.BlockSpec((N_EXPERTS, LANES), lambda i: (0, 0)),
        ],
        out_shape=[
            jax.ShapeDtypeStruct((TOP_K, n), I32),
            jax.ShapeDtypeStruct((TOP_K, n), I32),
            jax.ShapeDtypeStruct((n, LANES), F32),
            jax.ShapeDtypeStruct((N_EXPERTS, LANES), F32),
        ],
        scratch_shapes=[pltpu.VMEM((N_EXPERTS, LANES), F32)],
        name="router",
        compiler_params=_params("arbitrary"),
    )(x, wr_t, br, upper)


def _slots_kernel(starts_ref, eidx_ref, rank_ref, slot_ref):
    eidx = eidx_ref[...]
    slot = rank_ref[...]
    for e in range(N_EXPERTS):
        slot = slot + jnp.where(eidx == e, starts_ref[e], 0)
    slot_ref[...] = slot


def _slots(starts, eidx, rank, tn=2048):
    n = eidx.shape[1]
    tn = min(tn, n)
    spec = pl.BlockSpec((TOP_K, tn), lambda i, s: (0, i))
    return pl.pallas_call(
        _slots_kernel,
        grid_spec=pltpu.PrefetchScalarGridSpec(
            num_scalar_prefetch=1, grid=(n // tn,), in_specs=[spec, spec], out_specs=spec),
        out_shape=jax.ShapeDtypeStruct((TOP_K, n), I32),
        name="slots",
        compiler_params=_params("parallel"),
    )(starts, eidx, rank)


def _sc_mesh():
    return plsc.VectorSubcoreMesh(core_axis_name="c", subcore_axis_name="s")


def _sc_dispatch(parts, slots_flat, rows):
    n, width = parts[0].shape
    blocks = n // SC_WINDOW
    out_type = [jax.ShapeDtypeStruct((rows, width), part.dtype) for part in parts]

    @functools.partial(pl.kernel, out_type=out_type, mesh=_sc_mesh(), scratch_types=[], name="sc_dispatch")
    def run(*refs):
        i_hbm = refs[len(parts)]
        for x_hbm, o_hbm in zip(refs[:len(parts)], refs[len(parts) + 1:]):
            def body(x_vmem, i_vmem, o_hbm=o_hbm):
                pltpu.sync_copy(x_vmem, o_hbm.at[i_vmem.at[0]])

            pltpu.emit_pipeline(
                body,
                grid=(blocks, TOP_K),
                in_specs=[pl.BlockSpec((SC_WINDOW, width), lambda i, k: (i, 0)),
                          pl.BlockSpec((1, SC_WINDOW), lambda i, k: (0, k * blocks + i))],
                out_specs=[],
                core_axis_name=("c", "s"),
                dimension_semantics=(pltpu.PARALLEL, pltpu.ARBITRARY),
                trace_scopes=False,
            )(x_hbm, i_hbm)

    return run(*parts, slots_flat)


def _sc_gather(tables, idx_flat):
    count = idx_flat.shape[1]
    width = tables[0].shape[1]
    out_type = [jax.ShapeDtypeStruct((count, width), table.dtype) for table in tables]

    @functools.partial(pl.kernel, out_type=out_type, mesh=_sc_mesh(), scratch_types=[], name="sc_gather")
    def run(*refs):
        i_hbm = refs[len(tables)]
        for t_hbm, o_hbm in zip(refs[:len(tables)], refs[len(tables) + 1:]):
            def body(i_vmem, o_vmem, t_hbm=t_hbm):
                pltpu.sync_copy(t_hbm.at[i_vmem.at[0]], o_vmem)

            pltpu.emit_pipeline(
                body,
                grid=(count // SC_WINDOW,),
                in_specs=[pl.BlockSpec((1, SC_WINDOW), lambda i: (0, i))],
                out_specs=[pl.BlockSpec((SC_WINDOW, width), lambda i: (i, 0))],
                core_axis_name=("c", "s"),
                dimension_semantics=(pltpu.PARALLEL,),
                trace_scopes=False,
            )(i_hbm, o_hbm)

    return run(*tables, idx_flat)


def _expert_kernel(te_ref, used_ref, xa_ref, xb_ref, wg_ref, wu_ref, wd_ref, oa_ref, ob_ref, wgb_ref, wub_ref, wdb_ref):
    j = pl.program_id(0)

    @pl.when((j == 0) | (te_ref[j] != te_ref[jnp.maximum(j - 1, 0)]))
    def _():
        wgb_ref[...] = wg_ref[0].astype(BF16)
        wub_ref[...] = wu_ref[0].astype(BF16)
        wdb_ref[...] = wd_ref[0].astype(BF16)

    @pl.when(j < used_ref[0])
    def _():
        x = jnp.concatenate([_unpack_pairs(xa_ref[...]), _unpack_pairs(xb_ref[...])], axis=1).astype(BF16)
        hg = jnp.dot(x, wgb_ref[...], preferred_element_type=F32)
        hu = jnp.dot(x, wub_ref[...], preferred_element_type=F32)
        hid = (_silu(hg) * hu).astype(BF16)
        oa_ref[...], ob_ref[...] = _pack_row(jnp.dot(hid, wdb_ref[...], preferred_element_type=F32))


def _experts(tile_expert, tiles_used, xa, xb, layer, wg, wu, wd):
    rows = xa.shape[0]
    row = lambda j, te, used: (j, 0)
    exp = lambda j, te, used: (layer * N_EXPERTS + te[j], 0, 0)
    return pl.pallas_call(
        _expert_kernel,
        grid_spec=pltpu.PrefetchScalarGridSpec(
            num_scalar_prefetch=2,
            grid=(rows // ROW_TILE,),
            in_specs=[
                pl.BlockSpec((ROW_TILE, PART), row),
                pl.BlockSpec((ROW_TILE, PART), row),
                pl.BlockSpec((1, D_MODEL, D_EXPERT), exp),
                pl.BlockSpec((1, D_MODEL, D_EXPERT), exp),
                pl.BlockSpec((1, D_EXPERT, D_MODEL), exp),
            ],
            out_specs=[pl.BlockSpec((ROW_TILE, PART), row), pl.BlockSpec((ROW_TILE, PART), row)],
            scratch_shapes=[pltpu.VMEM((D_MODEL, D_EXPERT), BF16), pltpu.VMEM((D_MODEL, D_EXPERT), BF16),
                            pltpu.VMEM((D_EXPERT, D_MODEL), BF16)],
        ),
        out_shape=[jax.ShapeDtypeStruct((rows, PART), I32), jax.ShapeDtypeStruct((rows, PART), I32)],
        name="experts",
        compiler_params=_params("arbitrary"),
    )(tile_expert, tiles_used, xa, xb, wg, wu, wd)


def _finish_kernel(x_ref, xb_ref, p_ref, ga_ref, gb_ref, wtok_ref, wgs_ref, wus_ref, wds_ref, wpg_ref, bpg_ref, wp_ref,
                   lng_ref, lnb_ref, xo_ref, xob_ref):
    xb = xb_ref[...]
    hs = _silu(jnp.dot(xb, wgs_ref[...], preferred_element_type=F32)) * jnp.dot(
        xb, wus_ref[...], preferred_element_type=F32)
    shared = jnp.dot(hs.astype(BF16), wds_ref[...], preferred_element_type=F32)
    gate = jax.nn.sigmoid(jnp.dot(xb, wpg_ref[...], preferred_element_type=F32) + bpg_ref[...])
    ple = gate * jnp.dot(p_ref[0].astype(BF16), wp_ref[...], preferred_element_type=F32)

    wtok = wtok_ref[...]
    r_a = None
    r_b = None
    for k in range(TOP_K):
        wk = wtok[:, k:k + 1]
        a = wk * _unpack_pairs(ga_ref[k])
        b = wk * _unpack_pairs(gb_ref[k])
        r_a = a if r_a is None else r_a + a
        r_b = b if r_b is None else r_b + b
    routed = jnp.concatenate([r_a, r_b], axis=1)

    xn = _layer_norm(ALPHA * x_ref[...] + (routed + shared + ple), lng_ref[...], lnb_ref[...])
    xo_ref[...] = xn
    xob_ref[...] = xn.astype(BF16)


def _finish(x, xb, p_all, layer, stream, ga, gb, wtok, wgs, wus, wds, wpg, bpg, wp, ln_g, ln_b, tm=256):
    n = x.shape[0]
    p_block = stream * (n // tm)
    row = lambda i: (i, 0)
    const = lambda i: (0, 0)
    return pl.pallas_call(
        _finish_kernel,
        grid=(n // tm,),
        in_specs=[
            pl.BlockSpec((tm, D_MODEL), row),
            pl.BlockSpec((tm, D_MODEL), row),
            pl.BlockSpec((1, tm, P_DIM), lambda i: (layer, p_block + i, 0)),
            pl.BlockSpec((TOP_K, tm, PART), lambda i: (0, i, 0)),
            pl.BlockSpec((TOP_K, tm, PART), lambda i: (0, i, 0)),
            pl.BlockSpec((tm, LANES), row),
            pl.BlockSpec((D_MODEL, D_SHARED), const),
            pl.BlockSpec((D_MODEL, D_SHARED), const),
            pl.BlockSpec((D_SHARED, D_MODEL), const),
            pl.BlockSpec((D_MODEL, D_MODEL), const),
            pl.BlockSpec((1, D_MODEL), const),
            pl.BlockSpec((P_DIM, D_MODEL), const),
            pl.BlockSpec((1, D_MODEL), const),
            pl.BlockSpec((1, D_MODEL), const),
        ],
        out_specs=[pl.BlockSpec((tm, D_MODEL), row), pl.BlockSpec((tm, D_MODEL), row)],
        out_shape=[jax.ShapeDtypeStruct((n, D_MODEL), F32), jax.ShapeDtypeStruct((n, D_MODEL), BF16)],
        name="moe_finish_ln",
        compiler_params=_params("parallel"),
    )(x, xb, p_all, ga, gb, wtok, wgs, wus, wds, wpg, bpg, wp, ln_g, ln_b)


def _moe(x, xb, xpa, xpb, p_all, stream, w_router_t, b_router, layer, wg, wu, wd, wgs, wus, wds, wpg, bpg, wp, ln_g, ln_b):
    n = x.shape[0]
    pairs = n * TOP_K
    tiles = pairs // ROW_TILE + N_EXPERTS
    eidx, rank, wtok, counts = _router(x, w_router_t, b_router)

    cnt = counts[:, 0].astype(I32)
    group_tiles = (cnt + ROW_TILE - 1) // ROW_TILE
    tile_end = jnp.cumsum(group_tiles)
    starts = (tile_end - group_tiles) * ROW_TILE
    tile_id = jnp.arange(tiles, dtype=I32)
    tile_expert = jnp.minimum(jnp.sum((tile_end[None, :] <= tile_id[:, None]).astype(I32), axis=1), N_EXPERTS - 1)

    slots_flat = _slots(starts, eidx, rank).reshape(1, pairs)
    xsa, xsb = _sc_dispatch([xpa, xpb], slots_flat, tiles * ROW_TILE)
    ysa, ysb = _experts(tile_expert, tile_end[N_EXPERTS - 1:], xsa, xsb, layer, wg, wu, wd)
    ga, gb = _sc_gather([ysa, ysb], slots_flat)
    return _finish(x, xb, p_all, layer, stream, ga.reshape(TOP_K, n, PART), gb.reshape(TOP_K, n, PART), wtok, wgs, wus, wds, wpg, bpg, wp, ln_g, ln_b)


def kernel(x, p, w_in, b_in, sg_ln_g, sg_ln_b, w_s, b_s, mh_g, w_pa, w_pb, w_out, ln1_g, ln1_b, w_router, b_router, w_gate_e, w_up_e, w_down_e, w_gate_s, w_up_s, w_down_s, w_pg, b_pg, w_p, ln2_g, ln2_b):
    batch, seq, _ = x.shape
    streams = STREAMS if batch % STREAMS == 0 else 1
    sb = batch // streams
    n = sb * seq
    xf = [x[s * sb:(s + 1) * sb].reshape(n, D_MODEL) for s in range(streams)]
    xb = [xs.astype(BF16) for xs in xf]
    p_all = p.reshape(DEPTH, batch * seq, P_DIM)

    wg_all = w_gate_e.reshape(DEPTH * N_EXPERTS, D_MODEL, D_EXPERT)
    wu_all = w_up_e.reshape(DEPTH * N_EXPERTS, D_MODEL, D_EXPERT)
    wd_all = w_down_e.reshape(DEPTH * N_EXPERTS, D_EXPERT, D_MODEL)

    c_uv = 2 * G_WIDTH
    c_qkv = c_uv + 2 * M_QK + M_V
    c_o = c_qkv + M_V
    c_if = c_o + 2 * M_HEADS

    for l in range(DEPTH):
        w = w_in[l]
        b = b_in[l][None, :]
        sgu_w = (w[:, :G_WIDTH].astype(BF16), b[:, :G_WIDTH], w[:, G_WIDTH:c_uv].astype(BF16), b[:, G_WIDTH:c_uv],
                 sg_ln_g[l][None, :], sg_ln_b[l][None, :], w_s[l], b_s[l].T)
        w_qkv, b_qkv = w[:, c_uv:c_qkv].astype(BF16), b[:, c_uv:c_qkv]
        w_o, b_o = w[:, c_qkv:c_o].astype(BF16), b[:, c_qkv:c_o]
        w_if = jnp.pad(w[:, c_o:c_if], ((0, 0), (0, LANES - 2 * M_HEADS))).astype(BF16)
        b_if = jnp.pad(b[:, c_o:c_if], ((0, 0), (0, LANES - 2 * M_HEADS)))
        c_gb = c_if + D_MODEL
        mix_w = (w[:, c_if:c_gb].astype(BF16), b[:, c_if:c_gb], w[:, c_gb:].astype(BF16), b[:, c_gb:],
                 w_pa[l].astype(BF16), w_pb[l].astype(BF16), w_out[l].astype(BF16),
                 ln1_g[l][None, :], ln1_b[l][None, :])
        moe_w = (w_router[l].T, b_router[l][:, None],
                 l, wg_all, wu_all, wd_all,
                 w_gate_s[l].astype(BF16), w_up_s[l].astype(BF16), w_down_s[l].astype(BF16),
                 w_pg[l].astype(BF16), b_pg[l][None, :], w_p[l].astype(BF16),
                 ln2_g[l][None, :], ln2_b[l][None, :])

        packed = []
        for s in range(streams):
            qkv, osig, gif = _mproj(xb[s], w_qkv, b_qkv, w_o, b_o, w_if, b_if)
            gif_rows = gif[:, :2 * M_HEADS].reshape(sb, seq, 2 * M_HEADS).transpose(0, 2, 1)
            a = _sgu(xb[s], *sgu_w)
            mb = _mlstm(qkv, gif, gif_rows, osig, mh_g[l][None, :], sb, seq)
            xf[s], xb[s], xpa, xpb = _merge(xf[s], xb[s], a, mb, *mix_w)
            packed.append((xpa, xpb))
        for s in range(streams):
            xf[s], xb[s] = _moe(xf[s], xb[s], *packed[s], p_all, s, *moe_w)
    return jnp.concatenate([xs.reshape(sb, seq, D_MODEL) for xs in xf], axis=0)
```

```python
import functools

import jax
import jax.numpy as jnp
from jax import lax
from jax.experimental import pallas as pl
from jax.experimental.pallas import tpu as pltpu
from jax.experimental.pallas import tpu_sc as plsc

D_MODEL = 1024
DEPTH = 4
CHUNK = 64
P_DIM = 256
G_WIDTH = 1024
G_GROUPS = 8
G_GROUP_DIM = G_WIDTH // G_GROUPS
G_BLOCK = 128
M_HEADS = 4
M_QK_DIM = 128
M_V_DIM = 256
M_QK = M_HEADS * M_QK_DIM
M_V = M_HEADS * M_V_DIM
N_EXPERTS = 64
TOP_K = 8
N_GROUPS = 8
TOPK_GROUPS = 4
GROUP_SIZE = N_EXPERTS // N_GROUPS
D_EXPERT = 256
D_SHARED = 256
ROUTE_SCALE = 2.5
ALPHA = (2 * DEPTH) ** 0.25
LN_EPS = 1e-5
RMS_EPS = 1e-6

LANES = 128
VMEM_LIMIT = 56 * 1024 * 1024
HALF = D_MODEL // 2
PART = HALF // 2
ROW_TILE = 512
STREAMS = 2
MLSTM_GROUP = 8
SC_WINDOW = 128

F32 = jnp.float32
BF16 = jnp.bfloat16
I32 = jnp.int32


def _params(*semantics):
    return pltpu.CompilerParams(dimension_semantics=semantics, vmem_limit_bytes=VMEM_LIMIT)


def _layer_norm(x, g, b):
    mu = jnp.mean(x, axis=-1, keepdims=True)
    xc = x - mu
    var = jnp.mean(xc * xc, axis=-1, keepdims=True)
    return xc * lax.rsqrt(var + LN_EPS) * g + b


def _gelu(x):
    return 0.5 * x * (1.0 + lax.erf(x * (2.0 ** -0.5)))


def _silu(x):
    return x * jax.nn.sigmoid(x)


def _pack_pairs(x):
    lo = lax.bitcast_convert_type(x[:, :PART].astype(BF16).astype(F32), I32)
    hi = lax.bitcast_convert_type(x[:, PART:].astype(BF16).astype(F32), I32)
    return lax.shift_right_logical(lo, 16) | (hi & jnp.int32(-65536))


def _pack_row(x):
    return _pack_pairs(x[:, :HALF]), _pack_pairs(x[:, HALF:])


def _unpack_pairs(w):
    lo = lax.bitcast_convert_type(lax.shift_left(w, 16), F32)
    hi = lax.bitcast_convert_type(w & jnp.int32(-65536), F32)
    return jnp.concatenate([lo, hi], axis=1)


def _mproj_kernel(x_ref, wqkv_ref, bqkv_ref, wo_ref, bo_ref, wif_ref, bif_ref, qkv_ref, osig_ref, gif_ref):
    x = x_ref[...]
    qkv_ref[...] = (jnp.dot(x, wqkv_ref[...], preferred_element_type=F32) + bqkv_ref[...]).astype(qkv_ref.dtype)
    osig_ref[...] = jax.nn.sigmoid(
        jnp.dot(x, wo_ref[...], preferred_element_type=F32) + bo_ref[...]).astype(osig_ref.dtype)
    gif_ref[...] = jnp.dot(x, wif_ref[...], preferred_element_type=F32) + bif_ref[...]


def _mproj(x, w_qkv, b_qkv, w_o, b_o, w_if, b_if, tm=512):
    n, k = x.shape
    row = lambda i: (i, 0)
    const = lambda i: (0, 0)
    widths = (w_qkv.shape[1], w_o.shape[1], w_if.shape[1])
    in_specs = [pl.BlockSpec((tm, k), row)]
    for width in widths:
        in_specs += [pl.BlockSpec((k, width), const), pl.BlockSpec((1, width), const)]
    return pl.pallas_call(
        _mproj_kernel,
        grid=(n // tm,),
        in_specs=in_specs,
        out_specs=[pl.BlockSpec((tm, width), row) for width in widths],
        out_shape=[jax.ShapeDtypeStruct((n, widths[0]), BF16), jax.ShapeDtypeStruct((n, widths[1]), BF16),
                   jax.ShapeDtypeStruct((n, widths[2]), F32)],
        name="mlstm_proj",
        compiler_params=_params("parallel"),
    )(x, w_qkv, b_qkv, w_o, b_o, w_if, b_if)


def _sgu_kernel(x_ref, wu_ref, bu_ref, wv_ref, bv_ref, lng_ref, lnb_ref, ws_ref, bs_ref, a_ref, *, blocks):
    x = x_ref[...]
    v = _gelu(jnp.dot(x, wv_ref[...], preferred_element_type=F32) + bv_ref[...])
    vln = _layer_norm(v, lng_ref[...], lnb_ref[...]).astype(BF16)
    u = _gelu(jnp.dot(x, wu_ref[...], preferred_element_type=F32) + bu_ref[...])
    t_chunk = lax.broadcasted_iota(jnp.int32, (G_BLOCK, G_BLOCK), 0) // CHUNK
    s_chunk = lax.broadcasted_iota(jnp.int32, (G_BLOCK, G_BLOCK), 1) // CHUNK
    causal = s_chunk <= t_chunk
    for g in range(G_GROUPS):
        w = jnp.where(causal, ws_ref[g], 0.0).astype(BF16)
        bias = bs_ref[:, g:g + 1]
        cs = slice(g * G_GROUP_DIM, (g + 1) * G_GROUP_DIM)
        for blk in range(blocks):
            rs = slice(blk * G_BLOCK, (blk + 1) * G_BLOCK)
            mixed = jnp.dot(w, vln[rs, cs], preferred_element_type=F32) + bias
            a_ref[rs, cs] = (u[rs, cs] * mixed).astype(a_ref.dtype)


def _sgu(x, w_u, b_u, w_v, b_v, ln_g, ln_b, w_s, b_s_t, blocks=4):
    n = x.shape[0]
    tp = blocks * G_BLOCK
    const = lambda i: (0, 0)
    wspec = pl.BlockSpec((D_MODEL, G_WIDTH), const)
    vspec = pl.BlockSpec((1, G_WIDTH), const)
    return pl.pallas_call(
        functools.partial(_sgu_kernel, blocks=blocks),
        grid=(n // tp,),
        in_specs=[
            pl.BlockSpec((tp, D_MODEL), lambda i: (i, 0)),
            wspec, vspec, wspec, vspec, vspec, vspec,
            pl.BlockSpec((G_GROUPS, G_BLOCK, G_BLOCK), lambda i: (0, 0, 0)),
            pl.BlockSpec((G_BLOCK, G_GROUPS), const),
        ],
        out_specs=pl.BlockSpec((tp, G_WIDTH), lambda i: (i, 0)),
        out_shape=jax.ShapeDtypeStruct((n, G_WIDTH), BF16),
        name="spatial_gating",
        compiler_params=_params("parallel"),
    )(x, w_u, b_u, w_v, b_v, ln_g, ln_b, w_s, b_s_t)


def _segment_cumsum(x, axis, seg):
    pos = lax.broadcasted_iota(jnp.int32, x.shape, axis) % seg
    shift = 1
    while shift < seg:
        x = x + jnp.where(pos >= shift, pltpu.roll(x, shift, axis), 0.0)
        shift *= 2
    return x


def _mlstm_kernel(qkv_ref, gc_ref, gr_ref, osig_ref, mhg_ref, mb_ref, c_ref, n_ref, m_ref, *, chunks):
    @pl.when(pl.program_id(1) == 0)
    def _():
        c_ref[...] = jnp.zeros_like(c_ref)
        n_ref[...] = jnp.zeros_like(n_ref)
        m_ref[...] = jnp.zeros_like(m_ref)

    scale = M_QK_DIM ** -0.5
    gc = gc_ref[...]
    gr = gr_ref[0]
    b_cols = _segment_cumsum(jax.nn.log_sigmoid(gc), 0, CHUNK)
    b_rows = _segment_cumsum(jax.nn.log_sigmoid(gr), 1, CHUNK)
    t_idx = lax.broadcasted_iota(jnp.int32, (CHUNK, CHUNK), 0)
    s_idx = lax.broadcasted_iota(jnp.int32, (CHUNK, CHUNK), 1)
    tri = s_idx <= t_idx

    heads = range(M_HEADS)
    rows = lambda c: slice(c * CHUNK, (c + 1) * CHUNK)

    def local_terms(cs):
        units = [(c, h) for c in cs for h in heads]
        q = {u: qkv_ref[rows(u[0]), u[1] * M_QK_DIM:(u[1] + 1) * M_QK_DIM] for u in units}
        k = {u: qkv_ref[rows(u[0]), M_QK + u[1] * M_QK_DIM:M_QK + (u[1] + 1) * M_QK_DIM] for u in units}
        v = {u: qkv_ref[rows(u[0]), 2 * M_QK + u[1] * M_V_DIM:2 * M_QK + (u[1] + 1) * M_V_DIM] for u in units}
        i_col = {u: gc[rows(u[0]), u[1]:u[1] + 1] for u in units}
        b_col = {u: b_cols[rows(u[0]), M_HEADS + u[1]:M_HEADS + u[1] + 1] for u in units}
        ib_row = {u: (b_rows[M_HEADS + u[1]:M_HEADS + u[1] + 1, rows(u[0])]
                      - gr[u[1]:u[1] + 1, rows(u[0])]) for u in units}
        d = {u: jnp.where(tri, b_col[u] - ib_row[u], -jnp.inf) for u in units}
        d_max = {u: jnp.max(d[u], axis=1, keepdims=True) for u in units}
        qk = {u: lax.dot_general(q[u], k[u], (((1,), (1,)), ((), ())), preferred_element_type=F32) for u in units}
        s_loc = {u: qk[u] * scale * jnp.exp(d[u] - d_max[u]) for u in units}
        a_loc = {u: jnp.dot(s_loc[u].astype(BF16), v[u], preferred_element_type=F32) for u in units}
        r_loc = {u: jnp.sum(s_loc[u], axis=1, keepdims=True) for u in units}
        b_last = {u: b_col[u][CHUNK - 1:CHUNK, :] for u in units}
        g_col = {u: b_last[u] - b_col[u] + i_col[u] for u in units}
        g_max = {u: jnp.max(g_col[u], axis=0, keepdims=True) for u in units}
        wk = {u: jnp.exp(g_col[u] - g_max[u]) * k[u].astype(F32) for u in units}
        u_loc = {u: jnp.dot(wk[u].T.astype(BF16), v[u], preferred_element_type=F32) for u in units}
        nk_loc = {u: jnp.sum(wk[u], axis=0, keepdims=True) for u in units}
        return dict(q=q, b_col=b_col, d_max=d_max, a_loc=a_loc, r_loc=r_loc, b_last=b_last, g_max=g_max,
                    u_loc=u_loc, nk_loc=nk_loc)

    def carried_step(c, loc, c_state, n_state, m_state):
        inter = [loc["b_col"][(c, h)] + m_state[h] for h in heads]
        m_t = [jnp.maximum(inter[h], loc["d_max"][(c, h)]) for h in heads]
        w_intra = [jnp.exp(loc["d_max"][(c, h)] - m_t[h]) for h in heads]
        w_inter = [jnp.exp(inter[h] - m_t[h]) for h in heads]
        qc = [jnp.dot(loc["q"][(c, h)], c_state[h].astype(BF16), preferred_element_type=F32) * scale for h in heads]
        qn = [jnp.sum(loc["q"][(c, h)].astype(F32) * n_state[h], axis=1, keepdims=True) * scale for h in heads]
        m_new = [jnp.maximum(loc["b_last"][(c, h)] + m_state[h], loc["g_max"][(c, h)]) for h in heads]
        decay = [jnp.exp(loc["b_last"][(c, h)] + m_state[h] - m_new[h]) for h in heads]
        beta = [jnp.exp(loc["g_max"][(c, h)] - m_new[h]) for h in heads]
        c_next = [decay[h] * c_state[h] + beta[h] * loc["u_loc"][(c, h)] for h in heads]
        n_next = [decay[h] * n_state[h] + beta[h] * loc["nk_loc"][(c, h)] for h in heads]
        num = [w_intra[h] * loc["a_loc"][(c, h)] + w_inter[h] * qc[h] for h in heads]
        den = [w_intra[h] * loc["r_loc"][(c, h)] + w_inter[h] * qn[h] for h in heads]
        hv = [num[h] * (1.0 / jnp.maximum(jnp.abs(den[h]), jnp.exp(-m_t[h]))) for h in heads]
        hv = [hv[h] * lax.rsqrt(jnp.mean(hv[h] * hv[h], axis=1, keepdims=True) + RMS_EPS) for h in heads]
        for h in heads:
            vs = slice(h * M_V_DIM, (h + 1) * M_V_DIM)
            mb_ref[rows(c), vs] = (osig_ref[rows(c), vs].astype(F32) * (hv[h] * mhg_ref[:, vs])).astype(mb_ref.dtype)
        return c_next, n_next, m_new

    c_state = [c_ref[h] for h in heads]
    n_state = [n_ref[h] for h in heads]
    m_state = [m_ref[h][:, 0:1] for h in heads]
    groups = [list(range(g, min(g + MLSTM_GROUP, chunks))) for g in range(0, chunks, MLSTM_GROUP)]
    loc = local_terms(groups[0])
    for gi, group in enumerate(groups):
        loc_next = local_terms(groups[gi + 1]) if gi + 1 < len(groups) else None
        for c in group:
            c_state, n_state, m_state = carried_step(c, loc, c_state, n_state, m_state)
        loc = loc_next

    for h in heads:
        c_ref[h], n_ref[h] = c_state[h], n_state[h]
        m_ref[h] = jnp.broadcast_to(m_state[h], (1, LANES))


def _mlstm(qkv, gates_col, gates_row, osig_src, mh_g, batch, seq, chunks=8):
    ts = chunks * CHUNK
    tiles = seq // ts
    n = batch * seq
    return pl.pallas_call(
        functools.partial(_mlstm_kernel, chunks=chunks),
        grid=(batch, tiles),
        in_specs=[
            pl.BlockSpec((ts, 2 * M_QK + M_V), lambda b, t: (b * tiles + t, 0)),
            pl.BlockSpec((ts, LANES), lambda b, t: (b * tiles + t, 0)),
            pl.BlockSpec((1, 2 * M_HEADS, ts), lambda b, t: (b, 0, t)),
            pl.BlockSpec((ts, M_V), lambda b, t: (b * tiles + t, 0)),
            pl.BlockSpec((1, M_V), lambda b, t: (0, 0)),
        ],
        out_specs=pl.BlockSpec((ts, M_V), lambda b, t: (b * tiles + t, 0)),
        out_shape=jax.ShapeDtypeStruct((n, M_V), BF16),
        scratch_shapes=[
            pltpu.VMEM((M_HEADS, M_QK_DIM, M_V_DIM), F32),
            pltpu.VMEM((M_HEADS, 1, M_QK_DIM), F32),
            pltpu.VMEM((M_HEADS, 1, LANES), F32),
        ],
        name="mlstm",
        compiler_params=_params("parallel", "arbitrary"),
    )(qkv, gates_col, gates_row, osig_src, mh_g)


def _merge_kernel(x_ref, xin_ref, a_ref, mb_ref, wga_ref, bga_ref, wgb_ref, bgb_ref, wpa_ref, wpb_ref, wout_ref,
                  g_ref, b_ref, xo_ref, xob_ref, xpa_ref, xpb_ref):
    xin = xin_ref[...]
    ga = jax.nn.sigmoid(jnp.dot(xin, wga_ref[...], preferred_element_type=F32) + bga_ref[...])
    gb = jax.nn.sigmoid(jnp.dot(xin, wgb_ref[...], preferred_element_type=F32) + bgb_ref[...])
    ya = jnp.dot(a_ref[...], wpa_ref[...], preferred_element_type=F32)
    yb = jnp.dot(mb_ref[...], wpb_ref[...], preferred_element_type=F32)
    y = ga * ya + gb * yb
    mix = jnp.dot(y.astype(BF16), wout_ref[...], preferred_element_type=F32)
    xn = _layer_norm(ALPHA * x_ref[...] + mix, g_ref[...], b_ref[...])
    xo_ref[...] = xn
    xob_ref[...] = xn.astype(BF16)
    xpa_ref[...], xpb_ref[...] = _pack_row(xn)


def _merge(x, xin, a, mb, w_ga, b_ga, w_gb, b_gb, w_pa, w_pb, w_out, ln_g, ln_b, tm=512):
    n = x.shape[0]
    row = lambda i: (i, 0)
    const = lambda i: (0, 0)
    wspec = pl.BlockSpec((D_MODEL, D_MODEL), const)
    vspec = pl.BlockSpec((1, D_MODEL), const)
    return pl.pallas_call(
        _merge_kernel,
        grid=(n // tm,),
        in_specs=[
            pl.BlockSpec((tm, D_MODEL), row),
            pl.BlockSpec((tm, D_MODEL), row),
            pl.BlockSpec((tm, G_WIDTH), row),
            pl.BlockSpec((tm, M_V), row),
            wspec, vspec, wspec, vspec, wspec, wspec, wspec, vspec, vspec,
        ],
        out_specs=[pl.BlockSpec((tm, D_MODEL), row), pl.BlockSpec((tm, D_MODEL), row),
                   pl.BlockSpec((tm, PART), row), pl.BlockSpec((tm, PART), row)],
        out_shape=[jax.ShapeDtypeStruct((n, D_MODEL), F32), jax.ShapeDtypeStruct((n, D_MODEL), BF16),
                   jax.ShapeDtypeStruct((n, PART), I32), jax.ShapeDtypeStruct((n, PART), I32)],
        name="merge_ln",
        compiler_params=_params("parallel"),
    )(x, xin, a, mb, w_ga, b_ga, w_gb, b_gb, w_pa, w_pb, w_out, ln_g, ln_b)


def _first_max(v, idx, axes, sentinel):
    m = jnp.max(v, axis=axes, keepdims=True)
    first = jnp.min(jnp.where(v == m, idx, sentinel), axis=axes, keepdims=True)
    return m, first


def _router_kernel(x_ref, wr_ref, br_ref, upper_ref, eidx_ref, rank_ref, wtok_ref, cnt_ref, run_ref, *, tm):
    @pl.when(pl.program_id(0) == 0)
    def _():
        run_ref[...] = jnp.zeros_like(run_ref)

    def split(v):
        hi = v.astype(BF16)
        return hi, (v - hi.astype(F32)).astype(BF16)

    nt = lambda a, b: lax.dot_general(a, b, (((1,), (1,)), ((), ())), preferred_element_type=F32)
    w_hi, w_lo = split(wr_ref[...])
    x_hi, x_lo = split(x_ref[...])
    logits = nt(w_hi, x_hi) + (nt(w_hi, x_lo) + nt(w_lo, x_hi))
    scores = jax.nn.sigmoid(logits)
    sel = (scores + br_ref[...]).reshape(N_GROUPS, GROUP_SIZE, tm)
    scores3 = scores.reshape(N_GROUPS, GROUP_SIZE, tm)
    member = lax.broadcasted_iota(jnp.int32, sel.shape, 1)
    group = lax.broadcasted_iota(jnp.int32, sel.shape, 0)
    neg = -jnp.inf

    m1, f1 = _first_max(sel, member, 1, GROUP_SIZE)
    m2 = jnp.max(jnp.where(member == f1, neg, sel), axis=1, keepdims=True)
    gscore = m1 + m2
    gid = lax.broadcasted_iota(jnp.int32, gscore.shape, 0)
    gmask = jnp.zeros(gscore.shape, dtype=jnp.bool_)
    for _ in range(TOPK_GROUPS):
        _, fg = _first_max(gscore, gid, 0, N_GROUPS)
        hit = gid == fg
        gmask = gmask | hit
        gscore = jnp.where(hit, neg, gscore)

    cand = jnp.where(gmask, sel, neg)
    eid = group * GROUP_SIZE + member
    chosen = jnp.zeros(sel.shape, dtype=jnp.bool_)
    picks = []
    for _ in range(TOP_K):
        _, fe = _first_max(cand, eid, (0, 1), N_EXPERTS)
        hit = eid == fe
        chosen = chosen | hit
        cand = jnp.where(hit, neg, cand)
        picks.append((fe[0], hit, jnp.sum(jnp.where(hit, scores3, 0.0), axis=(0, 1), keepdims=True)[0]))

    chosen2 = jnp.where(chosen, 1.0, 0.0).reshape(N_EXPERTS, tm)
    prefix = jnp.dot(chosen2.astype(BF16), upper_ref[...], preferred_element_type=F32) + run_ref[:, 0:1]
    prefix3 = prefix.reshape(N_GROUPS, GROUP_SIZE, tm)

    total = picks[0][2]
    for _, _, wk in picks[1:]:
        total = total + wk
    eidx_ref[...] = jnp.concatenate([fe for fe, _, _ in picks], axis=0)
    rank_ref[...] = jnp.concatenate(
        [jnp.sum(jnp.where(hit, prefix3, 0.0), axis=(0, 1), keepdims=True)[0] for _, hit, _ in picks],
        axis=0).astype(I32)
    w_rows = jnp.concatenate([wk / total * ROUTE_SCALE for _, _, wk in picks]
                             + [jnp.zeros((LANES - TOP_K, tm), F32)], axis=0)
    wtok_ref[...] = w_rows.T

    run = run_ref[...] + jnp.sum(chosen2, axis=1, keepdims=True)
    run_ref[...] = run
    cnt_ref[...] = run


def _router(x, wr_t, br, tm=512):
    n = x.shape[0]
    tok = lambda i: (0, i)
    upper = (jnp.arange(tm)[:, None] < jnp.arange(tm)[None, :]).astype(BF16)
    return pl.pallas_call(
        functools.partial(_router_kernel, tm=tm),
        grid=(n // tm,),
        in_specs=[
            pl.BlockSpec((tm, D_MODEL), lambda i: (i, 0)),
            pl.BlockSpec((N_EXPERTS, D_MODEL), lambda i: (0, 0)),
            pl.BlockSpec((N_EXPERTS, 1), lambda i: (0, 0)),
            pl.BlockSpec((tm, tm), lambda i: (0, 0)),
        ],
        out_specs=[
            pl.BlockSpec((TOP_K, tm), tok),
            pl.BlockSpec((TOP_K, tm), tok),
            pl.BlockSpec((tm, LANES), lambda i: (i, 0)),
            pl.BlockSpec((N_EXPERTS, LANES), lambda i: (0, 0)),
        ],
        out_shape=[
            jax.ShapeDtypeStruct((TOP_K, n), I32),
            jax.ShapeDtypeStruct((TOP_K, n), I32),
            jax.ShapeDtypeStruct((n, LANES), F32),
            jax.ShapeDtypeStruct((N_EXPERTS, LANES), F32),
        ],
        scratch_shapes=[pltpu.VMEM((N_EXPERTS, LANES), F32)],
        name="router",
        compiler_params=_params("arbitrary"),
    )(x, wr_t, br, upper)


def _slots_kernel(starts_ref, eidx_ref, rank_ref, slot_ref):
    eidx = eidx_ref[...]
    slot = rank_ref[...]
    for e in range(N_EXPERTS):
        slot = slot + jnp.where(eidx == e, starts_ref[e], 0)
    slot_ref[...] = slot


def _slots(starts, eidx, rank, tn=2048):
    n = eidx.shape[1]
    tn = min(tn, n)
    spec = pl.BlockSpec((TOP_K, tn), lambda i, s: (0, i))
    return pl.pallas_call(
        _slots_kernel,
        grid_spec=pltpu.PrefetchScalarGridSpec(
            num_scalar_prefetch=1, grid=(n // tn,), in_specs=[spec, spec], out_specs=spec),
        out_shape=jax.ShapeDtypeStruct((TOP_K, n), I32),
        name="slots",
        compiler_params=_params("parallel"),
    )(starts, eidx, rank)


def _sc_mesh():
    return plsc.VectorSubcoreMesh(core_axis_name="c", subcore_axis_name="s")


def _sc_dispatch(parts, slots_flat, rows):
    n, width = parts[0].shape
    blocks = n // SC_WINDOW
    out_type = [jax.ShapeDtypeStruct((rows, width), part.dtype) for part in parts]

    @functools.partial(pl.kernel, out_type=out_type, mesh=_sc_mesh(), scratch_types=[], name="sc_dispatch")
    def run(*refs):
        i_hbm = refs[len(parts)]
        for x_hbm, o_hbm in zip(refs[:len(parts)], refs[len(parts) + 1:]):
            def body(x_vmem, i_vmem, o_hbm=o_hbm):
                pltpu.sync_copy(x_vmem, o_hbm.at[i_vmem.at[0]])

            pltpu.emit_pipeline(
                body,
                grid=(blocks, TOP_K),
                in_specs=[pl.BlockSpec((SC_WINDOW, width), lambda i, k: (i, 0)),
                          pl.BlockSpec((1, SC_WINDOW), lambda i, k: (0, k * blocks + i))],
                out_specs=[],
                core_axis_name=("c", "s"),
                dimension_semantics=(pltpu.PARALLEL, pltpu.ARBITRARY),
                trace_scopes=False,
            )(x_hbm, i_hbm)

    return run(*parts, slots_flat)


def _sc_gather(tables, idx_flat):
    count = idx_flat.shape[1]
    width = tables[0].shape[1]
    out_type = [jax.ShapeDtypeStruct((count, width), table.dtype) for table in tables]

    @functools.partial(pl.kernel, out_type=out_type, mesh=_sc_mesh(), scratch_types=[], name="sc_gather")
    def run(*refs):
        i_hbm = refs[len(tables)]
        for t_hbm, o_hbm in zip(refs[:len(tables)], refs[len(tables) + 1:]):
            def body(i_vmem, o_vmem, t_hbm=t_hbm):
                pltpu.sync_copy(t_hbm.at[i_vmem.at[0]], o_vmem)

            pltpu.emit_pipeline(
                body,
                grid=(count // SC_WINDOW,),
                in_specs=[pl.BlockSpec((1, SC_WINDOW), lambda i: (0, i))],
                out_specs=[pl.BlockSpec((SC_WINDOW, width), lambda i: (i, 0))],
                core_axis_name=("c", "s"),
                dimension_semantics=(pltpu.PARALLEL,),
                trace_scopes=False,
            )(i_hbm, o_hbm)

    return run(*tables, idx_flat)


def _expert_kernel(te_ref, used_ref, xa_ref, xb_ref, wg_ref, wu_ref, wd_ref, oa_ref, ob_ref, wgb_ref, wub_ref, wdb_ref):
    j = pl.program_id(0)

    @pl.when((j == 0) | (te_ref[j] != te_ref[jnp.maximum(j - 1, 0)]))
    def _():
        wgb_ref[...] = wg_ref[0].astype(BF16)
        wub_ref[...] = wu_ref[0].astype(BF16)
        wdb_ref[...] = wd_ref[0].astype(BF16)

    @pl.when(j < used_ref[0])
    def _():
        x = jnp.concatenate([_unpack_pairs(xa_ref[...]), _unpack_pairs(xb_ref[...])], axis=1).astype(BF16)
        hg = jnp.dot(x, wgb_ref[...], preferred_element_type=F32)
        hu = jnp.dot(x, wub_ref[...], preferred_element_type=F32)
        hid = (_silu(hg) * hu).astype(BF16)
        oa_ref[...], ob_ref[...] = _pack_row(jnp.dot(hid, wdb_ref[...], preferred_element_type=F32))


def _experts(tile_expert, tiles_used, xa, xb, layer, wg, wu, wd):
    rows = xa.shape[0]
    row = lambda j, te, used: (j, 0)
    exp = lambda j, te, used: (layer * N_EXPERTS + te[j], 0, 0)
    return pl.pallas_call(
        _expert_kernel,
        grid_spec=pltpu.PrefetchScalarGridSpec(
            num_scalar_prefetch=2,
            grid=(rows // ROW_TILE,),
            in_specs=[
                pl.BlockSpec((ROW_TILE, PART), row),
                pl.BlockSpec((ROW_TILE, PART), row),
                pl.BlockSpec((1, D_MODEL, D_EXPERT), exp),
                pl.BlockSpec((1, D_MODEL, D_EXPERT), exp),
                pl.BlockSpec((1, D_EXPERT, D_MODEL), exp),
            ],
            out_specs=[pl.BlockSpec((ROW_TILE, PART), row), pl.BlockSpec((ROW_TILE, PART), row)],
            scratch_shapes=[pltpu.VMEM((D_MODEL, D_EXPERT), BF16), pltpu.VMEM((D_MODEL, D_EXPERT), BF16),
                            pltpu.VMEM((D_EXPERT, D_MODEL), BF16)],
        ),
        out_shape=[jax.ShapeDtypeStruct((rows, PART), I32), jax.ShapeDtypeStruct((rows, PART), I32)],
        name="experts",
        compiler_params=_params("arbitrary"),
    )(tile_expert, tiles_used, xa, xb, wg, wu, wd)


def _finish_kernel(x_ref, xb_ref, p_ref, ga_ref, gb_ref, wtok_ref, wgs_ref, wus_ref, wds_ref, wpg_ref, bpg_ref, wp_ref,
                   lng_ref, lnb_ref, *out_refs):
    xb = xb_ref[...]
    hs = _silu(jnp.dot(xb, wgs_ref[...], preferred_element_type=F32)) * jnp.dot(
        xb, wus_ref[...], preferred_element_type=F32)
    shared = jnp.dot(hs.astype(BF16), wds_ref[...], preferred_element_type=F32)
    gate = jax.nn.sigmoid(jnp.dot(xb, wpg_ref[...], preferred_element_type=F32) + bpg_ref[...])
    ple = gate * jnp.dot(p_ref[0].astype(BF16), wp_ref[...], preferred_element_type=F32)

    wtok = wtok_ref[...]
    r_a = None
    r_b = None
    for k in range(TOP_K):
        wk = wtok[:, k:k + 1]
        a = wk * _unpack_pairs(ga_ref[k])
        b = wk * _unpack_pairs(gb_ref[k])
        r_a = a if r_a is None else r_a + a
        r_b = b if r_b is None else r_b + b
    routed = jnp.concatenate([r_a, r_b], axis=1)

    xn = _layer_norm(ALPHA * x_ref[...] + (routed + shared + ple), lng_ref[...], lnb_ref[...])
    if len(out_refs) == 2 and out_refs[1].dtype == BF16:
        out_refs[0][...] = xn
        out_refs[1][...] = xn.astype(BF16)
    else:
        out_refs[-1][...] = xn


def _finish(x, xb, p_all, layer, stream, ga, gb, wtok, wgs, wus, wds, wpg, bpg, wp, ln_g, ln_b,
            out_base=None, total_rows=None, tm=512):
    n = x.shape[0]
    p_block = stream * (n // tm)
    row = lambda i: (i, 0)
    const = lambda i: (0, 0)
    if total_rows is None:
        out_specs = [pl.BlockSpec((tm, D_MODEL), row), pl.BlockSpec((tm, D_MODEL), row)]
        out_shape = [jax.ShapeDtypeStruct((n, D_MODEL), F32), jax.ShapeDtypeStruct((n, D_MODEL), BF16)]
    else:
        out_specs = [pl.BlockSpec((tm, D_MODEL), lambda i: (p_block + i, 0))]
        out_shape = [jax.ShapeDtypeStruct((total_rows, D_MODEL), F32)]
    extra_specs = [] if out_base is None else [pl.BlockSpec(memory_space=pl.ANY)]
    extra_args = [] if out_base is None else [out_base]
    aliases = {} if out_base is None else {14: 0}
    return pl.pallas_call(
        _finish_kernel,
        grid=(n // tm,),
        input_output_aliases=aliases,
        in_specs=[
            pl.BlockSpec((tm, D_MODEL), row),
            pl.BlockSpec((tm, D_MODEL), row),
            pl.BlockSpec((1, tm, P_DIM), lambda i: (layer, p_block + i, 0)),
            pl.BlockSpec((TOP_K, tm, PART), lambda i: (0, i, 0)),
            pl.BlockSpec((TOP_K, tm, PART), lambda i: (0, i, 0)),
            pl.BlockSpec((tm, LANES), row),
            pl.BlockSpec((D_MODEL, D_SHARED), const),
            pl.BlockSpec((D_MODEL, D_SHARED), const),
            pl.BlockSpec((D_SHARED, D_MODEL), const),
            pl.BlockSpec((D_MODEL, D_MODEL), const),
            pl.BlockSpec((1, D_MODEL), const),
            pl.BlockSpec((P_DIM, D_MODEL), const),
            pl.BlockSpec((1, D_MODEL), const),
            pl.BlockSpec((1, D_MODEL), const),
        ] + extra_specs,
        out_specs=out_specs,
        out_shape=out_shape,
        name="moe_finish_ln",
        compiler_params=_params("parallel"),
    )(x, xb, p_all, ga, gb, wtok, wgs, wus, wds, wpg, bpg, wp, ln_g, ln_b, *extra_args)


def _moe(x, xb, xpa, xpb, p_all, stream, w_router_t, b_router, layer, wg, wu, wd, wgs, wus, wds, wpg, bpg, wp, ln_g, ln_b,
         out_base=None, total_rows=None):
    n = x.shape[0]
    pairs = n * TOP_K
    tiles = pairs // ROW_TILE + N_EXPERTS
    eidx, rank, wtok, counts = _router(x, w_router_t, b_router)

    cnt = counts[:, 0].astype(I32)
    group_tiles = (cnt + ROW_TILE - 1) // ROW_TILE
    tile_end = jnp.cumsum(group_tiles)
    starts = (tile_end - group_tiles) * ROW_TILE
    tile_id = jnp.arange(tiles, dtype=I32)
    tile_expert = jnp.minimum(jnp.sum((tile_end[None, :] <= tile_id[:, None]).astype(I32), axis=1), N_EXPERTS - 1)

    slots_flat = _slots(starts, eidx, rank).reshape(1, pairs)
    xsa, xsb = _sc_dispatch([xpa, xpb], slots_flat, tiles * ROW_TILE)
    ysa, ysb = _experts(tile_expert, tile_end[N_EXPERTS - 1:], xsa, xsb, layer, wg, wu, wd)
    ga, gb = _sc_gather([ysa, ysb], slots_flat)
    return _finish(x, xb, p_all, layer, stream, ga.reshape(TOP_K, n, PART), gb.reshape(TOP_K, n, PART), wtok, wgs, wus, wds, wpg, bpg, wp, ln_g, ln_b,
                   out_base=out_base, total_rows=total_rows)


def kernel(x, p, w_in, b_in, sg_ln_g, sg_ln_b, w_s, b_s, mh_g, w_pa, w_pb, w_out, ln1_g, ln1_b, w_router, b_router, w_gate_e, w_up_e, w_down_e, w_gate_s, w_up_s, w_down_s, w_pg, b_pg, w_p, ln2_g, ln2_b):
    batch, seq, _ = x.shape
    streams = STREAMS if batch % STREAMS == 0 else 1
    sb = batch // streams
    n = sb * seq
    xf = [x[s * sb:(s + 1) * sb].reshape(n, D_MODEL) for s in range(streams)]
    xb = [xs.astype(BF16) for xs in xf]
    p_all = p.reshape(DEPTH, batch * seq, P_DIM)

    wg_all = w_gate_e.reshape(DEPTH * N_EXPERTS, D_MODEL, D_EXPERT)
    wu_all = w_up_e.reshape(DEPTH * N_EXPERTS, D_MODEL, D_EXPERT)
    wd_all = w_down_e.reshape(DEPTH * N_EXPERTS, D_EXPERT, D_MODEL)

    c_uv = 2 * G_WIDTH
    c_qkv = c_uv + 2 * M_QK + M_V
    c_o = c_qkv + M_V
    c_if = c_o + 2 * M_HEADS

    for l in range(DEPTH):
        w = w_in[l]
        b = b_in[l][None, :]
        sgu_w = (w[:, :G_WIDTH].astype(BF16), b[:, :G_WIDTH], w[:, G_WIDTH:c_uv].astype(BF16), b[:, G_WIDTH:c_uv],
                 sg_ln_g[l][None, :], sg_ln_b[l][None, :], w_s[l], b_s[l].T)
        w_qkv, b_qkv = w[:, c_uv:c_qkv].astype(BF16), b[:, c_uv:c_qkv]
        w_o, b_o = w[:, c_qkv:c_o].astype(BF16), b[:, c_qkv:c_o]
        w_if = jnp.pad(w[:, c_o:c_if], ((0, 0), (0, LANES - 2 * M_HEADS))).astype(BF16)
        b_if = jnp.pad(b[:, c_o:c_if], ((0, 0), (0, LANES - 2 * M_HEADS)))
        c_gb = c_if + D_MODEL
        mix_w = (w[:, c_if:c_gb].astype(BF16), b[:, c_if:c_gb], w[:, c_gb:].astype(BF16), b[:, c_gb:],
                 w_pa[l].astype(BF16), w_pb[l].astype(BF16), w_out[l].astype(BF16),
                 ln1_g[l][None, :], ln1_b[l][None, :])
        moe_w = (w_router[l].T, b_router[l][:, None],
                 l, wg_all, wu_all, wd_all,
                 w_gate_s[l].astype(BF16), w_up_s[l].astype(BF16), w_down_s[l].astype(BF16),
                 w_pg[l].astype(BF16), b_pg[l][None, :], w_p[l].astype(BF16),
                 ln2_g[l][None, :], ln2_b[l][None, :])

        packed = []
        for s in range(streams):
            qkv, osig, gif = _mproj(xb[s], w_qkv, b_qkv, w_o, b_o, w_if, b_if)
            gif_rows = gif[:, :2 * M_HEADS].reshape(sb, seq, 2 * M_HEADS).transpose(0, 2, 1)
            a = _sgu(xb[s], *sgu_w)
            mb = _mlstm(qkv, gif, gif_rows, osig, mh_g[l][None, :], sb, seq)
            xf[s], xb[s], xpa, xpb = _merge(xf[s], xb[s], a, mb, *mix_w)
            packed.append((xpa, xpb))
        if l < DEPTH - 1:
            for s in range(streams):
                xf[s], xb[s] = _moe(xf[s], xb[s], *packed[s], p_all, s, *moe_w)
        else:
            out = None
            for s in range(streams):
                out, = _moe(xf[s], xb[s], *packed[s], p_all, s, *moe_w, out_base=out, total_rows=batch * seq)
    return out.reshape(batch, seq, D_MODEL)
```

```python
import functools

import jax
import jax.numpy as jnp
from jax import lax
from jax.experimental import pallas as pl
from jax.experimental.pallas import tpu as pltpu
from jax.experimental.pallas import tpu_sc as plsc

D_MODEL = 1024
DEPTH = 4
CHUNK = 64
P_DIM = 256
G_WIDTH = 1024
G_GROUPS = 8
G_GROUP_DIM = G_WIDTH // G_GROUPS
G_BLOCK = 128
M_HEADS = 4
M_QK_DIM = 128
M_V_DIM = 256
M_QK = M_HEADS * M_QK_DIM
M_V = M_HEADS * M_V_DIM
N_EXPERTS = 64
TOP_K = 8
N_GROUPS = 8
TOPK_GROUPS = 4
GROUP_SIZE = N_EXPERTS // N_GROUPS
D_EXPERT = 256
D_SHARED = 256
ROUTE_SCALE = 2.5
ALPHA = (2 * DEPTH) ** 0.25
LN_EPS = 1e-5
RMS_EPS = 1e-6

LANES = 128
VMEM_LIMIT = 56 * 1024 * 1024
HALF = D_MODEL // 2
PART = HALF // 2
ROW_TILE = 512
STREAMS = 2
MLSTM_GROUP = 8
SC_WINDOW = 128

F32 = jnp.float32
BF16 = jnp.bfloat16
I32 = jnp.int32


def _params(*semantics):
    return pltpu.CompilerParams(dimension_semantics=semantics, vmem_limit_bytes=VMEM_LIMIT)


def _layer_norm(x, g, b):
    mu = jnp.mean(x, axis=-1, keepdims=True)
    xc = x - mu
    var = jnp.mean(xc * xc, axis=-1, keepdims=True)
    return xc * lax.rsqrt(var + LN_EPS) * g + b


def _gelu(x):
    return 0.5 * x * (1.0 + lax.erf(x * (2.0 ** -0.5)))


def _silu(x):
    return x * jax.nn.sigmoid(x)


def _pack_pairs(x):
    lo = lax.bitcast_convert_type(x[:, :PART].astype(BF16).astype(F32), I32)
    hi = lax.bitcast_convert_type(x[:, PART:].astype(BF16).astype(F32), I32)
    return lax.shift_right_logical(lo, 16) | (hi & jnp.int32(-65536))


def _pack_row(x):
    return _pack_pairs(x[:, :HALF]), _pack_pairs(x[:, HALF:])


def _unpack_pairs(w):
    lo = lax.bitcast_convert_type(lax.shift_left(w, 16), F32)
    hi = lax.bitcast_convert_type(w & jnp.int32(-65536), F32)
    return jnp.concatenate([lo, hi], axis=1)


def _mproj_kernel(x_ref, wqkv_ref, bqkv_ref, wo_ref, bo_ref, wif_ref, bif_ref, qkv_ref, osig_ref, gif_ref):
    x = x_ref[...]
    qkv_ref[...] = (jnp.dot(x, wqkv_ref[...], preferred_element_type=F32) + bqkv_ref[...]).astype(qkv_ref.dtype)
    osig_ref[...] = jax.nn.sigmoid(
        jnp.dot(x, wo_ref[...], preferred_element_type=F32) + bo_ref[...]).astype(osig_ref.dtype)
    gif_ref[...] = jnp.dot(x, wif_ref[...], preferred_element_type=F32) + bif_ref[...]


def _mproj(x, w_qkv, b_qkv, w_o, b_o, w_if, b_if, tm=512):
    n, k = x.shape
    row = lambda i: (i, 0)
    const = lambda i: (0, 0)
    widths = (w_qkv.shape[1], w_o.shape[1], w_if.shape[1])
    in_specs = [pl.BlockSpec((tm, k), row)]
    for width in widths:
        in_specs += [pl.BlockSpec((k, width), const), pl.BlockSpec((1, width), const)]
    return pl.pallas_call(
        _mproj_kernel,
        grid=(n // tm,),
        in_specs=in_specs,
        out_specs=[pl.BlockSpec((tm, width), row) for width in widths],
        out_shape=[jax.ShapeDtypeStruct((n, widths[0]), BF16), jax.ShapeDtypeStruct((n, widths[1]), BF16),
                   jax.ShapeDtypeStruct((n, widths[2]), F32)],
        name="mlstm_proj",
        compiler_params=_params("parallel"),
    )(x, w_qkv, b_qkv, w_o, b_o, w_if, b_if)


def _sgu_kernel(x_ref, wu_ref, bu_ref, wv_ref, bv_ref, lng_ref, lnb_ref, ws_ref, bs_ref, a_ref, *, blocks):
    x = x_ref[...]
    v = _gelu(jnp.dot(x, wv_ref[...], preferred_element_type=F32) + bv_ref[...])
    vln = _layer_norm(v, lng_ref[...], lnb_ref[...]).astype(BF16)
    u = _gelu(jnp.dot(x, wu_ref[...], preferred_element_type=F32) + bu_ref[...])
    t_chunk = lax.broadcasted_iota(jnp.int32, (G_BLOCK, G_BLOCK), 0) // CHUNK
    s_chunk = lax.broadcasted_iota(jnp.int32, (G_BLOCK, G_BLOCK), 1) // CHUNK
    causal = s_chunk <= t_chunk
    for g in range(G_GROUPS):
        w = jnp.where(causal, ws_ref[g], 0.0).astype(BF16)
        bias = bs_ref[:, g:g + 1]
        cs = slice(g * G_GROUP_DIM, (g + 1) * G_GROUP_DIM)
        for blk in range(blocks):
            rs = slice(blk * G_BLOCK, (blk + 1) * G_BLOCK)
            mixed = jnp.dot(w, vln[rs, cs], preferred_element_type=F32) + bias
            a_ref[rs, cs] = (u[rs, cs] * mixed).astype(a_ref.dtype)


def _sgu(x, w_u, b_u, w_v, b_v, ln_g, ln_b, w_s, b_s_t, blocks=4):
    n = x.shape[0]
    tp = blocks * G_BLOCK
    const = lambda i: (0, 0)
    wspec = pl.BlockSpec((D_MODEL, G_WIDTH), const)
    vspec = pl.BlockSpec((1, G_WIDTH), const)
    return pl.pallas_call(
        functools.partial(_sgu_kernel, blocks=blocks),
        grid=(n // tp,),
        in_specs=[
            pl.BlockSpec((tp, D_MODEL), lambda i: (i, 0)),
            wspec, vspec, wspec, vspec, vspec, vspec,
            pl.BlockSpec((G_GROUPS, G_BLOCK, G_BLOCK), lambda i: (0, 0, 0)),
            pl.BlockSpec((G_BLOCK, G_GROUPS), const),
        ],
        out_specs=pl.BlockSpec((tp, G_WIDTH), lambda i: (i, 0)),
        out_shape=jax.ShapeDtypeStruct((n, G_WIDTH), BF16),
        name="spatial_gating",
        compiler_params=_params("parallel"),
    )(x, w_u, b_u, w_v, b_v, ln_g, ln_b, w_s, b_s_t)


def _segment_cumsum(x, axis, seg):
    pos = lax.broadcasted_iota(jnp.int32, x.shape, axis) % seg
    shift = 1
    while shift < seg:
        x = x + jnp.where(pos >= shift, pltpu.roll(x, shift, axis), 0.0)
        shift *= 2
    return x


def _mlstm_kernel(qkv_ref, gc_ref, gr_ref, osig_ref, mhg_ref, mb_ref, c_ref, n_ref, m_ref, *, chunks):
    @pl.when(pl.program_id(1) == 0)
    def _():
        c_ref[...] = jnp.zeros_like(c_ref)
        n_ref[...] = jnp.zeros_like(n_ref)
        m_ref[...] = jnp.zeros_like(m_ref)

    scale = M_QK_DIM ** -0.5
    gc = gc_ref[...]
    gr = gr_ref[0]
    b_cols = _segment_cumsum(jax.nn.log_sigmoid(gc), 0, CHUNK)
    b_rows = _segment_cumsum(jax.nn.log_sigmoid(gr), 1, CHUNK)
    t_idx = lax.broadcasted_iota(jnp.int32, (CHUNK, CHUNK), 0)
    s_idx = lax.broadcasted_iota(jnp.int32, (CHUNK, CHUNK), 1)
    tri = s_idx <= t_idx

    heads = range(M_HEADS)
    rows = lambda c: slice(c * CHUNK, (c + 1) * CHUNK)

    def local_terms(cs):
        units = [(c, h) for c in cs for h in heads]
        q = {u: qkv_ref[rows(u[0]), u[1] * M_QK_DIM:(u[1] + 1) * M_QK_DIM] for u in units}
        k = {u: qkv_ref[rows(u[0]), M_QK + u[1] * M_QK_DIM:M_QK + (u[1] + 1) * M_QK_DIM] for u in units}
        v = {u: qkv_ref[rows(u[0]), 2 * M_QK + u[1] * M_V_DIM:2 * M_QK + (u[1] + 1) * M_V_DIM] for u in units}
        i_col = {u: gc[rows(u[0]), u[1]:u[1] + 1] for u in units}
        b_col = {u: b_cols[rows(u[0]), M_HEADS + u[1]:M_HEADS + u[1] + 1] for u in units}
        ib_row = {u: (b_rows[M_HEADS + u[1]:M_HEADS + u[1] + 1, rows(u[0])]
                      - gr[u[1]:u[1] + 1, rows(u[0])]) for u in units}
        d = {u: jnp.where(tri, b_col[u] - ib_row[u], -jnp.inf) for u in units}
        d_max = {u: jnp.max(d[u], axis=1, keepdims=True) for u in units}
        qk = {u: lax.dot_general(q[u], k[u], (((1,), (1,)), ((), ())), preferred_element_type=F32) for u in units}
        s_loc = {u: qk[u] * scale * jnp.exp(d[u] - d_max[u]) for u in units}
        a_loc = {u: jnp.dot(s_loc[u].astype(BF16), v[u], preferred_element_type=F32) for u in units}
        r_loc = {u: jnp.sum(s_loc[u], axis=1, keepdims=True) for u in units}
        b_last = {u: b_col[u][CHUNK - 1:CHUNK, :] for u in units}
        g_col = {u: b_last[u] - b_col[u] + i_col[u] for u in units}
        g_max = {u: jnp.max(g_col[u], axis=0, keepdims=True) for u in units}
        wk = {u: jnp.exp(g_col[u] - g_max[u]) * k[u].astype(F32) for u in units}
        u_loc = {u: jnp.dot(wk[u].T.astype(BF16), v[u], preferred_element_type=F32) for u in units}
        nk_loc = {u: jnp.sum(wk[u], axis=0, keepdims=True) for u in units}
        return dict(q=q, b_col=b_col, d_max=d_max, a_loc=a_loc, r_loc=r_loc, b_last=b_last, g_max=g_max,
                    u_loc=u_loc, nk_loc=nk_loc)

    def carried_step(c, loc, c_state, n_state, m_state):
        inter = [loc["b_col"][(c, h)] + m_state[h] for h in heads]
        m_t = [jnp.maximum(inter[h], loc["d_max"][(c, h)]) for h in heads]
        w_intra = [jnp.exp(loc["d_max"][(c, h)] - m_t[h]) for h in heads]
        w_inter = [jnp.exp(inter[h] - m_t[h]) for h in heads]
        qc = [jnp.dot(loc["q"][(c, h)], c_state[h].astype(BF16), preferred_element_type=F32) * scale for h in heads]
        qn = [jnp.sum(loc["q"][(c, h)].astype(F32) * n_state[h], axis=1, keepdims=True) * scale for h in heads]
        m_new = [jnp.maximum(loc["b_last"][(c, h)] + m_state[h], loc["g_max"][(c, h)]) for h in heads]
        decay = [jnp.exp(loc["b_last"][(c, h)] + m_state[h] - m_new[h]) for h in heads]
        beta = [jnp.exp(loc["g_max"][(c, h)] - m_new[h]) for h in heads]
        c_next = [decay[h] * c_state[h] + beta[h] * loc["u_loc"][(c, h)] for h in heads]
        n_next = [decay[h] * n_state[h] + beta[h] * loc["nk_loc"][(c, h)] for h in heads]
        num = [w_intra[h] * loc["a_loc"][(c, h)] + w_inter[h] * qc[h] for h in heads]
        den = [w_intra[h] * loc["r_loc"][(c, h)] + w_inter[h] * qn[h] for h in heads]
        hv = [num[h] * (1.0 / jnp.maximum(jnp.abs(den[h]), jnp.exp(-m_t[h]))) for h in heads]
        hv = [hv[h] * lax.rsqrt(jnp.mean(hv[h] * hv[h], axis=1, keepdims=True) + RMS_EPS) for h in heads]
        for h in heads:
            vs = slice(h * M_V_DIM, (h + 1) * M_V_DIM)
            mb_ref[rows(c), vs] = (osig_ref[rows(c), vs].astype(F32) * (hv[h] * mhg_ref[:, vs])).astype(mb_ref.dtype)
        return c_next, n_next, m_new

    c_state = [c_ref[h] for h in heads]
    n_state = [n_ref[h] for h in heads]
    m_state = [m_ref[h][:, 0:1] for h in heads]
    groups = [list(range(g, min(g + MLSTM_GROUP, chunks))) for g in range(0, chunks, MLSTM_GROUP)]
    loc = local_terms(groups[0])
    for gi, group in enumerate(groups):
        loc_next = local_terms(groups[gi + 1]) if gi + 1 < len(groups) else None
        for c in group:
            c_state, n_state, m_state = carried_step(c, loc, c_state, n_state, m_state)
        loc = loc_next

    for h in heads:
        c_ref[h], n_ref[h] = c_state[h], n_state[h]
        m_ref[h] = jnp.broadcast_to(m_state[h], (1, LANES))


def _mlstm(qkv, gates_col, gates_row, osig_src, mh_g, batch, seq, chunks=8):
    ts = chunks * CHUNK
    tiles = seq // ts
    n = batch * seq
    return pl.pallas_call(
        functools.partial(_mlstm_kernel, chunks=chunks),
        grid=(batch, tiles),
        in_specs=[
            pl.BlockSpec((ts, 2 * M_QK + M_V), lambda b, t: (b * tiles + t, 0)),
            pl.BlockSpec((ts, LANES), lambda b, t: (b * tiles + t, 0)),
            pl.BlockSpec((1, 2 * M_HEADS, ts), lambda b, t: (b, 0, t)),
            pl.BlockSpec((ts, M_V), lambda b, t: (b * tiles + t, 0)),
            pl.BlockSpec((1, M_V), lambda b, t: (0, 0)),
        ],
        out_specs=pl.BlockSpec((ts, M_V), lambda b, t: (b * tiles + t, 0)),
        out_shape=jax.ShapeDtypeStruct((n, M_V), BF16),
        scratch_shapes=[
            pltpu.VMEM((M_HEADS, M_QK_DIM, M_V_DIM), F32),
            pltpu.VMEM((M_HEADS, 1, M_QK_DIM), F32),
            pltpu.VMEM((M_HEADS, 1, LANES), F32),
        ],
        name="mlstm",
        compiler_params=_params("parallel", "arbitrary"),
    )(qkv, gates_col, gates_row, osig_src, mh_g)


def _merge_kernel(x_ref, xin_ref, a_ref, mb_ref, wga_ref, bga_ref, wgb_ref, bgb_ref, wpa_ref, wpb_ref, wout_ref,
                  g_ref, b_ref, xo_ref, xob_ref, xpa_ref, xpb_ref):
    xin = xin_ref[...]
    ga = jax.nn.sigmoid(jnp.dot(xin, wga_ref[...], preferred_element_type=F32) + bga_ref[...])
    gb = jax.nn.sigmoid(jnp.dot(xin, wgb_ref[...], preferred_element_type=F32) + bgb_ref[...])
    ya = jnp.dot(a_ref[...], wpa_ref[...], preferred_element_type=F32)
    yb = jnp.dot(mb_ref[...], wpb_ref[...], preferred_element_type=F32)
    y = ga * ya + gb * yb
    mix = jnp.dot(y.astype(BF16), wout_ref[...], preferred_element_type=F32)
    xn = _layer_norm(ALPHA * x_ref[...] + mix, g_ref[...], b_ref[...])
    xo_ref[...] = xn
    xob_ref[...] = xn.astype(BF16)
    xpa_ref[...], xpb_ref[...] = _pack_row(xn)


def _merge(x, xin, a, mb, w_ga, b_ga, w_gb, b_gb, w_pa, w_pb, w_out, ln_g, ln_b, tm=512):
    n = x.shape[0]
    row = lambda i: (i, 0)
    const = lambda i: (0, 0)
    wspec = pl.BlockSpec((D_MODEL, D_MODEL), const)
    vspec = pl.BlockSpec((1, D_MODEL), const)
    return pl.pallas_call(
        _merge_kernel,
        grid=(n // tm,),
        in_specs=[
            pl.BlockSpec((tm, D_MODEL), row),
            pl.BlockSpec((tm, D_MODEL), row),
            pl.BlockSpec((tm, G_WIDTH), row),
            pl.BlockSpec((tm, M_V), row),
            wspec, vspec, wspec, vspec, wspec, wspec, wspec, vspec, vspec,
        ],
        out_specs=[pl.BlockSpec((tm, D_MODEL), row), pl.BlockSpec((tm, D_MODEL), row),
                   pl.BlockSpec((tm, PART), row), pl.BlockSpec((tm, PART), row)],
        out_shape=[jax.ShapeDtypeStruct((n, D_MODEL), F32), jax.ShapeDtypeStruct((n, D_MODEL), BF16),
                   jax.ShapeDtypeStruct((n, PART), I32), jax.ShapeDtypeStruct((n, PART), I32)],
        name="merge_ln",
        compiler_params=_params("parallel"),
    )(x, xin, a, mb, w_ga, b_ga, w_gb, b_gb, w_pa, w_pb, w_out, ln_g, ln_b)


def _first_max(v, idx, axes, sentinel):
    m = jnp.max(v, axis=axes, keepdims=True)
    first = jnp.min(jnp.where(v == m, idx, sentinel), axis=axes, keepdims=True)
    return m, first


def _router_kernel(x_ref, wr_ref, br_ref, upper_ref, eidx_ref, rank_ref, wtok_ref, cnt_ref, run_ref, *, tm):
    @pl.when(pl.program_id(0) == 0)
    def _():
        run_ref[...] = jnp.zeros_like(run_ref)

    def split(v):
        hi = v.astype(BF16)
        return hi, (v - hi.astype(F32)).astype(BF16)

    nt = lambda a, b: lax.dot_general(a, b, (((1,), (1,)), ((), ())), preferred_element_type=F32)
    w_hi, w_lo = split(wr_ref[...])
    x_hi, x_lo = split(x_ref[...])
    logits = nt(w_hi, x_hi) + (nt(w_hi, x_lo) + nt(w_lo, x_hi))
    scores = jax.nn.sigmoid(logits)
    sel = (scores + br_ref[...]).reshape(N_GROUPS, GROUP_SIZE, tm)
    scores3 = scores.reshape(N_GROUPS, GROUP_SIZE, tm)
    member = lax.broadcasted_iota(jnp.int32, sel.shape, 1)
    group = lax.broadcasted_iota(jnp.int32, sel.shape, 0)
    neg = -jnp.inf

    m1, f1 = _first_max(sel, member, 1, GROUP_SIZE)
    m2 = jnp.max(jnp.where(member == f1, neg, sel), axis=1, keepdims=True)
    gscore = m1 + m2
    gid = lax.broadcasted_iota(jnp.int32, gscore.shape, 0)
    gmask = jnp.zeros(gscore.shape, dtype=jnp.bool_)
    for _ in range(TOPK_GROUPS):
        _, fg = _first_max(gscore, gid, 0, N_GROUPS)
        hit = gid == fg
        gmask = gmask | hit
        gscore = jnp.where(hit, neg, gscore)

    cand = jnp.where(gmask, sel, neg)
    eid = group * GROUP_SIZE + member
    chosen = jnp.zeros(sel.shape, dtype=jnp.bool_)
    picks = []
    for _ in range(TOP_K):
        _, fe = _first_max(cand, eid, (0, 1), N_EXPERTS)
        hit = eid == fe
        chosen = chosen | hit
        cand = jnp.where(hit, neg, cand)
        picks.append((fe[0], hit, jnp.sum(jnp.where(hit, scores3, 0.0), axis=(0, 1), keepdims=True)[0]))

    chosen2 = jnp.where(chosen, 1.0, 0.0).reshape(N_EXPERTS, tm)
    prefix = jnp.dot(chosen2.astype(BF16), upper_ref[...], preferred_element_type=F32) + run_ref[:, 0:1]
    prefix3 = prefix.reshape(N_GROUPS, GROUP_SIZE, tm)

    total = picks[0][2]
    for _, _, wk in picks[1:]:
        total = total + wk
    eidx_ref[...] = jnp.concatenate([fe for fe, _, _ in picks], axis=0)
    rank_ref[...] = jnp.concatenate(
        [jnp.sum(jnp.where(hit, prefix3, 0.0), axis=(0, 1), keepdims=True)[0] for _, hit, _ in picks],
        axis=0).astype(I32)
    w_rows = jnp.concatenate([wk / total * ROUTE_SCALE for _, _, wk in picks]
                             + [jnp.zeros((LANES - TOP_K, tm), F32)], axis=0)
    wtok_ref[...] = w_rows.T

    run = run_ref[...] + jnp.sum(chosen2, axis=1, keepdims=True)
    run_ref[...] = run
    cnt_ref[...] = run


def _router(x, wr_t, br, tm=512):
    n = x.shape[0]
    tok = lambda i: (0, i)
    upper = (jnp.arange(tm)[:, None] < jnp.arange(tm)[None, :]).astype(BF16)
    return pl.pallas_call(
        functools.partial(_router_kernel, tm=tm),
        grid=(n // tm,),
        in_specs=[
            pl.BlockSpec((tm, D_MODEL), lambda i: (i, 0)),
            pl.BlockSpec((N_EXPERTS, D_MODEL), lambda i: (0, 0)),
            pl.BlockSpec((N_EXPERTS, 1), lambda i: (0, 0)),
            pl.BlockSpec((tm, tm), lambda i: (0, 0)),
        ],
        out_specs=[
            pl.BlockSpec((TOP_K, tm), tok),
            pl.BlockSpec((TOP_K, tm), tok),
            pl.BlockSpec((tm, LANES), lambda i: (i, 0)),
            pl.BlockSpec((N_EXPERTS, LANES), lambda i: (0, 0)),
        ],
        out_shape=[
            jax.ShapeDtypeStruct((TOP_K, n), I32),
            jax.ShapeDtypeStruct((TOP_K, n), I32),
            jax.ShapeDtypeStruct((n, LANES), F32),
            jax.ShapeDtypeStruct((N_EXPERTS, LANES), F32),
        ],
        scratch_shapes=[pltpu.VMEM((N_EXPERTS, LANES), F32)],
        name="router",
        compiler_params=_params("arbitrary"),
    )(x, wr_t, br, upper)


def _slots_kernel(starts_ref, eidx_ref, rank_ref, slot_ref):
    eidx = eidx_ref[...]
    slot = rank_ref[...]
    for e in range(N_EXPERTS):
        slot = slot + jnp.where(eidx == e, starts_ref[e], 0)
    slot_ref[...] = slot


def _slots(starts, eidx, rank, tn=2048):
    n = eidx.shape[1]
    tn = min(tn, n)
    spec = pl.BlockSpec((TOP_K, tn), lambda i, s: (0, i))
    return pl.pallas_call(
        _slots_kernel,
        grid_spec=pltpu.PrefetchScalarGridSpec(
            num_scalar_prefetch=1, grid=(n // tn,), in_specs=[spec, spec], out_specs=spec),
        out_shape=jax.ShapeDtypeStruct((TOP_K, n), I32),
        name="slots",
        compiler_params=_params("parallel"),
    )(starts, eidx, rank)


def _sc_mesh():
    return plsc.VectorSubcoreMesh(core_axis_name="c", subcore_axis_name="s")


def _sc_dispatch(parts, slots_flat, rows):
    n, width = parts[0].shape
    blocks = n // SC_WINDOW
    out_type = [jax.ShapeDtypeStruct((rows, width), part.dtype) for part in parts]

    @functools.partial(pl.kernel, out_type=out_type, mesh=_sc_mesh(), scratch_types=[], name="sc_dispatch")
    def run(*refs):
        i_hbm = refs[len(parts)]
        for x_hbm, o_hbm in zip(refs[:len(parts)], refs[len(parts) + 1:]):
            def body(x_vmem, i_vmem, o_hbm=o_hbm):
                pltpu.sync_copy(x_vmem, o_hbm.at[i_vmem.at[0]])

            pltpu.emit_pipeline(
                body,
                grid=(blocks, TOP_K),
                in_specs=[pl.BlockSpec((SC_WINDOW, width), lambda i, k: (i, 0)),
                          pl.BlockSpec((1, SC_WINDOW), lambda i, k: (0, k * blocks + i))],
                out_specs=[],
                core_axis_name=("c", "s"),
                dimension_semantics=(pltpu.PARALLEL, pltpu.ARBITRARY),
                trace_scopes=False,
            )(x_hbm, i_hbm)

    return run(*parts, slots_flat)


def _sc_gather(tables, idx_flat):
    count = idx_flat.shape[1]
    width = tables[0].shape[1]
    out_type = [jax.ShapeDtypeStruct((count, width), table.dtype) for table in tables]

    @functools.partial(pl.kernel, out_type=out_type, mesh=_sc_mesh(), scratch_types=[], name="sc_gather")
    def run(*refs):
        i_hbm = refs[len(tables)]
        for t_hbm, o_hbm in zip(refs[:len(tables)], refs[len(tables) + 1:]):
            def body(i_vmem, o_vmem, t_hbm=t_hbm):
                pltpu.sync_copy(t_hbm.at[i_vmem.at[0]], o_vmem)

            pltpu.emit_pipeline(
                body,
                grid=(count // SC_WINDOW,),
                in_specs=[pl.BlockSpec((1, SC_WINDOW), lambda i: (0, i))],
                out_specs=[pl.BlockSpec((SC_WINDOW, width), lambda i: (i, 0))],
                core_axis_name=("c", "s"),
                dimension_semantics=(pltpu.PARALLEL,),
                trace_scopes=False,
            )(i_hbm, o_hbm)

    return run(*tables, idx_flat)


def _expert_kernel(te_ref, used_ref, xa_ref, xb_ref, wg_ref, wu_ref, wd_ref, oa_ref, ob_ref, wgb_ref, wub_ref, wdb_ref):
    j = pl.program_id(0)

    @pl.when((j == 0) | (te_ref[j] != te_ref[jnp.maximum(j - 1, 0)]))
    def _():
        wgb_ref[...] = wg_ref[0].astype(BF16)
        wub_ref[...] = wu_ref[0].astype(BF16)
        wdb_ref[...] = wd_ref[0].astype(BF16)

    @pl.when(j < used_ref[0])
    def _():
        x = jnp.concatenate([_unpack_pairs(xa_ref[...]), _unpack_pairs(xb_ref[...])], axis=1).astype(BF16)
        hg = jnp.dot(x, wgb_ref[...], preferred_element_type=F32)
        hu = jnp.dot(x, wub_ref[...], preferred_element_type=F32)
        hid = (_silu(hg) * hu).astype(BF16)
        oa_ref[...], ob_ref[...] = _pack_row(jnp.dot(hid, wdb_ref[...], preferred_element_type=F32))


def _experts(tile_expert, tiles_used, xa, xb, layer, wg, wu, wd):
    rows = xa.shape[0]
    row = lambda j, te, used: (j, 0)
    exp = lambda j, te, used: (layer * N_EXPERTS + te[j], 0, 0)
    return pl.pallas_call(
        _expert_kernel,
        grid_spec=pltpu.PrefetchScalarGridSpec(
            num_scalar_prefetch=2,
            grid=(rows // ROW_TILE,),
            in_specs=[
                pl.BlockSpec((ROW_TILE, PART), row),
                pl.BlockSpec((ROW_TILE, PART), row),
                pl.BlockSpec((1, D_MODEL, D_EXPERT), exp),
                pl.BlockSpec((1, D_MODEL, D_EXPERT), exp),
                pl.BlockSpec((1, D_EXPERT, D_MODEL), exp),
            ],
            out_specs=[pl.BlockSpec((ROW_TILE, PART), row), pl.BlockSpec((ROW_TILE, PART), row)],
            scratch_shapes=[pltpu.VMEM((D_MODEL, D_EXPERT), BF16), pltpu.VMEM((D_MODEL, D_EXPERT), BF16),
                            pltpu.VMEM((D_EXPERT, D_MODEL), BF16)],
        ),
        out_shape=[jax.ShapeDtypeStruct((rows, PART), I32), jax.ShapeDtypeStruct((rows, PART), I32)],
        name="experts",
        compiler_params=_params("arbitrary"),
    )(tile_expert, tiles_used, xa, xb, wg, wu, wd)


def _finish_kernel(x_ref, xb_ref, p_ref, ga_ref, gb_ref, wtok_ref, wgs_ref, wus_ref, wds_ref, wpg_ref, bpg_ref, wp_ref,
                   lng_ref, lnb_ref, *out_refs):
    xb = xb_ref[...]
    hs = _silu(jnp.dot(xb, wgs_ref[...], preferred_element_type=F32)) * jnp.dot(
        xb, wus_ref[...], preferred_element_type=F32)
    shared = jnp.dot(hs.astype(BF16), wds_ref[...], preferred_element_type=F32)
    gate = jax.nn.sigmoid(jnp.dot(xb, wpg_ref[...], preferred_element_type=F32) + bpg_ref[...])
    ple = gate * jnp.dot(p_ref[0].astype(BF16), wp_ref[...], preferred_element_type=F32)

    wtok = wtok_ref[...]
    r_a = None
    r_b = None
    for k in range(TOP_K):
        wk = wtok[:, k:k + 1]
        a = wk * _unpack_pairs(ga_ref[k])
        b = wk * _unpack_pairs(gb_ref[k])
        r_a = a if r_a is None else r_a + a
        r_b = b if r_b is None else r_b + b
    routed = jnp.concatenate([r_a, r_b], axis=1)

    xn = _layer_norm(ALPHA * x_ref[...] + (routed + shared + ple), lng_ref[...], lnb_ref[...])
    if len(out_refs) == 2 and out_refs[1].dtype == BF16:
        out_refs[0][...] = xn
        out_refs[1][...] = xn.astype(BF16)
    else:
        out_refs[-1][...] = xn


def _finish(x, xb, p_all, layer, stream, ga, gb, wtok, wgs, wus, wds, wpg, bpg, wp, ln_g, ln_b,
            out_base=None, total_rows=None, tm=512):
    n = x.shape[0]
    p_block = stream * (n // tm)
    row = lambda i: (i, 0)
    const = lambda i: (0, 0)
    if total_rows is None:
        out_specs = [pl.BlockSpec((tm, D_MODEL), row), pl.BlockSpec((tm, D_MODEL), row)]
        out_shape = [jax.ShapeDtypeStruct((n, D_MODEL), F32), jax.ShapeDtypeStruct((n, D_MODEL), BF16)]
    else:
        out_specs = [pl.BlockSpec((tm, D_MODEL), lambda i: (p_block + i, 0))]
        out_shape = [jax.ShapeDtypeStruct((total_rows, D_MODEL), F32)]
    extra_specs = [] if out_base is None else [pl.BlockSpec(memory_space=pl.ANY)]
    extra_args = [] if out_base is None else [out_base]
    aliases = {} if out_base is None else {14: 0}
    return pl.pallas_call(
        _finish_kernel,
        grid=(n // tm,),
        input_output_aliases=aliases,
        in_specs=[
            pl.BlockSpec((tm, D_MODEL), row),
            pl.BlockSpec((tm, D_MODEL), row),
            pl.BlockSpec((1, tm, P_DIM), lambda i: (layer, p_block + i, 0)),
            pl.BlockSpec((TOP_K, tm, PART), lambda i: (0, i, 0)),
            pl.BlockSpec((TOP_K, tm, PART), lambda i: (0, i, 0)),
            pl.BlockSpec((tm, LANES), row),
            pl.BlockSpec((D_MODEL, D_SHARED), const),
            pl.BlockSpec((D_MODEL, D_SHARED), const),
            pl.BlockSpec((D_SHARED, D_MODEL), const),
            pl.BlockSpec((D_MODEL, D_MODEL), const),
            pl.BlockSpec((1, D_MODEL), const),
            pl.BlockSpec((P_DIM, D_MODEL), const),
            pl.BlockSpec((1, D_MODEL), const),
            pl.BlockSpec((1, D_MODEL), const),
        ] + extra_specs,
        out_specs=out_specs,
        out_shape=out_shape,
        name="moe_finish_ln",
        compiler_params=_params("parallel"),
    )(x, xb, p_all, ga, gb, wtok, wgs, wus, wds, wpg, bpg, wp, ln_g, ln_b, *extra_args)


def _route(x, xpa, xpb, w_router_t, b_router):
    n = x.shape[0]
    pairs = n * TOP_K
    tiles = pairs // ROW_TILE + N_EXPERTS
    eidx, rank, wtok, counts = _router(x, w_router_t, b_router)

    cnt = counts[:, 0].astype(I32)
    group_tiles = (cnt + ROW_TILE - 1) // ROW_TILE
    tile_end = jnp.cumsum(group_tiles)
    starts = (tile_end - group_tiles) * ROW_TILE
    tile_id = jnp.arange(tiles, dtype=I32)
    tile_expert = jnp.minimum(jnp.sum((tile_end[None, :] <= tile_id[:, None]).astype(I32), axis=1), N_EXPERTS - 1)

    slots_flat = _slots(starts, eidx, rank).reshape(1, pairs)
    xsa, xsb = _sc_dispatch([xpa, xpb], slots_flat, tiles * ROW_TILE)
    return dict(xsa=xsa, xsb=xsb, slots_flat=slots_flat, wtok=wtok, tile_expert=tile_expert,
                tiles_used=tile_end[N_EXPERTS - 1:])


class _IssueOrder:
    def __init__(self):
        self.last = None

    def before(self, operand):
        if self.last is None:
            return operand
        self.last, operand = lax.optimization_barrier((self.last, operand))
        return operand

    def after(self, result):
        self.last = result


def kernel(x, p, w_in, b_in, sg_ln_g, sg_ln_b, w_s, b_s, mh_g, w_pa, w_pb, w_out, ln1_g, ln1_b, w_router, b_router, w_gate_e, w_up_e, w_down_e, w_gate_s, w_up_s, w_down_s, w_pg, b_pg, w_p, ln2_g, ln2_b):
    batch, seq, _ = x.shape
    streams = STREAMS if batch % STREAMS == 0 else 1
    sb = batch // streams
    n = sb * seq
    xf = [x[s * sb:(s + 1) * sb].reshape(n, D_MODEL) for s in range(streams)]
    xb = [xs.astype(BF16) for xs in xf]
    p_all = p.reshape(DEPTH, batch * seq, P_DIM)

    wg_all = w_gate_e.reshape(DEPTH * N_EXPERTS, D_MODEL, D_EXPERT)
    wu_all = w_up_e.reshape(DEPTH * N_EXPERTS, D_MODEL, D_EXPERT)
    wd_all = w_down_e.reshape(DEPTH * N_EXPERTS, D_EXPERT, D_MODEL)

    c_uv = 2 * G_WIDTH
    c_qkv = c_uv + 2 * M_QK + M_V
    c_o = c_qkv + M_V
    c_if = c_o + 2 * M_HEADS
    c_gb = c_if + D_MODEL

    def layer_weights(l):
        w = w_in[l]
        b = b_in[l][None, :]
        return dict(
            sgu=(w[:, :G_WIDTH].astype(BF16), b[:, :G_WIDTH], w[:, G_WIDTH:c_uv].astype(BF16), b[:, G_WIDTH:c_uv],
                 sg_ln_g[l][None, :], sg_ln_b[l][None, :], w_s[l], b_s[l].T),
            mproj=(w[:, c_uv:c_qkv].astype(BF16), b[:, c_uv:c_qkv], w[:, c_qkv:c_o].astype(BF16), b[:, c_qkv:c_o],
                   jnp.pad(w[:, c_o:c_if], ((0, 0), (0, LANES - 2 * M_HEADS))).astype(BF16),
                   jnp.pad(b[:, c_o:c_if], ((0, 0), (0, LANES - 2 * M_HEADS)))),
            mhg=mh_g[l][None, :],
            merge=(w[:, c_if:c_gb].astype(BF16), b[:, c_if:c_gb], w[:, c_gb:].astype(BF16), b[:, c_gb:],
                   w_pa[l].astype(BF16), w_pb[l].astype(BF16), w_out[l].astype(BF16),
                   ln1_g[l][None, :], ln1_b[l][None, :]),
            router=(w_router[l].T, b_router[l][:, None]),
            finish=(w_gate_s[l].astype(BF16), w_up_s[l].astype(BF16), w_down_s[l].astype(BF16),
                    w_pg[l].astype(BF16), b_pg[l][None, :], w_p[l].astype(BF16),
                    ln2_g[l][None, :], ln2_b[l][None, :]))

    weights = [layer_weights(l) for l in range(DEPTH)]
    order = _IssueOrder()
    st = [dict() for _ in range(streams)]
    result = [None]

    def proj(s, l):
        st[s]["qkv"], st[s]["osig"], gif = _mproj(order.before(xb[s]), *weights[l]["mproj"])
        st[s]["gif"] = gif
        st[s]["gif_rows"] = gif[:, :2 * M_HEADS].reshape(sb, seq, 2 * M_HEADS).transpose(0, 2, 1)
        order.after(gif)
        st[s]["a"] = _sgu(order.before(xb[s]), *weights[l]["sgu"])
        order.after(st[s]["a"])

    def recur(s, l):
        st[s]["mb"] = _mlstm(order.before(st[s]["qkv"]), st[s]["gif"], st[s]["gif_rows"], st[s]["osig"],
                             weights[l]["mhg"], sb, seq)
        order.after(st[s]["mb"])

    def merge(s, l):
        xf[s], xb[s], xpa, xpb = _merge(order.before(xf[s]), xb[s], st[s]["a"], st[s]["mb"], *weights[l]["merge"])
        order.after(xpb)
        st[s]["route"] = _route(order.before(xf[s]), xpa, xpb, *weights[l]["router"])
        order.after(st[s]["route"]["wtok"])

    def experts(s, l):
        r = st[s]["route"]
        ysa, ysb = _experts(r["tile_expert"], r["tiles_used"], order.before(r["xsa"]), r["xsb"], l,
                            wg_all, wu_all, wd_all)
        order.after(ysb)
        ga, gb = _sc_gather([ysa, ysb], r["slots_flat"])
        st[s]["gathered"] = (ga.reshape(TOP_K, n, PART), gb.reshape(TOP_K, n, PART))

    def finish(s, l):
        last = l == DEPTH - 1
        outs = _finish(order.before(xf[s]), xb[s], p_all, l, s, *st[s]["gathered"], st[s]["route"]["wtok"],
                       *weights[l]["finish"], out_base=result[0] if last else None,
                       total_rows=batch * seq if last else None)
        if last:
            result[0], = outs
        else:
            xf[s], xb[s] = outs
        order.after(outs[0])

    if streams == 1:
        for l in range(DEPTH):
            proj(0, l), recur(0, l), merge(0, l), experts(0, l), finish(0, l)
    else:
        proj(0, 0), recur(0, 0), merge(0, 0), proj(1, 0)
        for l in range(DEPTH):
            more = l + 1 < DEPTH
            experts(0, l), recur(1, l), finish(0, l), merge(1, l)
            if more:
                proj(0, l + 1)
            experts(1, l)
            if more:
                recur(0, l + 1)
            finish(1, l)
            if more:
                merge(0, l + 1), proj(1, l + 1)
    return result[0].reshape(batch, seq, D_MODEL)
```

```python
import functools

import jax
import jax.numpy as jnp
from jax import lax
from jax.experimental import pallas as pl
from jax.experimental.pallas import tpu as pltpu
from jax.experimental.pallas import tpu_sc as plsc

D_MODEL = 1024
DEPTH = 4
CHUNK = 64
P_DIM = 256
G_WIDTH = 1024
G_GROUPS = 8
G_GROUP_DIM = G_WIDTH // G_GROUPS
G_BLOCK = 128
M_HEADS = 4
M_QK_DIM = 128
M_V_DIM = 256
M_QK = M_HEADS * M_QK_DIM
M_V = M_HEADS * M_V_DIM
N_EXPERTS = 64
TOP_K = 8
N_GROUPS = 8
TOPK_GROUPS = 4
GROUP_SIZE = N_EXPERTS // N_GROUPS
D_EXPERT = 256
D_SHARED = 256
ROUTE_SCALE = 2.5
ALPHA = (2 * DEPTH) ** 0.25
LN_EPS = 1e-5
RMS_EPS = 1e-6

LANES = 128
VMEM_LIMIT = 56 * 1024 * 1024
HALF = D_MODEL // 2
PART = HALF // 2
ROW_TILE = 1024
STREAMS = 2
MLSTM_GROUP = 8
SC_WINDOW = 128

F32 = jnp.float32
BF16 = jnp.bfloat16
I32 = jnp.int32


def _params(*semantics):
    return pltpu.CompilerParams(dimension_semantics=semantics, vmem_limit_bytes=VMEM_LIMIT)


def _layer_norm(x, g, b):
    mu = jnp.mean(x, axis=-1, keepdims=True)
    xc = x - mu
    var = jnp.mean(xc * xc, axis=-1, keepdims=True)
    return xc * lax.rsqrt(var + LN_EPS) * g + b


def _gelu(x):
    return 0.5 * x * (1.0 + lax.erf(x * (2.0 ** -0.5)))


def _silu(x):
    return x * jax.nn.sigmoid(x)


def _pack_pairs(x):
    lo = lax.bitcast_convert_type(x[:, :PART].astype(BF16).astype(F32), I32)
    hi = lax.bitcast_convert_type(x[:, PART:].astype(BF16).astype(F32), I32)
    return lax.shift_right_logical(lo, 16) | (hi & jnp.int32(-65536))


def _pack_row(x):
    return _pack_pairs(x[:, :HALF]), _pack_pairs(x[:, HALF:])


def _unpack_pairs(w):
    lo = lax.bitcast_convert_type(lax.shift_left(w, 16), F32)
    hi = lax.bitcast_convert_type(w & jnp.int32(-65536), F32)
    return jnp.concatenate([lo, hi], axis=1)


def _mproj_kernel(x_ref, wqkv_ref, bqkv_ref, wo_ref, bo_ref, wif_ref, bif_ref, qkv_ref, osig_ref, gif_ref):
    x = x_ref[...]
    qkv_ref[...] = (jnp.dot(x, wqkv_ref[...], preferred_element_type=F32) + bqkv_ref[...]).astype(qkv_ref.dtype)
    osig_ref[...] = jax.nn.sigmoid(
        jnp.dot(x, wo_ref[...], preferred_element_type=F32) + bo_ref[...]).astype(osig_ref.dtype)
    gif_ref[...] = jnp.dot(x, wif_ref[...], preferred_element_type=F32) + bif_ref[...]


def _mproj(x, w_qkv, b_qkv, w_o, b_o, w_if, b_if, tm=512):
    n, k = x.shape
    row = lambda i: (i, 0)
    const = lambda i: (0, 0)
    widths = (w_qkv.shape[1], w_o.shape[1], w_if.shape[1])
    in_specs = [pl.BlockSpec((tm, k), row)]
    for width in widths:
        in_specs += [pl.BlockSpec((k, width), const), pl.BlockSpec((1, width), const)]
    return pl.pallas_call(
        _mproj_kernel,
        grid=(n // tm,),
        in_specs=in_specs,
        out_specs=[pl.BlockSpec((tm, width), row) for width in widths],
        out_shape=[jax.ShapeDtypeStruct((n, widths[0]), BF16), jax.ShapeDtypeStruct((n, widths[1]), BF16),
                   jax.ShapeDtypeStruct((n, widths[2]), F32)],
        name="mlstm_proj",
        compiler_params=_params("parallel"),
    )(x, w_qkv, b_qkv, w_o, b_o, w_if, b_if)


def _sgu_kernel(x_ref, wu_ref, bu_ref, wv_ref, bv_ref, lng_ref, lnb_ref, ws_ref, bs_ref, a_ref, *, blocks):
    x = x_ref[...]
    v = _gelu(jnp.dot(x, wv_ref[...], preferred_element_type=F32) + bv_ref[...])
    vln = _layer_norm(v, lng_ref[...], lnb_ref[...]).astype(BF16)
    u = _gelu(jnp.dot(x, wu_ref[...], preferred_element_type=F32) + bu_ref[...])
    t_chunk = lax.broadcasted_iota(jnp.int32, (G_BLOCK, G_BLOCK), 0) // CHUNK
    s_chunk = lax.broadcasted_iota(jnp.int32, (G_BLOCK, G_BLOCK), 1) // CHUNK
    causal = s_chunk <= t_chunk
    for g in range(G_GROUPS):
        w = jnp.where(causal, ws_ref[g], 0.0).astype(BF16)
        bias = bs_ref[:, g:g + 1]
        cs = slice(g * G_GROUP_DIM, (g + 1) * G_GROUP_DIM)
        for blk in range(blocks):
            rs = slice(blk * G_BLOCK, (blk + 1) * G_BLOCK)
            mixed = jnp.dot(w, vln[rs, cs], preferred_element_type=F32) + bias
            a_ref[rs, cs] = (u[rs, cs] * mixed).astype(a_ref.dtype)


def _sgu(x, w_u, b_u, w_v, b_v, ln_g, ln_b, w_s, b_s_t, blocks=4):
    n = x.shape[0]
    tp = blocks * G_BLOCK
    const = lambda i: (0, 0)
    wspec = pl.BlockSpec((D_MODEL, G_WIDTH), const)
    vspec = pl.BlockSpec((1, G_WIDTH), const)
    return pl.pallas_call(
        functools.partial(_sgu_kernel, blocks=blocks),
        grid=(n // tp,),
        in_specs=[
            pl.BlockSpec((tp, D_MODEL), lambda i: (i, 0)),
            wspec, vspec, wspec, vspec, vspec, vspec,
            pl.BlockSpec((G_GROUPS, G_BLOCK, G_BLOCK), lambda i: (0, 0, 0)),
            pl.BlockSpec((G_BLOCK, G_GROUPS), const),
        ],
        out_specs=pl.BlockSpec((tp, G_WIDTH), lambda i: (i, 0)),
        out_shape=jax.ShapeDtypeStruct((n, G_WIDTH), BF16),
        name="spatial_gating",
        compiler_params=_params("parallel"),
    )(x, w_u, b_u, w_v, b_v, ln_g, ln_b, w_s, b_s_t)


def _segment_cumsum(x, axis, seg):
    pos = lax.broadcasted_iota(jnp.int32, x.shape, axis) % seg
    shift = 1
    while shift < seg:
        x = x + jnp.where(pos >= shift, pltpu.roll(x, shift, axis), 0.0)
        shift *= 2
    return x


def _mlstm_kernel(qkv_ref, gc_ref, gr_ref, osig_ref, mhg_ref, mb_ref, c_ref, n_ref, m_ref, *, chunks):
    @pl.when(pl.program_id(1) == 0)
    def _():
        c_ref[...] = jnp.zeros_like(c_ref)
        n_ref[...] = jnp.zeros_like(n_ref)
        m_ref[...] = jnp.zeros_like(m_ref)

    scale = M_QK_DIM ** -0.5
    gc = gc_ref[...]
    gr = gr_ref[0]
    b_cols = _segment_cumsum(jax.nn.log_sigmoid(gc), 0, CHUNK)
    b_rows = _segment_cumsum(jax.nn.log_sigmoid(gr), 1, CHUNK)
    t_idx = lax.broadcasted_iota(jnp.int32, (CHUNK, CHUNK), 0)
    s_idx = lax.broadcasted_iota(jnp.int32, (CHUNK, CHUNK), 1)
    tri = s_idx <= t_idx

    heads = range(M_HEADS)
    rows = lambda c: slice(c * CHUNK, (c + 1) * CHUNK)

    def local_terms(cs):
        units = [(c, h) for c in cs for h in heads]
        q = {u: qkv_ref[rows(u[0]), u[1] * M_QK_DIM:(u[1] + 1) * M_QK_DIM] for u in units}
        k = {u: qkv_ref[rows(u[0]), M_QK + u[1] * M_QK_DIM:M_QK + (u[1] + 1) * M_QK_DIM] for u in units}
        v = {u: qkv_ref[rows(u[0]), 2 * M_QK + u[1] * M_V_DIM:2 * M_QK + (u[1] + 1) * M_V_DIM] for u in units}
        i_col = {u: gc[rows(u[0]), u[1]:u[1] + 1] for u in units}
        b_col = {u: b_cols[rows(u[0]), M_HEADS + u[1]:M_HEADS + u[1] + 1] for u in units}
        ib_row = {u: (b_rows[M_HEADS + u[1]:M_HEADS + u[1] + 1, rows(u[0])]
                      - gr[u[1]:u[1] + 1, rows(u[0])]) for u in units}
        d = {u: jnp.where(tri, b_col[u] - ib_row[u], -jnp.inf) for u in units}
        d_max = {u: jnp.max(d[u], axis=1, keepdims=True) for u in units}
        qk = {u: lax.dot_general(q[u], k[u], (((1,), (1,)), ((), ())), preferred_element_type=F32) for u in units}
        s_loc = {u: qk[u] * scale * jnp.exp(d[u] - d_max[u]) for u in units}
        a_loc = {u: jnp.dot(s_loc[u].astype(BF16), v[u], preferred_element_type=F32) for u in units}
        r_loc = {u: jnp.sum(s_loc[u], axis=1, keepdims=True) for u in units}
        b_last = {u: b_col[u][CHUNK - 1:CHUNK, :] for u in units}
        g_col = {u: b_last[u] - b_col[u] + i_col[u] for u in units}
        g_max = {u: jnp.max(g_col[u], axis=0, keepdims=True) for u in units}
        wk = {u: jnp.exp(g_col[u] - g_max[u]) * k[u].astype(F32) for u in units}
        u_loc = {u: jnp.dot(wk[u].T.astype(BF16), v[u], preferred_element_type=F32) for u in units}
        nk_loc = {u: jnp.sum(wk[u], axis=0, keepdims=True) for u in units}
        return dict(q=q, b_col=b_col, d_max=d_max, a_loc=a_loc, r_loc=r_loc, b_last=b_last, g_max=g_max,
                    u_loc=u_loc, nk_loc=nk_loc)

    def carried_step(c, loc, c_state, n_state, m_state):
        inter = [loc["b_col"][(c, h)] + m_state[h] for h in heads]
        m_t = [jnp.maximum(inter[h], loc["d_max"][(c, h)]) for h in heads]
        w_intra = [jnp.exp(loc["d_max"][(c, h)] - m_t[h]) for h in heads]
        w_inter = [jnp.exp(inter[h] - m_t[h]) for h in heads]
        qc = [jnp.dot(loc["q"][(c, h)], c_state[h].astype(BF16), preferred_element_type=F32) * scale for h in heads]
        qn = [jnp.sum(loc["q"][(c, h)].astype(F32) * n_state[h], axis=1, keepdims=True) * scale for h in heads]
        m_new = [jnp.maximum(loc["b_last"][(c, h)] + m_state[h], loc["g_max"][(c, h)]) for h in heads]
        decay = [jnp.exp(loc["b_last"][(c, h)] + m_state[h] - m_new[h]) for h in heads]
        beta = [jnp.exp(loc["g_max"][(c, h)] - m_new[h]) for h in heads]
        c_next = [decay[h] * c_state[h] + beta[h] * loc["u_loc"][(c, h)] for h in heads]
        n_next = [decay[h] * n_state[h] + beta[h] * loc["nk_loc"][(c, h)] for h in heads]
        num = [w_intra[h] * loc["a_loc"][(c, h)] + w_inter[h] * qc[h] for h in heads]
        den = [w_intra[h] * loc["r_loc"][(c, h)] + w_inter[h] * qn[h] for h in heads]
        hv = [num[h] * (1.0 / jnp.maximum(jnp.abs(den[h]), jnp.exp(-m_t[h]))) for h in heads]
        hv = [hv[h] * lax.rsqrt(jnp.mean(hv[h] * hv[h], axis=1, keepdims=True) + RMS_EPS) for h in heads]
        for h in heads:
            vs = slice(h * M_V_DIM, (h + 1) * M_V_DIM)
            mb_ref[rows(c), vs] = (osig_ref[rows(c), vs].astype(F32) * (hv[h] * mhg_ref[:, vs])).astype(mb_ref.dtype)
        return c_next, n_next, m_new

    c_state = [c_ref[h] for h in heads]
    n_state = [n_ref[h] for h in heads]
    m_state = [m_ref[h][:, 0:1] for h in heads]
    groups = [list(range(g, min(g + MLSTM_GROUP, chunks))) for g in range(0, chunks, MLSTM_GROUP)]
    loc = local_terms(groups[0])
    for gi, group in enumerate(groups):
        loc_next = local_terms(groups[gi + 1]) if gi + 1 < len(groups) else None
        for c in group:
            c_state, n_state, m_state = carried_step(c, loc, c_state, n_state, m_state)
        loc = loc_next

    for h in heads:
        c_ref[h], n_ref[h] = c_state[h], n_state[h]
        m_ref[h] = jnp.broadcast_to(m_state[h], (1, LANES))


def _mlstm(qkv, gates_col, gates_row, osig_src, mh_g, batch, seq, chunks=8):
    ts = chunks * CHUNK
    tiles = seq // ts
    n = batch * seq
    return pl.pallas_call(
        functools.partial(_mlstm_kernel, chunks=chunks),
        grid=(batch, tiles),
        in_specs=[
            pl.BlockSpec((ts, 2 * M_QK + M_V), lambda b, t: (b * tiles + t, 0)),
            pl.BlockSpec((ts, LANES), lambda b, t: (b * tiles + t, 0)),
            pl.BlockSpec((1, 2 * M_HEADS, ts), lambda b, t: (b, 0, t)),
            pl.BlockSpec((ts, M_V), lambda b, t: (b * tiles + t, 0)),
            pl.BlockSpec((1, M_V), lambda b, t: (0, 0)),
        ],
        out_specs=pl.BlockSpec((ts, M_V), lambda b, t: (b * tiles + t, 0)),
        out_shape=jax.ShapeDtypeStruct((n, M_V), BF16),
        scratch_shapes=[
            pltpu.VMEM((M_HEADS, M_QK_DIM, M_V_DIM), F32),
            pltpu.VMEM((M_HEADS, 1, M_QK_DIM), F32),
            pltpu.VMEM((M_HEADS, 1, LANES), F32),
        ],
        name="mlstm",
        compiler_params=_params("parallel", "arbitrary"),
    )(qkv, gates_col, gates_row, osig_src, mh_g)


def _merge_kernel(x_ref, xin_ref, a_ref, mb_ref, wga_ref, bga_ref, wgb_ref, bgb_ref, wpa_ref, wpb_ref, wout_ref,
                  g_ref, b_ref, xo_ref, xob_ref, xpa_ref, xpb_ref):
    xin = xin_ref[...]
    ga = jax.nn.sigmoid(jnp.dot(xin, wga_ref[...], preferred_element_type=F32) + bga_ref[...])
    gb = jax.nn.sigmoid(jnp.dot(xin, wgb_ref[...], preferred_element_type=F32) + bgb_ref[...])
    ya = jnp.dot(a_ref[...], wpa_ref[...], preferred_element_type=F32)
    yb = jnp.dot(mb_ref[...], wpb_ref[...], preferred_element_type=F32)
    y = ga * ya + gb * yb
    mix = jnp.dot(y.astype(BF16), wout_ref[...], preferred_element_type=F32)
    xn = _layer_norm(ALPHA * x_ref[...] + mix, g_ref[...], b_ref[...])
    xo_ref[...] = xn
    xob_ref[...] = xn.astype(BF16)
    xpa_ref[...], xpb_ref[...] = _pack_row(xn)


def _merge(x, xin, a, mb, w_ga, b_ga, w_gb, b_gb, w_pa, w_pb, w_out, ln_g, ln_b, tm=512):
    n = x.shape[0]
    row = lambda i: (i, 0)
    const = lambda i: (0, 0)
    wspec = pl.BlockSpec((D_MODEL, D_MODEL), const)
    vspec = pl.BlockSpec((1, D_MODEL), const)
    return pl.pallas_call(
        _merge_kernel,
        grid=(n // tm,),
        in_specs=[
            pl.BlockSpec((tm, D_MODEL), row),
            pl.BlockSpec((tm, D_MODEL), row),
            pl.BlockSpec((tm, G_WIDTH), row),
            pl.BlockSpec((tm, M_V), row),
            wspec, vspec, wspec, vspec, wspec, wspec, wspec, vspec, vspec,
        ],
        out_specs=[pl.BlockSpec((tm, D_MODEL), row), pl.BlockSpec((tm, D_MODEL), row),
                   pl.BlockSpec((tm, PART), row), pl.BlockSpec((tm, PART), row)],
        out_shape=[jax.ShapeDtypeStruct((n, D_MODEL), F32), jax.ShapeDtypeStruct((n, D_MODEL), BF16),
                   jax.ShapeDtypeStruct((n, PART), I32), jax.ShapeDtypeStruct((n, PART), I32)],
        name="merge_ln",
        compiler_params=_params("parallel"),
    )(x, xin, a, mb, w_ga, b_ga, w_gb, b_gb, w_pa, w_pb, w_out, ln_g, ln_b)


def _first_max(v, idx, axes, sentinel):
    m = jnp.max(v, axis=axes, keepdims=True)
    first = jnp.min(jnp.where(v == m, idx, sentinel), axis=axes, keepdims=True)
    return m, first


def _router_kernel(x_ref, wr_ref, br_ref, upper_ref, eidx_ref, rank_ref, wtok_ref, cnt_ref, run_ref, *, tm):
    @pl.when(pl.program_id(0) == 0)
    def _():
        run_ref[...] = jnp.zeros_like(run_ref)

    def split(v):
        hi = v.astype(BF16)
        return hi, (v - hi.astype(F32)).astype(BF16)

    nt = lambda a, b: lax.dot_general(a, b, (((1,), (1,)), ((), ())), preferred_element_type=F32)
    w_hi, w_lo = split(wr_ref[...])
    x_hi, x_lo = split(x_ref[...])
    logits = nt(w_hi, x_hi) + (nt(w_hi, x_lo) + nt(w_lo, x_hi))
    scores = jax.nn.sigmoid(logits)
    sel = (scores + br_ref[...]).reshape(N_GROUPS, GROUP_SIZE, tm)
    scores3 = scores.reshape(N_GROUPS, GROUP_SIZE, tm)
    member = lax.broadcasted_iota(jnp.int32, sel.shape, 1)
    group = lax.broadcasted_iota(jnp.int32, sel.shape, 0)
    neg = -jnp.inf

    m1, f1 = _first_max(sel, member, 1, GROUP_SIZE)
    m2 = jnp.max(jnp.where(member == f1, neg, sel), axis=1, keepdims=True)
    gscore = m1 + m2
    gid = lax.broadcasted_iota(jnp.int32, gscore.shape, 0)
    gmask = jnp.zeros(gscore.shape, dtype=jnp.bool_)
    for _ in range(TOPK_GROUPS):
        _, fg = _first_max(gscore, gid, 0, N_GROUPS)
        hit = gid == fg
        gmask = gmask | hit
        gscore = jnp.where(hit, neg, gscore)

    cand = jnp.where(gmask, sel, neg)
    eid = group * GROUP_SIZE + member
    chosen = jnp.zeros(sel.shape, dtype=jnp.bool_)
    picks = []
    for _ in range(TOP_K):
        _, fe = _first_max(cand, eid, (0, 1), N_EXPERTS)
        hit = eid == fe
        chosen = chosen | hit
        cand = jnp.where(hit, neg, cand)
        picks.append((fe[0], hit, jnp.sum(jnp.where(hit, scores3, 0.0), axis=(0, 1), keepdims=True)[0]))

    chosen2 = jnp.where(chosen, 1.0, 0.0).reshape(N_EXPERTS, tm)
    prefix = jnp.dot(chosen2.astype(BF16), upper_ref[...], preferred_element_type=F32) + run_ref[:, 0:1]
    prefix3 = prefix.reshape(N_GROUPS, GROUP_SIZE, tm)

    total = picks[0][2]
    for _, _, wk in picks[1:]:
        total = total + wk
    eidx_ref[...] = jnp.concatenate([fe for fe, _, _ in picks], axis=0)
    rank_ref[...] = jnp.concatenate(
        [jnp.sum(jnp.where(hit, prefix3, 0.0), axis=(0, 1), keepdims=True)[0] for _, hit, _ in picks],
        axis=0).astype(I32)
    w_rows = jnp.concatenate([wk / total * ROUTE_SCALE for _, _, wk in picks]
                             + [jnp.zeros((LANES - TOP_K, tm), F32)], axis=0)
    wtok_ref[...] = w_rows.T

    run = run_ref[...] + jnp.sum(chosen2, axis=1, keepdims=True)
    run_ref[...] = run
    cnt_ref[...] = run


def _router(x, wr_t, br, tm=512):
    n = x.shape[0]
    tok = lambda i: (0, i)
    upper = (jnp.arange(tm)[:, None] < jnp.arange(tm)[None, :]).astype(BF16)
    return pl.pallas_call(
        functools.partial(_router_kernel, tm=tm),
        grid=(n // tm,),
        in_specs=[
            pl.BlockSpec((tm, D_MODEL), lambda i: (i, 0)),
            pl.BlockSpec((N_EXPERTS, D_MODEL), lambda i: (0, 0)),
            pl.BlockSpec((N_EXPERTS, 1), lambda i: (0, 0)),
            pl.BlockSpec((tm, tm), lambda i: (0, 0)),
        ],
        out_specs=[
            pl.BlockSpec((TOP_K, tm), tok),
            pl.BlockSpec((TOP_K, tm), tok),
            pl.BlockSpec((tm, LANES), lambda i: (i, 0)),
            pl.BlockSpec((N_EXPERTS, LANES), lambda i: (0, 0)),
        ],
        out_shape=[
            jax.ShapeDtypeStruct((TOP_K, n), I32),
            jax.ShapeDtypeStruct((TOP_K, n), I32),
            jax.ShapeDtypeStruct((n, LANES), F32),
            jax.ShapeDtypeStruct((N_EXPERTS, LANES), F32),
        ],
        scratch_shapes=[pltpu.VMEM((N_EXPERTS, LANES), F32)],
        name="router",
        compiler_params=_params("arbitrary"),
    )(x, wr_t, br, upper)


def _slots_kernel(starts_ref, eidx_ref, rank_ref, slot_ref):
    eidx = eidx_ref[...]
    slot = rank_ref[...]
    for e in range(N_EXPERTS):
        slot = slot + jnp.where(eidx == e, starts_ref[e], 0)
    slot_ref[...] = slot


def _slots(starts, eidx, rank, tn=2048):
    n = eidx.shape[1]
    tn = min(tn, n)
    spec = pl.BlockSpec((TOP_K, tn), lambda i, s: (0, i))
    return pl.pallas_call(
        _slots_kernel,
        grid_spec=pltpu.PrefetchScalarGridSpec(
            num_scalar_prefetch=1, grid=(n // tn,), in_specs=[spec, spec], out_specs=spec),
        out_shape=jax.ShapeDtypeStruct((TOP_K, n), I32),
        name="slots",
        compiler_params=_params("parallel"),
    )(starts, eidx, rank)


def _sc_mesh():
    return plsc.VectorSubcoreMesh(core_axis_name="c", subcore_axis_name="s")


def _sc_dispatch(parts, slots_flat, rows):
    n, width = parts[0].shape
    blocks = n // SC_WINDOW
    out_type = [jax.ShapeDtypeStruct((rows, width), part.dtype) for part in parts]

    @functools.partial(pl.kernel, out_type=out_type, mesh=_sc_mesh(), scratch_types=[], name="sc_dispatch")
    def run(*refs):
        i_hbm = refs[len(parts)]
        for x_hbm, o_hbm in zip(refs[:len(parts)], refs[len(parts) + 1:]):
            def body(x_vmem, i_vmem, o_hbm=o_hbm):
                pltpu.sync_copy(x_vmem, o_hbm.at[i_vmem.at[0]])

            pltpu.emit_pipeline(
                body,
                grid=(blocks, TOP_K),
                in_specs=[pl.BlockSpec((SC_WINDOW, width), lambda i, k: (i, 0)),
                          pl.BlockSpec((1, SC_WINDOW), lambda i, k: (0, k * blocks + i))],
                out_specs=[],
                core_axis_name=("c", "s"),
                dimension_semantics=(pltpu.PARALLEL, pltpu.ARBITRARY),
                trace_scopes=False,
            )(x_hbm, i_hbm)

    return run(*parts, slots_flat)


def _sc_gather(tables, idx_flat):
    count = idx_flat.shape[1]
    width = tables[0].shape[1]
    out_type = [jax.ShapeDtypeStruct((count, width), table.dtype) for table in tables]

    @functools.partial(pl.kernel, out_type=out_type, mesh=_sc_mesh(), scratch_types=[], name="sc_gather")
    def run(*refs):
        i_hbm = refs[len(tables)]
        for t_hbm, o_hbm in zip(refs[:len(tables)], refs[len(tables) + 1:]):
            def body(i_vmem, o_vmem, t_hbm=t_hbm):
                pltpu.sync_copy(t_hbm.at[i_vmem.at[0]], o_vmem)

            pltpu.emit_pipeline(
                body,
                grid=(count // SC_WINDOW,),
                in_specs=[pl.BlockSpec((1, SC_WINDOW), lambda i: (0, i))],
                out_specs=[pl.BlockSpec((SC_WINDOW, width), lambda i: (i, 0))],
                core_axis_name=("c", "s"),
                dimension_semantics=(pltpu.PARALLEL,),
                trace_scopes=False,
            )(i_hbm, o_hbm)

    return run(*tables, idx_flat)


def _expert_kernel(te_ref, used_ref, xa_ref, xb_ref, wg_ref, wu_ref, wd_ref, oa_ref, ob_ref, wgb_ref, wub_ref, wdb_ref):
    j = pl.program_id(0)

    @pl.when((j == 0) | (te_ref[j] != te_ref[jnp.maximum(j - 1, 0)]))
    def _():
        wgb_ref[...] = wg_ref[0].astype(BF16)
        wub_ref[...] = wu_ref[0].astype(BF16)
        wdb_ref[...] = wd_ref[0].astype(BF16)

    @pl.when(j < used_ref[0])
    def _():
        x = jnp.concatenate([_unpack_pairs(xa_ref[...]), _unpack_pairs(xb_ref[...])], axis=1).astype(BF16)
        hg = jnp.dot(x, wgb_ref[...], preferred_element_type=F32)
        hu = jnp.dot(x, wub_ref[...], preferred_element_type=F32)
        hid = (_silu(hg) * hu).astype(BF16)
        oa_ref[...], ob_ref[...] = _pack_row(jnp.dot(hid, wdb_ref[...], preferred_element_type=F32))


def _experts(tile_expert, tiles_used, xa, xb, layer, wg, wu, wd):
    rows = xa.shape[0]
    row = lambda j, te, used: (j, 0)
    exp = lambda j, te, used: (layer * N_EXPERTS + te[j], 0, 0)
    return pl.pallas_call(
        _expert_kernel,
        grid_spec=pltpu.PrefetchScalarGridSpec(
            num_scalar_prefetch=2,
            grid=(rows // ROW_TILE,),
            in_specs=[
                pl.BlockSpec((ROW_TILE, PART), row),
                pl.BlockSpec((ROW_TILE, PART), row),
                pl.BlockSpec((1, D_MODEL, D_EXPERT), exp),
                pl.BlockSpec((1, D_MODEL, D_EXPERT), exp),
                pl.BlockSpec((1, D_EXPERT, D_MODEL), exp),
            ],
            out_specs=[pl.BlockSpec((ROW_TILE, PART), row), pl.BlockSpec((ROW_TILE, PART), row)],
            scratch_shapes=[pltpu.VMEM((D_MODEL, D_EXPERT), BF16), pltpu.VMEM((D_MODEL, D_EXPERT), BF16),
                            pltpu.VMEM((D_EXPERT, D_MODEL), BF16)],
        ),
        out_shape=[jax.ShapeDtypeStruct((rows, PART), I32), jax.ShapeDtypeStruct((rows, PART), I32)],
        name="experts",
        compiler_params=_params("arbitrary"),
    )(tile_expert, tiles_used, xa, xb, wg, wu, wd)


def _finish_kernel(x_ref, xb_ref, p_ref, ga_ref, gb_ref, wtok_ref, wgs_ref, wus_ref, wds_ref, wpg_ref, bpg_ref, wp_ref,
                   lng_ref, lnb_ref, *out_refs):
    xb = xb_ref[...]
    hs = _silu(jnp.dot(xb, wgs_ref[...], preferred_element_type=F32)) * jnp.dot(
        xb, wus_ref[...], preferred_element_type=F32)
    shared = jnp.dot(hs.astype(BF16), wds_ref[...], preferred_element_type=F32)
    gate = jax.nn.sigmoid(jnp.dot(xb, wpg_ref[...], preferred_element_type=F32) + bpg_ref[...])
    ple = gate * jnp.dot(p_ref[0].astype(BF16), wp_ref[...], preferred_element_type=F32)

    wtok = wtok_ref[...]
    r_a = None
    r_b = None
    for k in range(TOP_K):
        wk = wtok[:, k:k + 1]
        a = wk * _unpack_pairs(ga_ref[k])
        b = wk * _unpack_pairs(gb_ref[k])
        r_a = a if r_a is None else r_a + a
        r_b = b if r_b is None else r_b + b
    routed = jnp.concatenate([r_a, r_b], axis=1)

    xn = _layer_norm(ALPHA * x_ref[...] + (routed + shared + ple), lng_ref[...], lnb_ref[...])
    if len(out_refs) == 2 and out_refs[1].dtype == BF16:
        out_refs[0][...] = xn
        out_refs[1][...] = xn.astype(BF16)
    else:
        out_refs[-1][...] = xn


def _finish(x, xb, p_all, layer, stream, ga, gb, wtok, wgs, wus, wds, wpg, bpg, wp, ln_g, ln_b,
            out_base=None, total_rows=None, tm=512):
    n = x.shape[0]
    p_block = stream * (n // tm)
    row = lambda i: (i, 0)
    const = lambda i: (0, 0)
    if total_rows is None:
        out_specs = [pl.BlockSpec((tm, D_MODEL), row), pl.BlockSpec((tm, D_MODEL), row)]
        out_shape = [jax.ShapeDtypeStruct((n, D_MODEL), F32), jax.ShapeDtypeStruct((n, D_MODEL), BF16)]
    else:
        out_specs = [pl.BlockSpec((tm, D_MODEL), lambda i: (p_block + i, 0))]
        out_shape = [jax.ShapeDtypeStruct((total_rows, D_MODEL), F32)]
    extra_specs = [] if out_base is None else [pl.BlockSpec(memory_space=pl.ANY)]
    extra_args = [] if out_base is None else [out_base]
    aliases = {} if out_base is None else {14: 0}
    return pl.pallas_call(
        _finish_kernel,
        grid=(n // tm,),
        input_output_aliases=aliases,
        in_specs=[
            pl.BlockSpec((tm, D_MODEL), row),
            pl.BlockSpec((tm, D_MODEL), row),
            pl.BlockSpec((1, tm, P_DIM), lambda i: (layer, p_block + i, 0)),
            pl.BlockSpec((TOP_K, tm, PART), lambda i: (0, i, 0)),
            pl.BlockSpec((TOP_K, tm, PART), lambda i: (0, i, 0)),
            pl.BlockSpec((tm, LANES), row),
            pl.BlockSpec((D_MODEL, D_SHARED), const),
            pl.BlockSpec((D_MODEL, D_SHARED), const),
            pl.BlockSpec((D_SHARED, D_MODEL), const),
            pl.BlockSpec((D_MODEL, D_MODEL), const),
            pl.BlockSpec((1, D_MODEL), const),
            pl.BlockSpec((P_DIM, D_MODEL), const),
            pl.BlockSpec((1, D_MODEL), const),
            pl.BlockSpec((1, D_MODEL), const),
        ] + extra_specs,
        out_specs=out_specs,
        out_shape=out_shape,
        name="moe_finish_ln",
        compiler_params=_params("parallel"),
    )(x, xb, p_all, ga, gb, wtok, wgs, wus, wds, wpg, bpg, wp, ln_g, ln_b, *extra_args)


def _route(x, xpa, xpb, w_router_t, b_router):
    n = x.shape[0]
    pairs = n * TOP_K
    tiles = pairs // ROW_TILE + N_EXPERTS
    eidx, rank, wtok, counts = _router(x, w_router_t, b_router)

    cnt = counts[:, 0].astype(I32)
    group_tiles = (cnt + ROW_TILE - 1) // ROW_TILE
    tile_end = jnp.cumsum(group_tiles)
    starts = (tile_end - group_tiles) * ROW_TILE
    tile_id = jnp.arange(tiles, dtype=I32)
    tile_expert = jnp.minimum(jnp.sum((tile_end[None, :] <= tile_id[:, None]).astype(I32), axis=1), N_EXPERTS - 1)

    slots_flat = _slots(starts, eidx, rank).reshape(1, pairs)
    xsa, xsb = _sc_dispatch([xpa, xpb], slots_flat, tiles * ROW_TILE)
    return dict(xsa=xsa, xsb=xsb, slots_flat=slots_flat, wtok=wtok, tile_expert=tile_expert,
                tiles_used=tile_end[N_EXPERTS - 1:])


class _IssueOrder:
    def __init__(self):
        self.last = None

    def before(self, operand):
        if self.last is None:
            return operand
        self.last, operand = lax.optimization_barrier((self.last, operand))
        return operand

    def after(self, result):
        self.last = result


def kernel(x, p, w_in, b_in, sg_ln_g, sg_ln_b, w_s, b_s, mh_g, w_pa, w_pb, w_out, ln1_g, ln1_b, w_router, b_router, w_gate_e, w_up_e, w_down_e, w_gate_s, w_up_s, w_down_s, w_pg, b_pg, w_p, ln2_g, ln2_b):
    batch, seq, _ = x.shape
    streams = STREAMS if batch % STREAMS == 0 else 1
    sb = batch // streams
    n = sb * seq
    xf = [x[s * sb:(s + 1) * sb].reshape(n, D_MODEL) for s in range(streams)]
    xb = [xs.astype(BF16) for xs in xf]
    p_all = p.reshape(DEPTH, batch * seq, P_DIM)

    wg_all = w_gate_e.reshape(DEPTH * N_EXPERTS, D_MODEL, D_EXPERT)
    wu_all = w_up_e.reshape(DEPTH * N_EXPERTS, D_MODEL, D_EXPERT)
    wd_all = w_down_e.reshape(DEPTH * N_EXPERTS, D_EXPERT, D_MODEL)

    c_uv = 2 * G_WIDTH
    c_qkv = c_uv + 2 * M_QK + M_V
    c_o = c_qkv + M_V
    c_if = c_o + 2 * M_HEADS
    c_gb = c_if + D_MODEL

    def layer_weights(l):
        w = w_in[l]
        b = b_in[l][None, :]
        return dict(
            sgu=(w[:, :G_WIDTH].astype(BF16), b[:, :G_WIDTH], w[:, G_WIDTH:c_uv].astype(BF16), b[:, G_WIDTH:c_uv],
                 sg_ln_g[l][None, :], sg_ln_b[l][None, :], w_s[l], b_s[l].T),
            mproj=(w[:, c_uv:c_qkv].astype(BF16), b[:, c_uv:c_qkv], w[:, c_qkv:c_o].astype(BF16), b[:, c_qkv:c_o],
                   jnp.pad(w[:, c_o:c_if], ((0, 0), (0, LANES - 2 * M_HEADS))).astype(BF16),
                   jnp.pad(b[:, c_o:c_if], ((0, 0), (0, LANES - 2 * M_HEADS)))),
            mhg=mh_g[l][None, :],
            merge=(w[:, c_if:c_gb].astype(BF16), b[:, c_if:c_gb], w[:, c_gb:].astype(BF16), b[:, c_gb:],
                   w_pa[l].astype(BF16), w_pb[l].astype(BF16), w_out[l].astype(BF16),
                   ln1_g[l][None, :], ln1_b[l][None, :]),
            router=(w_router[l].T, b_router[l][:, None]),
            finish=(w_gate_s[l].astype(BF16), w_up_s[l].astype(BF16), w_down_s[l].astype(BF16),
                    w_pg[l].astype(BF16), b_pg[l][None, :], w_p[l].astype(BF16),
                    ln2_g[l][None, :], ln2_b[l][None, :]))

    weights = [layer_weights(l) for l in range(DEPTH)]
    order = _IssueOrder()
    st = [dict() for _ in range(streams)]
    result = [None]

    def proj(s, l):
        st[s]["qkv"], st[s]["osig"], gif = _mproj(order.before(xb[s]), *weights[l]["mproj"])
        st[s]["gif"] = gif
        st[s]["gif_rows"] = gif[:, :2 * M_HEADS].reshape(sb, seq, 2 * M_HEADS).transpose(0, 2, 1)
        order.after(gif)
        st[s]["a"] = _sgu(order.before(xb[s]), *weights[l]["sgu"])
        order.after(st[s]["a"])

    def recur(s, l):
        st[s]["mb"] = _mlstm(order.before(st[s]["qkv"]), st[s]["gif"], st[s]["gif_rows"], st[s]["osig"],
                             weights[l]["mhg"], sb, seq)
        order.after(st[s]["mb"])

    def merge(s, l):
        xf[s], xb[s], xpa, xpb = _merge(order.before(xf[s]), xb[s], st[s]["a"], st[s]["mb"], *weights[l]["merge"])
        st[s]["packed"] = (xpa, xpb)
        order.after(xpb)

    def route(s, l):
        st[s]["route"] = _route(order.before(xf[s]), *st[s]["packed"], *weights[l]["router"])
        order.after(st[s]["route"]["wtok"])

    def experts(s, l):
        r = st[s]["route"]
        ysa, ysb = _experts(r["tile_expert"], r["tiles_used"], order.before(r["xsa"]), r["xsb"], l,
                            wg_all, wu_all, wd_all)
        order.after(ysb)
        ga, gb = _sc_gather([ysa, ysb], r["slots_flat"])
        st[s]["gathered"] = (ga.reshape(TOP_K, n, PART), gb.reshape(TOP_K, n, PART))

    def finish(s, l):
        last = l == DEPTH - 1
        outs = _finish(order.before(xf[s]), xb[s], p_all, l, s, *st[s]["gathered"], st[s]["route"]["wtok"],
                       *weights[l]["finish"], out_base=result[0] if last else None,
                       total_rows=batch * seq if last else None)
        if last:
            result[0], = outs
        else:
            xf[s], xb[s] = outs
        order.after(outs[0])

    if streams == 1:
        for l in range(DEPTH):
            proj(0, l), recur(0, l), merge(0, l), route(0, l), experts(0, l), finish(0, l)
    else:
        proj(0, 0), recur(0, 0), merge(0, 0), route(0, 0), proj(1, 0)
        for l in range(DEPTH):
            more = l + 1 < DEPTH
            experts(0, l), recur(1, l), merge(1, l), finish(0, l), route(1, l)
            if more:
                proj(0, l + 1)
            experts(1, l)
            if more:
                recur(0, l + 1), merge(0, l + 1)
            finish(1, l)
            if more:
                route(0, l + 1), proj(1, l + 1)
    return result[0].reshape(batch, seq, D_MODEL)
```

```python
import functools

import jax
import jax.numpy as jnp
from jax import lax
from jax.experimental import pallas as pl
from jax.experimental.pallas import tpu as pltpu
from jax.experimental.pallas import tpu_sc as plsc

D_MODEL = 1024
DEPTH = 4
CHUNK = 64
P_DIM = 256
G_WIDTH = 1024
G_GROUPS = 8
G_GROUP_DIM = G_WIDTH // G_GROUPS
G_BLOCK = 128
M_HEADS = 4
M_QK_DIM = 128
M_V_DIM = 256
M_QK = M_HEADS * M_QK_DIM
M_V = M_HEADS * M_V_DIM
N_EXPERTS = 64
TOP_K = 8
N_GROUPS = 8
TOPK_GROUPS = 4
GROUP_SIZE = N_EXPERTS // N_GROUPS
D_EXPERT = 256
D_SHARED = 256
ROUTE_SCALE = 2.5
ALPHA = (2 * DEPTH) ** 0.25
LN_EPS = 1e-5
RMS_EPS = 1e-6

LANES = 128
VMEM_LIMIT = 56 * 1024 * 1024
HALF = D_MODEL // 2
PART = HALF // 2
ROW_TILE = 1024
STREAMS = 2
ROUTER_PART = 512
MLSTM_GROUP = 8
SC_WINDOW = 128

F32 = jnp.float32
BF16 = jnp.bfloat16
I32 = jnp.int32


def _params(*semantics):
    return pltpu.CompilerParams(dimension_semantics=semantics, vmem_limit_bytes=VMEM_LIMIT)


def _layer_norm(x, g, b):
    mu = jnp.mean(x, axis=-1, keepdims=True)
    xc = x - mu
    var = jnp.mean(xc * xc, axis=-1, keepdims=True)
    return xc * lax.rsqrt(var + LN_EPS) * g + b


def _gelu(x):
    return 0.5 * x * (1.0 + lax.erf(x * (2.0 ** -0.5)))


def _silu(x):
    return x * jax.nn.sigmoid(x)


def _pack_pairs(x):
    lo = lax.bitcast_convert_type(x[:, :PART].astype(BF16).astype(F32), I32)
    hi = lax.bitcast_convert_type(x[:, PART:].astype(BF16).astype(F32), I32)
    return lax.shift_right_logical(lo, 16) | (hi & jnp.int32(-65536))


def _pack_row(x):
    return _pack_pairs(x[:, :HALF]), _pack_pairs(x[:, HALF:])


def _unpack_pairs(w):
    lo = lax.bitcast_convert_type(lax.shift_left(w, 16), F32)
    hi = lax.bitcast_convert_type(w & jnp.int32(-65536), F32)
    return jnp.concatenate([lo, hi], axis=1)


def _mproj_kernel(x_ref, wqkv_ref, bqkv_ref, wo_ref, bo_ref, wif_ref, bif_ref, qkv_ref, osig_ref, gif_ref, gifr_ref):
    x = x_ref[...]
    qkv_ref[...] = (jnp.dot(x, wqkv_ref[...], preferred_element_type=F32) + bqkv_ref[...]).astype(qkv_ref.dtype)
    osig_ref[...] = jax.nn.sigmoid(
        jnp.dot(x, wo_ref[...], preferred_element_type=F32) + bo_ref[...]).astype(osig_ref.dtype)
    gif = jnp.dot(x, wif_ref[...], preferred_element_type=F32) + bif_ref[...]
    gif_ref[...] = gif
    gifr_ref[...] = gif.T[:2 * M_HEADS, :]


def _mproj(x, w_qkv, b_qkv, w_o, b_o, w_if, b_if, tm=512):
    n, k = x.shape
    row = lambda i: (i, 0)
    const = lambda i: (0, 0)
    widths = (w_qkv.shape[1], w_o.shape[1], w_if.shape[1])
    in_specs = [pl.BlockSpec((tm, k), row)]
    for width in widths:
        in_specs += [pl.BlockSpec((k, width), const), pl.BlockSpec((1, width), const)]
    return pl.pallas_call(
        _mproj_kernel,
        grid=(n // tm,),
        in_specs=in_specs,
        out_specs=[pl.BlockSpec((tm, width), row) for width in widths]
        + [pl.BlockSpec((2 * M_HEADS, tm), lambda i: (0, i))],
        out_shape=[jax.ShapeDtypeStruct((n, widths[0]), BF16), jax.ShapeDtypeStruct((n, widths[1]), BF16),
                   jax.ShapeDtypeStruct((n, widths[2]), F32), jax.ShapeDtypeStruct((2 * M_HEADS, n), F32)],
        name="mlstm_proj",
        compiler_params=_params("parallel"),
    )(x, w_qkv, b_qkv, w_o, b_o, w_if, b_if)


def _sgu_kernel(x_ref, wu_ref, bu_ref, wv_ref, bv_ref, lng_ref, lnb_ref, ws_ref, bs_ref, a_ref, *, blocks):
    x = x_ref[...]
    v = _gelu(jnp.dot(x, wv_ref[...], preferred_element_type=F32) + bv_ref[...])
    vln = _layer_norm(v, lng_ref[...], lnb_ref[...]).astype(BF16)
    u = _gelu(jnp.dot(x, wu_ref[...], preferred_element_type=F32) + bu_ref[...])
    t_chunk = lax.broadcasted_iota(jnp.int32, (G_BLOCK, G_BLOCK), 0) // CHUNK
    s_chunk = lax.broadcasted_iota(jnp.int32, (G_BLOCK, G_BLOCK), 1) // CHUNK
    causal = s_chunk <= t_chunk
    for g in range(G_GROUPS):
        w = jnp.where(causal, ws_ref[g], 0.0).astype(BF16)
        bias = bs_ref[:, g:g + 1]
        cs = slice(g * G_GROUP_DIM, (g + 1) * G_GROUP_DIM)
        for blk in range(blocks):
            rs = slice(blk * G_BLOCK, (blk + 1) * G_BLOCK)
            mixed = jnp.dot(w, vln[rs, cs], preferred_element_type=F32) + bias
            a_ref[rs, cs] = (u[rs, cs] * mixed).astype(a_ref.dtype)


def _sgu(x, w_u, b_u, w_v, b_v, ln_g, ln_b, w_s, b_s_t, blocks=4):
    n = x.shape[0]
    tp = blocks * G_BLOCK
    const = lambda i: (0, 0)
    wspec = pl.BlockSpec((D_MODEL, G_WIDTH), const)
    vspec = pl.BlockSpec((1, G_WIDTH), const)
    return pl.pallas_call(
        functools.partial(_sgu_kernel, blocks=blocks),
        grid=(n // tp,),
        in_specs=[
            pl.BlockSpec((tp, D_MODEL), lambda i: (i, 0)),
            wspec, vspec, wspec, vspec, vspec, vspec,
            pl.BlockSpec((G_GROUPS, G_BLOCK, G_BLOCK), lambda i: (0, 0, 0)),
            pl.BlockSpec((G_BLOCK, G_GROUPS), const),
        ],
        out_specs=pl.BlockSpec((tp, G_WIDTH), lambda i: (i, 0)),
        out_shape=jax.ShapeDtypeStruct((n, G_WIDTH), BF16),
        name="spatial_gating",
        compiler_params=_params("parallel"),
    )(x, w_u, b_u, w_v, b_v, ln_g, ln_b, w_s, b_s_t)


def _segment_cumsum(x, axis, seg):
    pos = lax.broadcasted_iota(jnp.int32, x.shape, axis) % seg
    shift = 1
    while shift < seg:
        x = x + jnp.where(pos >= shift, pltpu.roll(x, shift, axis), 0.0)
        shift *= 2
    return x


def _mlstm_kernel(qkv_ref, gc_ref, gr_ref, osig_ref, mhg_ref, mb_ref, c_ref, n_ref, m_ref, *, chunks):
    @pl.when(pl.program_id(1) == 0)
    def _():
        c_ref[...] = jnp.zeros_like(c_ref)
        n_ref[...] = jnp.zeros_like(n_ref)
        m_ref[...] = jnp.zeros_like(m_ref)

    scale = M_QK_DIM ** -0.5
    gc = gc_ref[...]
    gr = gr_ref[...]
    b_cols = _segment_cumsum(jax.nn.log_sigmoid(gc), 0, CHUNK)
    b_rows = _segment_cumsum(jax.nn.log_sigmoid(gr), 1, CHUNK)
    t_idx = lax.broadcasted_iota(jnp.int32, (CHUNK, CHUNK), 0)
    s_idx = lax.broadcasted_iota(jnp.int32, (CHUNK, CHUNK), 1)
    tri = s_idx <= t_idx

    heads = range(M_HEADS)
    rows = lambda c: slice(c * CHUNK, (c + 1) * CHUNK)

    def local_terms(cs):
        units = [(c, h) for c in cs for h in heads]
        q = {u: qkv_ref[rows(u[0]), u[1] * M_QK_DIM:(u[1] + 1) * M_QK_DIM] for u in units}
        k = {u: qkv_ref[rows(u[0]), M_QK + u[1] * M_QK_DIM:M_QK + (u[1] + 1) * M_QK_DIM] for u in units}
        v = {u: qkv_ref[rows(u[0]), 2 * M_QK + u[1] * M_V_DIM:2 * M_QK + (u[1] + 1) * M_V_DIM] for u in units}
        i_col = {u: gc[rows(u[0]), u[1]:u[1] + 1] for u in units}
        b_col = {u: b_cols[rows(u[0]), M_HEADS + u[1]:M_HEADS + u[1] + 1] for u in units}
        ib_row = {u: (b_rows[M_HEADS + u[1]:M_HEADS + u[1] + 1, rows(u[0])]
                      - gr[u[1]:u[1] + 1, rows(u[0])]) for u in units}
        d = {u: jnp.where(tri, b_col[u] - ib_row[u], -jnp.inf) for u in units}
        d_max = {u: jnp.max(d[u], axis=1, keepdims=True) for u in units}
        qk = {u: lax.dot_general(q[u], k[u], (((1,), (1,)), ((), ())), preferred_element_type=F32) for u in units}
        s_loc = {u: qk[u] * scale * jnp.exp(d[u] - d_max[u]) for u in units}
        a_loc = {u: jnp.dot(s_loc[u].astype(BF16), v[u], preferred_element_type=F32) for u in units}
        r_loc = {u: jnp.sum(s_loc[u], axis=1, keepdims=True) for u in units}
        b_last = {u: b_col[u][CHUNK - 1:CHUNK, :] for u in units}
        g_col = {u: b_last[u] - b_col[u] + i_col[u] for u in units}
        g_max = {u: jnp.max(g_col[u], axis=0, keepdims=True) for u in units}
        wk = {u: jnp.exp(g_col[u] - g_max[u]) * k[u].astype(F32) for u in units}
        u_loc = {u: jnp.dot(wk[u].T.astype(BF16), v[u], preferred_element_type=F32) for u in units}
        nk_loc = {u: jnp.sum(wk[u], axis=0, keepdims=True) for u in units}
        return dict(q=q, b_col=b_col, d_max=d_max, a_loc=a_loc, r_loc=r_loc, b_last=b_last, g_max=g_max,
                    u_loc=u_loc, nk_loc=nk_loc)

    def carried_step(c, loc, c_state, n_state, m_state):
        inter = [loc["b_col"][(c, h)] + m_state[h] for h in heads]
        m_t = [jnp.maximum(inter[h], loc["d_max"][(c, h)]) for h in heads]
        w_intra = [jnp.exp(loc["d_max"][(c, h)] - m_t[h]) for h in heads]
        w_inter = [jnp.exp(inter[h] - m_t[h]) for h in heads]
        qc = [jnp.dot(loc["q"][(c, h)], c_state[h].astype(BF16), preferred_element_type=F32) * scale for h in heads]
        qn = [jnp.sum(loc["q"][(c, h)].astype(F32) * n_state[h], axis=1, keepdims=True) * scale for h in heads]
        m_new = [jnp.maximum(loc["b_last"][(c, h)] + m_state[h], loc["g_max"][(c, h)]) for h in heads]
        decay = [jnp.exp(loc["b_last"][(c, h)] + m_state[h] - m_new[h]) for h in heads]
        beta = [jnp.exp(loc["g_max"][(c, h)] - m_new[h]) for h in heads]
        c_next = [decay[h] * c_state[h] + beta[h] * loc["u_loc"][(c, h)] for h in heads]
        n_next = [decay[h] * n_state[h] + beta[h] * loc["nk_loc"][(c, h)] for h in heads]
        num = [w_intra[h] * loc["a_loc"][(c, h)] + w_inter[h] * qc[h] for h in heads]
        den = [w_intra[h] * loc["r_loc"][(c, h)] + w_inter[h] * qn[h] for h in heads]
        hv = [num[h] * (1.0 / jnp.maximum(jnp.abs(den[h]), jnp.exp(-m_t[h]))) for h in heads]
        hv = [hv[h] * lax.rsqrt(jnp.mean(hv[h] * hv[h], axis=1, keepdims=True) + RMS_EPS) for h in heads]
        for h in heads:
            vs = slice(h * M_V_DIM, (h + 1) * M_V_DIM)
            mb_ref[rows(c), vs] = (osig_ref[rows(c), vs].astype(F32) * (hv[h] * mhg_ref[:, vs])).astype(mb_ref.dtype)
        return c_next, n_next, m_new

    c_state = [c_ref[h] for h in heads]
    n_state = [n_ref[h] for h in heads]
    m_state = [m_ref[h][:, 0:1] for h in heads]
    groups = [list(range(g, min(g + MLSTM_GROUP, chunks))) for g in range(0, chunks, MLSTM_GROUP)]
    loc = local_terms(groups[0])
    for gi, group in enumerate(groups):
        loc_next = local_terms(groups[gi + 1]) if gi + 1 < len(groups) else None
        for c in group:
            c_state, n_state, m_state = carried_step(c, loc, c_state, n_state, m_state)
        loc = loc_next

    for h in heads:
        c_ref[h], n_ref[h] = c_state[h], n_state[h]
        m_ref[h] = jnp.broadcast_to(m_state[h], (1, LANES))


def _mlstm(qkv, gates_col, gates_row, osig_src, mh_g, batch, seq, chunks=8):
    ts = chunks * CHUNK
    tiles = seq // ts
    n = batch * seq
    return pl.pallas_call(
        functools.partial(_mlstm_kernel, chunks=chunks),
        grid=(batch, tiles),
        in_specs=[
            pl.BlockSpec((ts, 2 * M_QK + M_V), lambda b, t: (b * tiles + t, 0)),
            pl.BlockSpec((ts, LANES), lambda b, t: (b * tiles + t, 0)),
            pl.BlockSpec((2 * M_HEADS, ts), lambda b, t: (0, b * tiles + t)),
            pl.BlockSpec((ts, M_V), lambda b, t: (b * tiles + t, 0)),
            pl.BlockSpec((1, M_V), lambda b, t: (0, 0)),
        ],
        out_specs=pl.BlockSpec((ts, M_V), lambda b, t: (b * tiles + t, 0)),
        out_shape=jax.ShapeDtypeStruct((n, M_V), BF16),
        scratch_shapes=[
            pltpu.VMEM((M_HEADS, M_QK_DIM, M_V_DIM), F32),
            pltpu.VMEM((M_HEADS, 1, M_QK_DIM), F32),
            pltpu.VMEM((M_HEADS, 1, LANES), F32),
        ],
        name="mlstm",
        compiler_params=_params("parallel", "arbitrary"),
    )(qkv, gates_col, gates_row, osig_src, mh_g)


def _merge_kernel(x_ref, xin_ref, a_ref, mb_ref, wga_ref, bga_ref, wgb_ref, bgb_ref, wpa_ref, wpb_ref, wout_ref,
                  g_ref, b_ref, xo_ref, xob_ref, xpa_ref, xpb_ref):
    xin = xin_ref[...]
    ga = jax.nn.sigmoid(jnp.dot(xin, wga_ref[...], preferred_element_type=F32) + bga_ref[...])
    gb = jax.nn.sigmoid(jnp.dot(xin, wgb_ref[...], preferred_element_type=F32) + bgb_ref[...])
    ya = jnp.dot(a_ref[...], wpa_ref[...], preferred_element_type=F32)
    yb = jnp.dot(mb_ref[...], wpb_ref[...], preferred_element_type=F32)
    y = ga * ya + gb * yb
    mix = jnp.dot(y.astype(BF16), wout_ref[...], preferred_element_type=F32)
    xn = _layer_norm(ALPHA * x_ref[...] + mix, g_ref[...], b_ref[...])
    xo_ref[...] = xn
    xob_ref[...] = xn.astype(BF16)
    xpa_ref[...], xpb_ref[...] = _pack_row(xn)


def _merge(x, xin, a, mb, w_ga, b_ga, w_gb, b_gb, w_pa, w_pb, w_out, ln_g, ln_b, tm=512):
    n = x.shape[0]
    row = lambda i: (i, 0)
    const = lambda i: (0, 0)
    wspec = pl.BlockSpec((D_MODEL, D_MODEL), const)
    vspec = pl.BlockSpec((1, D_MODEL), const)
    return pl.pallas_call(
        _merge_kernel,
        grid=(n // tm,),
        in_specs=[
            pl.BlockSpec((tm, D_MODEL), row),
            pl.BlockSpec((tm, D_MODEL), row),
            pl.BlockSpec((tm, G_WIDTH), row),
            pl.BlockSpec((tm, M_V), row),
            wspec, vspec, wspec, vspec, wspec, wspec, wspec, vspec, vspec,
        ],
        out_specs=[pl.BlockSpec((tm, D_MODEL), row), pl.BlockSpec((tm, D_MODEL), row),
                   pl.BlockSpec((tm, PART), row), pl.BlockSpec((tm, PART), row)],
        out_shape=[jax.ShapeDtypeStruct((n, D_MODEL), F32), jax.ShapeDtypeStruct((n, D_MODEL), BF16),
                   jax.ShapeDtypeStruct((n, PART), I32), jax.ShapeDtypeStruct((n, PART), I32)],
        name="merge_ln",
        compiler_params=_params("parallel"),
    )(x, xin, a, mb, w_ga, b_ga, w_gb, b_gb, w_pa, w_pb, w_out, ln_g, ln_b)


def _first_max(v, idx, axes, sentinel):
    m = jnp.max(v, axis=axes, keepdims=True)
    first = jnp.min(jnp.where(v == m, idx, sentinel), axis=axes, keepdims=True)
    return m, first


def _router_kernel(x_ref, wr_ref, br_ref, upper_ref, eidx_ref, rank_ref, wtok_ref, cnt_ref, run_ref, *, tm):
    @pl.when(pl.program_id(0) == 0)
    def _():
        run_ref[...] = jnp.zeros_like(run_ref)

    tw = ROUTER_PART
    parts = range(tm // tw)
    shape3 = (N_GROUPS, GROUP_SIZE, tw)
    neg = -jnp.inf

    def split(v):
        hi = v.astype(BF16)
        return hi, (v - hi.astype(F32)).astype(BF16)

    nt = lambda a, b: lax.dot_general(a, b, (((1,), (1,)), ((), ())), preferred_element_type=F32)
    w_hi, w_lo = split(wr_ref[...])
    xs = [split(x_ref[i * tw:(i + 1) * tw, :]) for i in parts]
    logits = [nt(w_hi, x_hi) + (nt(w_hi, x_lo) + nt(w_lo, x_hi)) for x_hi, x_lo in xs]
    scores = [jax.nn.sigmoid(lg) for lg in logits]
    sel = [(sc + br_ref[...]).reshape(shape3) for sc in scores]
    scores3 = [sc.reshape(shape3) for sc in scores]
    member = lax.broadcasted_iota(jnp.int32, shape3, 1)
    group = lax.broadcasted_iota(jnp.int32, shape3, 0)

    best = [_first_max(v, member, 1, GROUP_SIZE) for v in sel]
    second = [jnp.max(jnp.where(member == f1, neg, v), axis=1, keepdims=True) for v, (_, f1) in zip(sel, best)]
    gscore = [m1 + m2 for (m1, _), m2 in zip(best, second)]
    gid = lax.broadcasted_iota(jnp.int32, gscore[0].shape, 0)
    gmask = [jnp.zeros(gscore[0].shape, dtype=jnp.bool_) for _ in parts]
    for _ in range(TOPK_GROUPS):
        hits = [gid == _first_max(g, gid, 0, N_GROUPS)[1] for g in gscore]
        gmask = [m | h for m, h in zip(gmask, hits)]
        gscore = [jnp.where(h, neg, g) for h, g in zip(hits, gscore)]

    cand = [jnp.where(m, v, neg) for m, v in zip(gmask, sel)]
    eid = group * GROUP_SIZE + member
    chosen = [jnp.zeros(shape3, dtype=jnp.bool_) for _ in parts]
    picks = [[] for _ in parts]
    for _ in range(TOP_K):
        firsts = [_first_max(c, eid, (0, 1), N_EXPERTS)[1] for c in cand]
        hits = [eid == fe for fe in firsts]
        chosen = [c | h for c, h in zip(chosen, hits)]
        cand = [jnp.where(h, neg, c) for h, c in zip(hits, cand)]
        picked = [jnp.sum(jnp.where(h, s3, 0.0), axis=(0, 1), keepdims=True)[0] for h, s3 in zip(hits, scores3)]
        for i in parts:
            picks[i].append((firsts[i][0], hits[i], picked[i]))

    chosen2 = [jnp.where(c, 1.0, 0.0).reshape(N_EXPERTS, tw) for c in chosen]
    counts = [jnp.sum(c2, axis=1, keepdims=True) for c2 in chosen2]
    bases = []
    base = run_ref[:, 0:1]
    for i in parts:
        bases.append(base)
        base = base + counts[i]
    prefix3 = [(jnp.dot(c2.astype(BF16), upper_ref[...], preferred_element_type=F32) + b0).reshape(shape3)
               for c2, b0 in zip(chosen2, bases)]

    for i in parts:
        cols = slice(i * tw, (i + 1) * tw)
        total = picks[i][0][2]
        for _, _, wk in picks[i][1:]:
            total = total + wk
        eidx_ref[:, cols] = jnp.concatenate([fe for fe, _, _ in picks[i]], axis=0)
        rank_ref[:, cols] = jnp.concatenate(
            [jnp.sum(jnp.where(hit, prefix3[i], 0.0), axis=(0, 1), keepdims=True)[0] for _, hit, _ in picks[i]],
            axis=0).astype(I32)
        w_rows = jnp.concatenate([wk / total * ROUTE_SCALE for _, _, wk in picks[i]]
                                 + [jnp.zeros((LANES - TOP_K, tw), F32)], axis=0)
        wtok_ref[cols, :] = w_rows.T

    run = run_ref[...] + (base - run_ref[:, 0:1])
    run_ref[...] = run
    cnt_ref[...] = run


def _router(x, wr_t, br, tm=1024):
    n = x.shape[0]
    tok = lambda i: (0, i)
    upper = (jnp.arange(ROUTER_PART)[:, None] < jnp.arange(ROUTER_PART)[None, :]).astype(BF16)
    return pl.pallas_call(
        functools.partial(_router_kernel, tm=tm),
        grid=(n // tm,),
        in_specs=[
            pl.BlockSpec((tm, D_MODEL), lambda i: (i, 0)),
            pl.BlockSpec((N_EXPERTS, D_MODEL), lambda i: (0, 0)),
            pl.BlockSpec((N_EXPERTS, 1), lambda i: (0, 0)),
            pl.BlockSpec((ROUTER_PART, ROUTER_PART), lambda i: (0, 0)),
        ],
        out_specs=[
            pl.BlockSpec((TOP_K, tm), tok),
            pl.BlockSpec((TOP_K, tm), tok),
            pl.BlockSpec((tm, LANES), lambda i: (i, 0)),
            pl.BlockSpec((N_EXPERTS, LANES), lambda i: (0, 0)),
        ],
        out_shape=[
            jax.ShapeDtypeStruct((TOP_K, n), I32),
            jax.ShapeDtypeStruct((TOP_K, n), I32),
            jax.ShapeDtypeStruct((n, LANES), F32),
            jax.ShapeDtypeStruct((N_EXPERTS, LANES), F32),
        ],
        scratch_shapes=[pltpu.VMEM((N_EXPERTS, LANES), F32)],
        name="router",
        compiler_params=_params("arbitrary"),
    )(x, wr_t, br, upper)


def _slots_kernel(starts_ref, eidx_ref, rank_ref, slot_ref):
    eidx = eidx_ref[...]
    slot = rank_ref[...]
    for e in range(N_EXPERTS):
        slot = slot + jnp.where(eidx == e, starts_ref[e], 0)
    slot_ref[...] = slot


def _slots(starts, eidx, rank, tn=2048):
    n = eidx.shape[1]
    tn = min(tn, n)
    spec = pl.BlockSpec((TOP_K, tn), lambda i, s: (0, i))
    return pl.pallas_call(
        _slots_kernel,
        grid_spec=pltpu.PrefetchScalarGridSpec(
            num_scalar_prefetch=1, grid=(n // tn,), in_specs=[spec, spec], out_specs=spec),
        out_shape=jax.ShapeDtypeStruct((TOP_K, n), I32),
        name="slots",
        compiler_params=_params("parallel"),
    )(starts, eidx, rank)


def _sc_mesh():
    return plsc.VectorSubcoreMesh(core_axis_name="c", subcore_axis_name="s")


def _sc_dispatch(parts, slots_flat, rows):
    n, width = parts[0].shape
    blocks = n // SC_WINDOW
    out_type = [jax.ShapeDtypeStruct((rows, width), part.dtype) for part in parts]

    @functools.partial(pl.kernel, out_type=out_type, mesh=_sc_mesh(), scratch_types=[], name="sc_dispatch")
    def run(*refs):
        i_hbm = refs[len(parts)]
        for x_hbm, o_hbm in zip(refs[:len(parts)], refs[len(parts) + 1:]):
            def body(x_vmem, i_vmem, o_hbm=o_hbm):
                pltpu.sync_copy(x_vmem, o_hbm.at[i_vmem.at[0]])

            pltpu.emit_pipeline(
                body,
                grid=(blocks, TOP_K),
                in_specs=[pl.BlockSpec((SC_WINDOW, width), lambda i, k: (i, 0)),
                          pl.BlockSpec((1, SC_WINDOW), lambda i, k: (0, k * blocks + i))],
                out_specs=[],
                core_axis_name=("c", "s"),
                dimension_semantics=(pltpu.PARALLEL, pltpu.ARBITRARY),
                trace_scopes=False,
            )(x_hbm, i_hbm)

    return run(*parts, slots_flat)


def _sc_gather(tables, idx_flat):
    count = idx_flat.shape[1]
    width = tables[0].shape[1]
    out_type = [jax.ShapeDtypeStruct((count, width), table.dtype) for table in tables]

    @functools.partial(pl.kernel, out_type=out_type, mesh=_sc_mesh(), scratch_types=[], name="sc_gather")
    def run(*refs):
        i_hbm = refs[len(tables)]
        for t_hbm, o_hbm in zip(refs[:len(tables)], refs[len(tables) + 1:]):
            def body(i_vmem, o_vmem, t_hbm=t_hbm):
                pltpu.sync_copy(t_hbm.at[i_vmem.at[0]], o_vmem)

            pltpu.emit_pipeline(
                body,
                grid=(count // SC_WINDOW,),
                in_specs=[pl.BlockSpec((1, SC_WINDOW), lambda i: (0, i))],
                out_specs=[pl.BlockSpec((SC_WINDOW, width), lambda i: (i, 0))],
                core_axis_name=("c", "s"),
                dimension_semantics=(pltpu.PARALLEL,),
                trace_scopes=False,
            )(i_hbm, o_hbm)

    return run(*tables, idx_flat)


def _expert_kernel(te_ref, used_ref, xa_ref, xb_ref, wg_ref, wu_ref, wd_ref, oa_ref, ob_ref, wgb_ref, wub_ref, wdb_ref):
    j = pl.program_id(0)

    @pl.when((j == 0) | (te_ref[j] != te_ref[jnp.maximum(j - 1, 0)]))
    def _():
        wgb_ref[...] = wg_ref[0].astype(BF16)
        wub_ref[...] = wu_ref[0].astype(BF16)
        wdb_ref[...] = wd_ref[0].astype(BF16)

    @pl.when(j < used_ref[0])
    def _():
        x = jnp.concatenate([_unpack_pairs(xa_ref[...]), _unpack_pairs(xb_ref[...])], axis=1).astype(BF16)
        hg = jnp.dot(x, wgb_ref[...], preferred_element_type=F32)
        hu = jnp.dot(x, wub_ref[...], preferred_element_type=F32)
        hid = (_silu(hg) * hu).astype(BF16)
        oa_ref[...], ob_ref[...] = _pack_row(jnp.dot(hid, wdb_ref[...], preferred_element_type=F32))


def _experts(tile_expert, tiles_used, xa, xb, layer, wg, wu, wd):
    rows = xa.shape[0]
    row = lambda j, te, used: (j, 0)
    exp = lambda j, te, used: (layer * N_EXPERTS + te[j], 0, 0)
    return pl.pallas_call(
        _expert_kernel,
        grid_spec=pltpu.PrefetchScalarGridSpec(
            num_scalar_prefetch=2,
            grid=(rows // ROW_TILE,),
            in_specs=[
                pl.BlockSpec((ROW_TILE, PART), row),
                pl.BlockSpec((ROW_TILE, PART), row),
                pl.BlockSpec((1, D_MODEL, D_EXPERT), exp),
                pl.BlockSpec((1, D_MODEL, D_EXPERT), exp),
                pl.BlockSpec((1, D_EXPERT, D_MODEL), exp),
            ],
            out_specs=[pl.BlockSpec((ROW_TILE, PART), row), pl.BlockSpec((ROW_TILE, PART), row)],
            scratch_shapes=[pltpu.VMEM((D_MODEL, D_EXPERT), BF16), pltpu.VMEM((D_MODEL, D_EXPERT), BF16),
                            pltpu.VMEM((D_EXPERT, D_MODEL), BF16)],
        ),
        out_shape=[jax.ShapeDtypeStruct((rows, PART), I32), jax.ShapeDtypeStruct((rows, PART), I32)],
        name="experts",
        compiler_params=_params("arbitrary"),
    )(tile_expert, tiles_used, xa, xb, wg, wu, wd)


def _finish_kernel(x_ref, xb_ref, p_ref, ga_ref, gb_ref, wtok_ref, wgs_ref, wus_ref, wds_ref, wpg_ref, bpg_ref, wp_ref,
                   lng_ref, lnb_ref, *out_refs):
    xb = xb_ref[...]
    hs = _silu(jnp.dot(xb, wgs_ref[...], preferred_element_type=F32)) * jnp.dot(
        xb, wus_ref[...], preferred_element_type=F32)
    shared = jnp.dot(hs.astype(BF16), wds_ref[...], preferred_element_type=F32)
    gate = jax.nn.sigmoid(jnp.dot(xb, wpg_ref[...], preferred_element_type=F32) + bpg_ref[...])
    ple = gate * jnp.dot(p_ref[0].astype(BF16), wp_ref[...], preferred_element_type=F32)

    wtok = wtok_ref[...]
    r_a = None
    r_b = None
    for k in range(TOP_K):
        wk = wtok[:, k:k + 1]
        a = wk * _unpack_pairs(ga_ref[k])
        b = wk * _unpack_pairs(gb_ref[k])
        r_a = a if r_a is None else r_a + a
        r_b = b if r_b is None else r_b + b
    routed = jnp.concatenate([r_a, r_b], axis=1)

    xn = _layer_norm(ALPHA * x_ref[...] + (routed + shared + ple), lng_ref[...], lnb_ref[...])
    if len(out_refs) == 2 and out_refs[1].dtype == BF16:
        out_refs[0][...] = xn
        out_refs[1][...] = xn.astype(BF16)
    else:
        out_refs[-1][...] = xn


def _finish(x, xb, p_all, layer, stream, ga, gb, wtok, wgs, wus, wds, wpg, bpg, wp, ln_g, ln_b,
            out_base=None, total_rows=None, tm=512):
    n = x.shape[0]
    p_block = stream * (n // tm)
    row = lambda i: (i, 0)
    const = lambda i: (0, 0)
    if total_rows is None:
        out_specs = [pl.BlockSpec((tm, D_MODEL), row), pl.BlockSpec((tm, D_MODEL), row)]
        out_shape = [jax.ShapeDtypeStruct((n, D_MODEL), F32), jax.ShapeDtypeStruct((n, D_MODEL), BF16)]
    else:
        out_specs = [pl.BlockSpec((tm, D_MODEL), lambda i: (p_block + i, 0))]
        out_shape = [jax.ShapeDtypeStruct((total_rows, D_MODEL), F32)]
    extra_specs = [] if out_base is None else [pl.BlockSpec(memory_space=pl.ANY)]
    extra_args = [] if out_base is None else [out_base]
    aliases = {} if out_base is None else {14: 0}
    return pl.pallas_call(
        _finish_kernel,
        grid=(n // tm,),
        input_output_aliases=aliases,
        in_specs=[
            pl.BlockSpec((tm, D_MODEL), row),
            pl.BlockSpec((tm, D_MODEL), row),
            pl.BlockSpec((1, tm, P_DIM), lambda i: (layer, p_block + i, 0)),
            pl.BlockSpec((TOP_K, tm, PART), lambda i: (0, i, 0)),
            pl.BlockSpec((TOP_K, tm, PART), lambda i: (0, i, 0)),
            pl.BlockSpec((tm, LANES), row),
            pl.BlockSpec((D_MODEL, D_SHARED), const),
            pl.BlockSpec((D_MODEL, D_SHARED), const),
            pl.BlockSpec((D_SHARED, D_MODEL), const),
            pl.BlockSpec((D_MODEL, D_MODEL), const),
            pl.BlockSpec((1, D_MODEL), const),
            pl.BlockSpec((P_DIM, D_MODEL), const),
            pl.BlockSpec((1, D_MODEL), const),
            pl.BlockSpec((1, D_MODEL), const),
        ] + extra_specs,
        out_specs=out_specs,
        out_shape=out_shape,
        name="moe_finish_ln",
        compiler_params=_params("parallel"),
    )(x, xb, p_all, ga, gb, wtok, wgs, wus, wds, wpg, bpg, wp, ln_g, ln_b, *extra_args)


def _route(x, xpa, xpb, w_router_t, b_router):
    n = x.shape[0]
    pairs = n * TOP_K
    tiles = pairs // ROW_TILE + N_EXPERTS
    eidx, rank, wtok, counts = _router(x, w_router_t, b_router)

    cnt = counts[:, 0].astype(I32)
    group_tiles = (cnt + ROW_TILE - 1) // ROW_TILE
    tile_end = jnp.cumsum(group_tiles)
    starts = (tile_end - group_tiles) * ROW_TILE
    tile_id = jnp.arange(tiles, dtype=I32)
    tile_expert = jnp.minimum(jnp.sum((tile_end[None, :] <= tile_id[:, None]).astype(I32), axis=1), N_EXPERTS - 1)

    slots_flat = _slots(starts, eidx, rank).reshape(1, pairs)
    xsa, xsb = _sc_dispatch([xpa, xpb], slots_flat, tiles * ROW_TILE)
    return dict(xsa=xsa, xsb=xsb, slots_flat=slots_flat, wtok=wtok, tile_expert=tile_expert,
                tiles_used=tile_end[N_EXPERTS - 1:])


class _IssueOrder:
    def __init__(self):
        self.last = None

    def before(self, operand):
        if self.last is None:
            return operand
        self.last, operand = lax.optimization_barrier((self.last, operand))
        return operand

    def after(self, result):
        self.last = result


def kernel(x, p, w_in, b_in, sg_ln_g, sg_ln_b, w_s, b_s, mh_g, w_pa, w_pb, w_out, ln1_g, ln1_b, w_router, b_router, w_gate_e, w_up_e, w_down_e, w_gate_s, w_up_s, w_down_s, w_pg, b_pg, w_p, ln2_g, ln2_b):
    batch, seq, _ = x.shape
    streams = STREAMS if batch % STREAMS == 0 else 1
    sb = batch // streams
    n = sb * seq
    xf = [x[s * sb:(s + 1) * sb].reshape(n, D_MODEL) for s in range(streams)]
    xb = [xs.astype(BF16) for xs in xf]
    p_all = p.reshape(DEPTH, batch * seq, P_DIM)

    wg_all = w_gate_e.reshape(DEPTH * N_EXPERTS, D_MODEL, D_EXPERT)
    wu_all = w_up_e.reshape(DEPTH * N_EXPERTS, D_MODEL, D_EXPERT)
    wd_all = w_down_e.reshape(DEPTH * N_EXPERTS, D_EXPERT, D_MODEL)

    c_uv = 2 * G_WIDTH
    c_qkv = c_uv + 2 * M_QK + M_V
    c_o = c_qkv + M_V
    c_if = c_o + 2 * M_HEADS
    c_gb = c_if + D_MODEL

    def layer_weights(l):
        w = w_in[l]
        b = b_in[l][None, :]
        return dict(
            sgu=(w[:, :G_WIDTH].astype(BF16), b[:, :G_WIDTH], w[:, G_WIDTH:c_uv].astype(BF16), b[:, G_WIDTH:c_uv],
                 sg_ln_g[l][None, :], sg_ln_b[l][None, :], w_s[l], b_s[l].T),
            mproj=(w[:, c_uv:c_qkv].astype(BF16), b[:, c_uv:c_qkv], w[:, c_qkv:c_o].astype(BF16), b[:, c_qkv:c_o],
                   jnp.pad(w[:, c_o:c_if], ((0, 0), (0, LANES - 2 * M_HEADS))).astype(BF16),
                   jnp.pad(b[:, c_o:c_if], ((0, 0), (0, LANES - 2 * M_HEADS)))),
            mhg=mh_g[l][None, :],
            merge=(w[:, c_if:c_gb].astype(BF16), b[:, c_if:c_gb], w[:, c_gb:].astype(BF16), b[:, c_gb:],
                   w_pa[l].astype(BF16), w_pb[l].astype(BF16), w_out[l].astype(BF16),
                   ln1_g[l][None, :], ln1_b[l][None, :]),
            router=(w_router[l].T, b_router[l][:, None]),
            finish=(w_gate_s[l].astype(BF16), w_up_s[l].astype(BF16), w_down_s[l].astype(BF16),
                    w_pg[l].astype(BF16), b_pg[l][None, :], w_p[l].astype(BF16),
                    ln2_g[l][None, :], ln2_b[l][None, :]))

    weights = [layer_weights(l) for l in range(DEPTH)]
    order = _IssueOrder()
    st = [dict() for _ in range(streams)]
    result = [None]

    def proj(s, l):
        st[s]["qkv"], st[s]["osig"], st[s]["gif"], st[s]["gif_rows"] = _mproj(
            order.before(xb[s]), *weights[l]["mproj"])
        order.after(st[s]["gif"])
        st[s]["a"] = _sgu(order.before(xb[s]), *weights[l]["sgu"])
        order.after(st[s]["a"])

    def recur(s, l):
        st[s]["mb"] = _mlstm(order.before(st[s]["qkv"]), st[s]["gif"], st[s]["gif_rows"], st[s]["osig"],
                             weights[l]["mhg"], sb, seq)
        order.after(st[s]["mb"])

    def merge(s, l):
        xf[s], xb[s], xpa, xpb = _merge(order.before(xf[s]), xb[s], st[s]["a"], st[s]["mb"], *weights[l]["merge"])
        st[s]["packed"] = (xpa, xpb)
        order.after(xpb)

    def route(s, l):
        st[s]["route"] = _route(order.before(xf[s]), *st[s]["packed"], *weights[l]["router"])
        order.after(st[s]["route"]["wtok"])

    def experts(s, l):
        r = st[s]["route"]
        ysa, ysb = _experts(r["tile_expert"], r["tiles_used"], order.before(r["xsa"]), r["xsb"], l,
                            wg_all, wu_all, wd_all)
        order.after(ysb)
        ga, gb = _sc_gather([ysa, ysb], r["slots_flat"])
        st[s]["gathered"] = (ga.reshape(TOP_K, n, PART), gb.reshape(TOP_K, n, PART))

    def finish(s, l):
        last = l == DEPTH - 1
        outs = _finish(order.before(xf[s]), xb[s], p_all, l, s, *st[s]["gathered"], st[s]["route"]["wtok"],
                       *weights[l]["finish"], out_base=result[0] if last else None,
                       total_rows=batch * seq if last else None)
        if last:
            result[0], = outs
        else:
            xf[s], xb[s] = outs
        order.after(outs[0])

    if streams == 1:
        for l in range(DEPTH):
            proj(0, l), recur(0, l), merge(0, l), route(0, l), experts(0, l), finish(0, l)
    else:
        proj(0, 0), recur(0, 0), merge(0, 0), route(0, 0), proj(1, 0)
        for l in range(DEPTH):
            more = l + 1 < DEPTH
            experts(0, l), recur(1, l), merge(1, l), finish(0, l), route(1, l)
            if more:
                proj(0, l + 1)
            experts(1, l)
            if more:
                recur(0, l + 1), merge(0, l + 1)
            finish(1, l)
            if more:
                route(0, l + 1), proj(1, l + 1)
    return result[0].reshape(batch, seq, D_MODEL)
```

```python
import functools

import jax
import jax.numpy as jnp
from jax import lax
from jax.experimental import pallas as pl
from jax.experimental.pallas import tpu as pltpu
from jax.experimental.pallas import tpu_sc as plsc

D_MODEL = 1024
DEPTH = 4
CHUNK = 64
P_DIM = 256
G_WIDTH = 1024
G_GROUPS = 8
G_GROUP_DIM = G_WIDTH // G_GROUPS
G_BLOCK = 128
M_HEADS = 4
M_QK_DIM = 128
M_V_DIM = 256
M_QK = M_HEADS * M_QK_DIM
M_V = M_HEADS * M_V_DIM
N_EXPERTS = 64
TOP_K = 8
N_GROUPS = 8
TOPK_GROUPS = 4
GROUP_SIZE = N_EXPERTS // N_GROUPS
D_EXPERT = 256
D_SHARED = 256
ROUTE_SCALE = 2.5
ALPHA = (2 * DEPTH) ** 0.25
LN_EPS = 1e-5
RMS_EPS = 1e-6

LANES = 128
VMEM_LIMIT = 56 * 1024 * 1024
HALF = D_MODEL // 2
PART = HALF // 2
ROW_TILE = 1024
STREAMS = 2
ROUTER_PART = 512
MLSTM_GROUP = 8
SC_WINDOW = 128

F32 = jnp.float32
BF16 = jnp.bfloat16
I32 = jnp.int32


def _params(*semantics):
    return pltpu.CompilerParams(dimension_semantics=semantics, vmem_limit_bytes=VMEM_LIMIT)


def _layer_norm(x, g, b):
    mu = jnp.mean(x, axis=-1, keepdims=True)
    xc = x - mu
    var = jnp.mean(xc * xc, axis=-1, keepdims=True)
    return xc * lax.rsqrt(var + LN_EPS) * g + b


def _gelu(x):
    return 0.5 * x * (1.0 + lax.erf(x * (2.0 ** -0.5)))


def _silu(x):
    return x * jax.nn.sigmoid(x)


def _pack_pairs(x):
    lo = lax.bitcast_convert_type(x[:, :PART].astype(BF16).astype(F32), I32)
    hi = lax.bitcast_convert_type(x[:, PART:].astype(BF16).astype(F32), I32)
    return lax.shift_right_logical(lo, 16) | (hi & jnp.int32(-65536))


def _pack_row(x):
    return _pack_pairs(x[:, :HALF]), _pack_pairs(x[:, HALF:])


def _unpack_pairs(w):
    lo = lax.bitcast_convert_type(lax.shift_left(w, 16), F32)
    hi = lax.bitcast_convert_type(w & jnp.int32(-65536), F32)
    return jnp.concatenate([lo, hi], axis=1)


def _mproj_kernel(x_ref, wqkv_ref, bqkv_ref, wo_ref, bo_ref, wif_ref, bif_ref, qkv_ref, osig_ref, gif_ref, gifr_ref):
    x = x_ref[...]
    qkv_ref[...] = (jnp.dot(x, wqkv_ref[...], preferred_element_type=F32) + bqkv_ref[...]).astype(qkv_ref.dtype)
    osig_ref[...] = jax.nn.sigmoid(
        jnp.dot(x, wo_ref[...], preferred_element_type=F32) + bo_ref[...]).astype(osig_ref.dtype)
    gif = jnp.dot(x, wif_ref[...], preferred_element_type=F32) + bif_ref[...]
    gif_ref[...] = gif
    gifr_ref[...] = gif.T[:2 * M_HEADS, :]


def _mproj(x, w_qkv, b_qkv, w_o, b_o, w_if, b_if, tm=512):
    n, k = x.shape
    row = lambda i: (i, 0)
    const = lambda i: (0, 0)
    widths = (w_qkv.shape[1], w_o.shape[1], w_if.shape[1])
    in_specs = [pl.BlockSpec((tm, k), row)]
    for width in widths:
        in_specs += [pl.BlockSpec((k, width), const), pl.BlockSpec((1, width), const)]
    return pl.pallas_call(
        _mproj_kernel,
        grid=(n // tm,),
        in_specs=in_specs,
        out_specs=[pl.BlockSpec((tm, width), row) for width in widths]
        + [pl.BlockSpec((2 * M_HEADS, tm), lambda i: (0, i))],
        out_shape=[jax.ShapeDtypeStruct((n, widths[0]), BF16), jax.ShapeDtypeStruct((n, widths[1]), BF16),
                   jax.ShapeDtypeStruct((n, widths[2]), F32), jax.ShapeDtypeStruct((2 * M_HEADS, n), F32)],
        name="mlstm_proj",
        compiler_params=_params("parallel"),
    )(x, w_qkv, b_qkv, w_o, b_o, w_if, b_if)


def _sgu_kernel(x_ref, wu_ref, bu_ref, wv_ref, bv_ref, lng_ref, lnb_ref, ws_ref, bs_ref, a_ref, *, blocks):
    x = x_ref[...]
    v = _gelu(jnp.dot(x, wv_ref[...], preferred_element_type=F32) + bv_ref[...])
    vln = _layer_norm(v, lng_ref[...], lnb_ref[...]).astype(BF16)
    u = _gelu(jnp.dot(x, wu_ref[...], preferred_element_type=F32) + bu_ref[...])
    t_chunk = lax.broadcasted_iota(jnp.int32, (G_BLOCK, G_BLOCK), 0) // CHUNK
    s_chunk = lax.broadcasted_iota(jnp.int32, (G_BLOCK, G_BLOCK), 1) // CHUNK
    causal = s_chunk <= t_chunk
    for g in range(G_GROUPS):
        w = jnp.where(causal, ws_ref[g], 0.0).astype(BF16)
        bias = bs_ref[:, g:g + 1]
        cs = slice(g * G_GROUP_DIM, (g + 1) * G_GROUP_DIM)
        for blk in range(blocks):
            rs = slice(blk * G_BLOCK, (blk + 1) * G_BLOCK)
            mixed = jnp.dot(w, vln[rs, cs], preferred_element_type=F32) + bias
            a_ref[rs, cs] = (u[rs, cs] * mixed).astype(a_ref.dtype)


def _sgu(x, w_u, b_u, w_v, b_v, ln_g, ln_b, w_s, b_s_t, blocks=4):
    n = x.shape[0]
    tp = blocks * G_BLOCK
    const = lambda i: (0, 0)
    wspec = pl.BlockSpec((D_MODEL, G_WIDTH), const)
    vspec = pl.BlockSpec((1, G_WIDTH), const)
    return pl.pallas_call(
        functools.partial(_sgu_kernel, blocks=blocks),
        grid=(n // tp,),
        in_specs=[
            pl.BlockSpec((tp, D_MODEL), lambda i: (i, 0)),
            wspec, vspec, wspec, vspec, vspec, vspec,
            pl.BlockSpec((G_GROUPS, G_BLOCK, G_BLOCK), lambda i: (0, 0, 0)),
            pl.BlockSpec((G_BLOCK, G_GROUPS), const),
        ],
        out_specs=pl.BlockSpec((tp, G_WIDTH), lambda i: (i, 0)),
        out_shape=jax.ShapeDtypeStruct((n, G_WIDTH), BF16),
        name="spatial_gating",
        compiler_params=_params("parallel"),
    )(x, w_u, b_u, w_v, b_v, ln_g, ln_b, w_s, b_s_t)


def _segment_cumsum(x, axis, seg):
    pos = lax.broadcasted_iota(jnp.int32, x.shape, axis) % seg
    shift = 1
    while shift < seg:
        x = x + jnp.where(pos >= shift, pltpu.roll(x, shift, axis), 0.0)
        shift *= 2
    return x


def _mlstm_kernel(qkv_ref, gc_ref, gr_ref, osig_ref, mhg_ref, mb_ref, c_ref, n_ref, m_ref, *, chunks):
    @pl.when(pl.program_id(1) == 0)
    def _():
        c_ref[...] = jnp.zeros_like(c_ref)
        n_ref[...] = jnp.zeros_like(n_ref)
        m_ref[...] = jnp.zeros_like(m_ref)

    scale = M_QK_DIM ** -0.5
    gc = gc_ref[...]
    gr = gr_ref[...]
    b_cols = _segment_cumsum(jax.nn.log_sigmoid(gc), 0, CHUNK)
    b_rows = _segment_cumsum(jax.nn.log_sigmoid(gr), 1, CHUNK)
    t_idx = lax.broadcasted_iota(jnp.int32, (CHUNK, CHUNK), 0)
    s_idx = lax.broadcasted_iota(jnp.int32, (CHUNK, CHUNK), 1)
    tri = s_idx <= t_idx

    heads = range(M_HEADS)
    rows = lambda c: slice(c * CHUNK, (c + 1) * CHUNK)

    def local_terms(cs):
        units = [(c, h) for c in cs for h in heads]
        q = {u: qkv_ref[rows(u[0]), u[1] * M_QK_DIM:(u[1] + 1) * M_QK_DIM] for u in units}
        k = {u: qkv_ref[rows(u[0]), M_QK + u[1] * M_QK_DIM:M_QK + (u[1] + 1) * M_QK_DIM] for u in units}
        v = {u: qkv_ref[rows(u[0]), 2 * M_QK + u[1] * M_V_DIM:2 * M_QK + (u[1] + 1) * M_V_DIM] for u in units}
        i_col = {u: gc[rows(u[0]), u[1]:u[1] + 1] for u in units}
        b_col = {u: b_cols[rows(u[0]), M_HEADS + u[1]:M_HEADS + u[1] + 1] for u in units}
        ib_row = {u: (b_rows[M_HEADS + u[1]:M_HEADS + u[1] + 1, rows(u[0])]
                      - gr[u[1]:u[1] + 1, rows(u[0])]) for u in units}
        d = {u: jnp.where(tri, b_col[u] - ib_row[u], -jnp.inf) for u in units}
        d_max = {u: jnp.max(d[u], axis=1, keepdims=True) for u in units}
        qk = {u: lax.dot_general(q[u], k[u], (((1,), (1,)), ((), ())), preferred_element_type=F32) for u in units}
        s_loc = {u: qk[u] * scale * jnp.exp(d[u] - d_max[u]) for u in units}
        a_loc = {u: jnp.dot(s_loc[u].astype(BF16), v[u], preferred_element_type=F32) for u in units}
        r_loc = {u: jnp.sum(s_loc[u], axis=1, keepdims=True) for u in units}
        b_last = {u: b_col[u][CHUNK - 1:CHUNK, :] for u in units}
        g_col = {u: b_last[u] - b_col[u] + i_col[u] for u in units}
        g_max = {u: jnp.max(g_col[u], axis=0, keepdims=True) for u in units}
        wk = {u: jnp.exp(g_col[u] - g_max[u]) * k[u].astype(F32) for u in units}
        u_loc = {u: jnp.dot(wk[u].T.astype(BF16), v[u], preferred_element_type=F32) for u in units}
        nk_loc = {u: jnp.sum(wk[u], axis=0, keepdims=True) for u in units}
        return dict(q=q, b_col=b_col, d_max=d_max, a_loc=a_loc, r_loc=r_loc, b_last=b_last, g_max=g_max,
                    u_loc=u_loc, nk_loc=nk_loc)

    def carried_step(c, loc, c_state, n_state, m_state):
        inter = [loc["b_col"][(c, h)] + m_state[h] for h in heads]
        m_t = [jnp.maximum(inter[h], loc["d_max"][(c, h)]) for h in heads]
        w_intra = [jnp.exp(loc["d_max"][(c, h)] - m_t[h]) for h in heads]
        w_inter = [jnp.exp(inter[h] - m_t[h]) for h in heads]
        qc = [jnp.dot(loc["q"][(c, h)], c_state[h].astype(BF16), preferred_element_type=F32) * scale for h in heads]
        qn = [jnp.sum(loc["q"][(c, h)].astype(F32) * n_state[h], axis=1, keepdims=True) * scale for h in heads]
        m_new = [jnp.maximum(loc["b_last"][(c, h)] + m_state[h], loc["g_max"][(c, h)]) for h in heads]
        decay = [jnp.exp(loc["b_last"][(c, h)] + m_state[h] - m_new[h]) for h in heads]
        beta = [jnp.exp(loc["g_max"][(c, h)] - m_new[h]) for h in heads]
        c_next = [decay[h] * c_state[h] + beta[h] * loc["u_loc"][(c, h)] for h in heads]
        n_next = [decay[h] * n_state[h] + beta[h] * loc["nk_loc"][(c, h)] for h in heads]
        num = [w_intra[h] * loc["a_loc"][(c, h)] + w_inter[h] * qc[h] for h in heads]
        den = [w_intra[h] * loc["r_loc"][(c, h)] + w_inter[h] * qn[h] for h in heads]
        hv = [num[h] * (1.0 / jnp.maximum(jnp.abs(den[h]), jnp.exp(-m_t[h]))) for h in heads]
        hv = [hv[h] * lax.rsqrt(jnp.mean(hv[h] * hv[h], axis=1, keepdims=True) + RMS_EPS) for h in heads]
        for h in heads:
            vs = slice(h * M_V_DIM, (h + 1) * M_V_DIM)
            mb_ref[rows(c), vs] = (osig_ref[rows(c), vs].astype(F32) * (hv[h] * mhg_ref[:, vs])).astype(mb_ref.dtype)
        return c_next, n_next, m_new

    c_state = [c_ref[h] for h in heads]
    n_state = [n_ref[h] for h in heads]
    m_state = [m_ref[h][:, 0:1] for h in heads]
    groups = [list(range(g, min(g + MLSTM_GROUP, chunks))) for g in range(0, chunks, MLSTM_GROUP)]
    loc = local_terms(groups[0])
    for gi, group in enumerate(groups):
        loc_next = local_terms(groups[gi + 1]) if gi + 1 < len(groups) else None
        for c in group:
            c_state, n_state, m_state = carried_step(c, loc, c_state, n_state, m_state)
        loc = loc_next

    for h in heads:
        c_ref[h], n_ref[h] = c_state[h], n_state[h]
        m_ref[h] = jnp.broadcast_to(m_state[h], (1, LANES))


def _mlstm(qkv, gates_col, gates_row, osig_src, mh_g, batch, seq, chunks=8):
    ts = chunks * CHUNK
    tiles = seq // ts
    n = batch * seq
    return pl.pallas_call(
        functools.partial(_mlstm_kernel, chunks=chunks),
        grid=(batch, tiles),
        in_specs=[
            pl.BlockSpec((ts, 2 * M_QK + M_V), lambda b, t: (b * tiles + t, 0)),
            pl.BlockSpec((ts, LANES), lambda b, t: (b * tiles + t, 0)),
            pl.BlockSpec((2 * M_HEADS, ts), lambda b, t: (0, b * tiles + t)),
            pl.BlockSpec((ts, M_V), lambda b, t: (b * tiles + t, 0)),
            pl.BlockSpec((1, M_V), lambda b, t: (0, 0)),
        ],
        out_specs=pl.BlockSpec((ts, M_V), lambda b, t: (b * tiles + t, 0)),
        out_shape=jax.ShapeDtypeStruct((n, M_V), BF16),
        scratch_shapes=[
            pltpu.VMEM((M_HEADS, M_QK_DIM, M_V_DIM), F32),
            pltpu.VMEM((M_HEADS, 1, M_QK_DIM), F32),
            pltpu.VMEM((M_HEADS, 1, LANES), F32),
        ],
        name="mlstm",
        compiler_params=_params("parallel", "arbitrary"),
    )(qkv, gates_col, gates_row, osig_src, mh_g)


def _merge_kernel(x_ref, xin_ref, a_ref, mb_ref, wga_ref, bga_ref, wgb_ref, bgb_ref, wpa_ref, wpb_ref, wout_ref,
                  g_ref, b_ref, xo_ref, xob_ref, xpa_ref, xpb_ref):
    xin = xin_ref[...]
    ga = jax.nn.sigmoid(jnp.dot(xin, wga_ref[...], preferred_element_type=F32) + bga_ref[...])
    gb = jax.nn.sigmoid(jnp.dot(xin, wgb_ref[...], preferred_element_type=F32) + bgb_ref[...])
    ya = jnp.dot(a_ref[...], wpa_ref[...], preferred_element_type=F32)
    yb = jnp.dot(mb_ref[...], wpb_ref[...], preferred_element_type=F32)
    y = ga * ya + gb * yb
    mix = jnp.dot(y.astype(BF16), wout_ref[...], preferred_element_type=F32)
    xn = _layer_norm(ALPHA * x_ref[...] + mix, g_ref[...], b_ref[...])
    xo_ref[...] = xn
    xob_ref[...] = xn.astype(BF16)
    xpa_ref[...], xpb_ref[...] = _pack_row(xn)


def _merge(x, xin, a, mb, w_ga, b_ga, w_gb, b_gb, w_pa, w_pb, w_out, ln_g, ln_b, x_row0=0, tm=512):
    n = xin.shape[0]
    row = lambda i: (i, 0)
    const = lambda i: (0, 0)
    wspec = pl.BlockSpec((D_MODEL, D_MODEL), const)
    vspec = pl.BlockSpec((1, D_MODEL), const)
    return pl.pallas_call(
        _merge_kernel,
        grid=(n // tm,),
        in_specs=[
            pl.BlockSpec((tm, D_MODEL), lambda i: (x_row0 // tm + i, 0)),
            pl.BlockSpec((tm, D_MODEL), row),
            pl.BlockSpec((tm, G_WIDTH), row),
            pl.BlockSpec((tm, M_V), row),
            wspec, vspec, wspec, vspec, wspec, wspec, wspec, vspec, vspec,
        ],
        out_specs=[pl.BlockSpec((tm, D_MODEL), row), pl.BlockSpec((tm, D_MODEL), row),
                   pl.BlockSpec((tm, PART), row), pl.BlockSpec((tm, PART), row)],
        out_shape=[jax.ShapeDtypeStruct((n, D_MODEL), F32), jax.ShapeDtypeStruct((n, D_MODEL), BF16),
                   jax.ShapeDtypeStruct((n, PART), I32), jax.ShapeDtypeStruct((n, PART), I32)],
        name="merge_ln",
        compiler_params=_params("parallel"),
    )(x, xin, a, mb, w_ga, b_ga, w_gb, b_gb, w_pa, w_pb, w_out, ln_g, ln_b)


def _first_max(v, idx, axes, sentinel):
    m = jnp.max(v, axis=axes, keepdims=True)
    first = jnp.min(jnp.where(v == m, idx, sentinel), axis=axes, keepdims=True)
    return m, first


def _router_kernel(x_ref, wr_ref, br_ref, upper_ref, eidx_ref, rank_ref, wtok_ref, cnt_ref, run_ref, *, tm):
    @pl.when(pl.program_id(0) == 0)
    def _():
        run_ref[...] = jnp.zeros_like(run_ref)

    tw = ROUTER_PART
    parts = range(tm // tw)
    shape3 = (N_GROUPS, GROUP_SIZE, tw)
    neg = -jnp.inf

    def split(v):
        hi = v.astype(BF16)
        return hi, (v - hi.astype(F32)).astype(BF16)

    nt = lambda a, b: lax.dot_general(a, b, (((1,), (1,)), ((), ())), preferred_element_type=F32)
    w_hi, w_lo = split(wr_ref[...])
    xs = [split(x_ref[i * tw:(i + 1) * tw, :]) for i in parts]
    logits = [nt(w_hi, x_hi) + (nt(w_hi, x_lo) + nt(w_lo, x_hi)) for x_hi, x_lo in xs]
    scores = [jax.nn.sigmoid(lg) for lg in logits]
    sel = [(sc + br_ref[...]).reshape(shape3) for sc in scores]
    scores3 = [sc.reshape(shape3) for sc in scores]
    member = lax.broadcasted_iota(jnp.int32, shape3, 1)
    group = lax.broadcasted_iota(jnp.int32, shape3, 0)

    best = [_first_max(v, member, 1, GROUP_SIZE) for v in sel]
    second = [jnp.max(jnp.where(member == f1, neg, v), axis=1, keepdims=True) for v, (_, f1) in zip(sel, best)]
    gscore = [m1 + m2 for (m1, _), m2 in zip(best, second)]
    gid = lax.broadcasted_iota(jnp.int32, gscore[0].shape, 0)
    gmask = [jnp.zeros(gscore[0].shape, dtype=jnp.bool_) for _ in parts]
    for _ in range(TOPK_GROUPS):
        hits = [gid == _first_max(g, gid, 0, N_GROUPS)[1] for g in gscore]
        gmask = [m | h for m, h in zip(gmask, hits)]
        gscore = [jnp.where(h, neg, g) for h, g in zip(hits, gscore)]

    cand = [jnp.where(m, v, neg) for m, v in zip(gmask, sel)]
    eid = group * GROUP_SIZE + member
    chosen = [jnp.zeros(shape3, dtype=jnp.bool_) for _ in parts]
    picks = [[] for _ in parts]
    for _ in range(TOP_K):
        firsts = [_first_max(c, eid, (0, 1), N_EXPERTS)[1] for c in cand]
        hits = [eid == fe for fe in firsts]
        chosen = [c | h for c, h in zip(chosen, hits)]
        cand = [jnp.where(h, neg, c) for h, c in zip(hits, cand)]
        picked = [jnp.sum(jnp.where(h, s3, 0.0), axis=(0, 1), keepdims=True)[0] for h, s3 in zip(hits, scores3)]
        for i in parts:
            picks[i].append((firsts[i][0], hits[i], picked[i]))

    chosen2 = [jnp.where(c, 1.0, 0.0).reshape(N_EXPERTS, tw) for c in chosen]
    counts = [jnp.sum(c2, axis=1, keepdims=True) for c2 in chosen2]
    bases = []
    base = run_ref[:, 0:1]
    for i in parts:
        bases.append(base)
        base = base + counts[i]
    prefix3 = [(jnp.dot(c2.astype(BF16), upper_ref[...], preferred_element_type=F32) + b0).reshape(shape3)
               for c2, b0 in zip(chosen2, bases)]

    for i in parts:
        cols = slice(i * tw, (i + 1) * tw)
        total = picks[i][0][2]
        for _, _, wk in picks[i][1:]:
            total = total + wk
        eidx_ref[:, cols] = jnp.concatenate([fe for fe, _, _ in picks[i]], axis=0)
        rank_ref[:, cols] = jnp.concatenate(
            [jnp.sum(jnp.where(hit, prefix3[i], 0.0), axis=(0, 1), keepdims=True)[0] for _, hit, _ in picks[i]],
            axis=0).astype(I32)
        w_rows = jnp.concatenate([wk / total * ROUTE_SCALE for _, _, wk in picks[i]]
                                 + [jnp.zeros((LANES - TOP_K, tw), F32)], axis=0)
        wtok_ref[cols, :] = w_rows.T

    run = run_ref[...] + (base - run_ref[:, 0:1])
    run_ref[...] = run
    cnt_ref[...] = run


def _router(x, wr_t, br, tm=1024):
    n = x.shape[0]
    tok = lambda i: (0, i)
    upper = (jnp.arange(ROUTER_PART)[:, None] < jnp.arange(ROUTER_PART)[None, :]).astype(BF16)
    return pl.pallas_call(
        functools.partial(_router_kernel, tm=tm),
        grid=(n // tm,),
        in_specs=[
            pl.BlockSpec((tm, D_MODEL), lambda i: (i, 0)),
            pl.BlockSpec((N_EXPERTS, D_MODEL), lambda i: (0, 0)),
            pl.BlockSpec((N_EXPERTS, 1), lambda i: (0, 0)),
            pl.BlockSpec((ROUTER_PART, ROUTER_PART), lambda i: (0, 0)),
        ],
        out_specs=[
            pl.BlockSpec((TOP_K, tm), tok),
            pl.BlockSpec((TOP_K, tm), tok),
            pl.BlockSpec((tm, LANES), lambda i: (i, 0)),
            pl.BlockSpec((N_EXPERTS, LANES), lambda i: (0, 0)),
        ],
        out_shape=[
            jax.ShapeDtypeStruct((TOP_K, n), I32),
            jax.ShapeDtypeStruct((TOP_K, n), I32),
            jax.ShapeDtypeStruct((n, LANES), F32),
            jax.ShapeDtypeStruct((N_EXPERTS, LANES), F32),
        ],
        scratch_shapes=[pltpu.VMEM((N_EXPERTS, LANES), F32)],
        name="router",
        compiler_params=_params("arbitrary"),
    )(x, wr_t, br, upper)


def _slots_kernel(starts_ref, eidx_ref, rank_ref, slot_ref):
    eidx = eidx_ref[...]
    slot = rank_ref[...]
    for e in range(N_EXPERTS):
        slot = slot + jnp.where(eidx == e, starts_ref[e], 0)
    slot_ref[...] = slot


def _slots(starts, eidx, rank, tn=2048):
    n = eidx.shape[1]
    tn = min(tn, n)
    spec = pl.BlockSpec((TOP_K, tn), lambda i, s: (0, i))
    return pl.pallas_call(
        _slots_kernel,
        grid_spec=pltpu.PrefetchScalarGridSpec(
            num_scalar_prefetch=1, grid=(n // tn,), in_specs=[spec, spec], out_specs=spec),
        out_shape=jax.ShapeDtypeStruct((TOP_K, n), I32),
        name="slots",
        compiler_params=_params("parallel"),
    )(starts, eidx, rank)


def _sc_mesh():
    return plsc.VectorSubcoreMesh(core_axis_name="c", subcore_axis_name="s")


def _sc_dispatch(parts, slots_flat, rows):
    n, width = parts[0].shape
    blocks = n // SC_WINDOW
    out_type = [jax.ShapeDtypeStruct((rows, width), part.dtype) for part in parts]

    @functools.partial(pl.kernel, out_type=out_type, mesh=_sc_mesh(), scratch_types=[], name="sc_dispatch")
    def run(*refs):
        i_hbm = refs[len(parts)]
        for x_hbm, o_hbm in zip(refs[:len(parts)], refs[len(parts) + 1:]):
            def body(x_vmem, i_vmem, o_hbm=o_hbm):
                pltpu.sync_copy(x_vmem, o_hbm.at[i_vmem.at[0]])

            pltpu.emit_pipeline(
                body,
                grid=(blocks, TOP_K),
                in_specs=[pl.BlockSpec((SC_WINDOW, width), lambda i, k: (i, 0)),
                          pl.BlockSpec((1, SC_WINDOW), lambda i, k: (0, k * blocks + i))],
                out_specs=[],
                core_axis_name=("c", "s"),
                dimension_semantics=(pltpu.PARALLEL, pltpu.ARBITRARY),
                trace_scopes=False,
            )(x_hbm, i_hbm)

    return run(*parts, slots_flat)


def _sc_gather(tables, idx_flat):
    count = idx_flat.shape[1]
    width = tables[0].shape[1]
    out_type = [jax.ShapeDtypeStruct((count, width), table.dtype) for table in tables]

    @functools.partial(pl.kernel, out_type=out_type, mesh=_sc_mesh(), scratch_types=[], name="sc_gather")
    def run(*refs):
        i_hbm = refs[len(tables)]
        for t_hbm, o_hbm in zip(refs[:len(tables)], refs[len(tables) + 1:]):
            def body(i_vmem, o_vmem, t_hbm=t_hbm):
                pltpu.sync_copy(t_hbm.at[i_vmem.at[0]], o_vmem)

            pltpu.emit_pipeline(
                body,
                grid=(count // SC_WINDOW,),
                in_specs=[pl.BlockSpec((1, SC_WINDOW), lambda i: (0, i))],
                out_specs=[pl.BlockSpec((SC_WINDOW, width), lambda i: (i, 0))],
                core_axis_name=("c", "s"),
                dimension_semantics=(pltpu.PARALLEL,),
                trace_scopes=False,
            )(i_hbm, o_hbm)

    return run(*tables, idx_flat)


def _expert_kernel(te_ref, used_ref, xa_ref, xb_ref, wg_ref, wu_ref, wd_ref, oa_ref, ob_ref, wgb_ref, wub_ref, wdb_ref):
    j = pl.program_id(0)

    @pl.when((j == 0) | (te_ref[j] != te_ref[jnp.maximum(j - 1, 0)]))
    def _():
        wgb_ref[...] = wg_ref[0].astype(BF16)
        wub_ref[...] = wu_ref[0].astype(BF16)
        wdb_ref[...] = wd_ref[0].astype(BF16)

    @pl.when(j < used_ref[0])
    def _():
        x = jnp.concatenate([_unpack_pairs(xa_ref[...]), _unpack_pairs(xb_ref[...])], axis=1).astype(BF16)
        hg = jnp.dot(x, wgb_ref[...], preferred_element_type=F32)
        hu = jnp.dot(x, wub_ref[...], preferred_element_type=F32)
        hid = (_silu(hg) * hu).astype(BF16)
        oa_ref[...], ob_ref[...] = _pack_row(jnp.dot(hid, wdb_ref[...], preferred_element_type=F32))


def _experts(tile_expert, tiles_used, xa, xb, layer, wg, wu, wd):
    rows = xa.shape[0]
    row = lambda j, te, used: (j, 0)
    exp = lambda j, te, used: (layer * N_EXPERTS + te[j], 0, 0)
    return pl.pallas_call(
        _expert_kernel,
        grid_spec=pltpu.PrefetchScalarGridSpec(
            num_scalar_prefetch=2,
            grid=(rows // ROW_TILE,),
            in_specs=[
                pl.BlockSpec((ROW_TILE, PART), row),
                pl.BlockSpec((ROW_TILE, PART), row),
                pl.BlockSpec((1, D_MODEL, D_EXPERT), exp),
                pl.BlockSpec((1, D_MODEL, D_EXPERT), exp),
                pl.BlockSpec((1, D_EXPERT, D_MODEL), exp),
            ],
            out_specs=[pl.BlockSpec((ROW_TILE, PART), row), pl.BlockSpec((ROW_TILE, PART), row)],
            scratch_shapes=[pltpu.VMEM((D_MODEL, D_EXPERT), BF16), pltpu.VMEM((D_MODEL, D_EXPERT), BF16),
                            pltpu.VMEM((D_EXPERT, D_MODEL), BF16)],
        ),
        out_shape=[jax.ShapeDtypeStruct((rows, PART), I32), jax.ShapeDtypeStruct((rows, PART), I32)],
        name="experts",
        compiler_params=_params("arbitrary"),
    )(tile_expert, tiles_used, xa, xb, wg, wu, wd)


def _finish_kernel(x_ref, xb_ref, p_ref, ga_ref, gb_ref, wtok_ref, wgs_ref, wus_ref, wds_ref, wpg_ref, bpg_ref, wp_ref,
                   lng_ref, lnb_ref, *out_refs):
    xb = xb_ref[...]
    hs = _silu(jnp.dot(xb, wgs_ref[...], preferred_element_type=F32)) * jnp.dot(
        xb, wus_ref[...], preferred_element_type=F32)
    shared = jnp.dot(hs.astype(BF16), wds_ref[...], preferred_element_type=F32)
    gate = jax.nn.sigmoid(jnp.dot(xb, wpg_ref[...], preferred_element_type=F32) + bpg_ref[...])
    ple = gate * jnp.dot(p_ref[0].astype(BF16), wp_ref[...], preferred_element_type=F32)

    wtok = wtok_ref[...]
    r_a = None
    r_b = None
    for k in range(TOP_K):
        wk = wtok[:, k:k + 1]
        a = wk * _unpack_pairs(ga_ref[k])
        b = wk * _unpack_pairs(gb_ref[k])
        r_a = a if r_a is None else r_a + a
        r_b = b if r_b is None else r_b + b
    routed = jnp.concatenate([r_a, r_b], axis=1)

    xn = _layer_norm(ALPHA * x_ref[...] + (routed + shared + ple), lng_ref[...], lnb_ref[...])
    if len(out_refs) == 2 and out_refs[1].dtype == BF16:
        out_refs[0][...] = xn
        out_refs[1][...] = xn.astype(BF16)
    else:
        out_refs[-1][...] = xn


def _finish(x, xb, p_all, layer, stream, ga, gb, wtok, wgs, wus, wds, wpg, bpg, wp, ln_g, ln_b,
            out_base=None, total_rows=None, tm=512):
    n = x.shape[0]
    p_block = stream * (n // tm)
    row = lambda i: (i, 0)
    const = lambda i: (0, 0)
    if total_rows is None:
        out_specs = [pl.BlockSpec((tm, D_MODEL), row), pl.BlockSpec((tm, D_MODEL), row)]
        out_shape = [jax.ShapeDtypeStruct((n, D_MODEL), F32), jax.ShapeDtypeStruct((n, D_MODEL), BF16)]
    else:
        out_specs = [pl.BlockSpec((tm, D_MODEL), lambda i: (p_block + i, 0))]
        out_shape = [jax.ShapeDtypeStruct((total_rows, D_MODEL), F32)]
    extra_specs = [] if out_base is None else [pl.BlockSpec(memory_space=pl.ANY)]
    extra_args = [] if out_base is None else [out_base]
    aliases = {} if out_base is None else {14: 0}
    return pl.pallas_call(
        _finish_kernel,
        grid=(n // tm,),
        input_output_aliases=aliases,
        in_specs=[
            pl.BlockSpec((tm, D_MODEL), row),
            pl.BlockSpec((tm, D_MODEL), row),
            pl.BlockSpec((1, tm, P_DIM), lambda i: (layer, p_block + i, 0)),
            pl.BlockSpec((TOP_K, tm, PART), lambda i: (0, i, 0)),
            pl.BlockSpec((TOP_K, tm, PART), lambda i: (0, i, 0)),
            pl.BlockSpec((tm, LANES), row),
            pl.BlockSpec((D_MODEL, D_SHARED), const),
            pl.BlockSpec((D_MODEL, D_SHARED), const),
            pl.BlockSpec((D_SHARED, D_MODEL), const),
            pl.BlockSpec((D_MODEL, D_MODEL), const),
            pl.BlockSpec((1, D_MODEL), const),
            pl.BlockSpec((P_DIM, D_MODEL), const),
            pl.BlockSpec((1, D_MODEL), const),
            pl.BlockSpec((1, D_MODEL), const),
        ] + extra_specs,
        out_specs=out_specs,
        out_shape=out_shape,
        name="moe_finish_ln",
        compiler_params=_params("parallel"),
    )(x, xb, p_all, ga, gb, wtok, wgs, wus, wds, wpg, bpg, wp, ln_g, ln_b, *extra_args)


def _route(x, xpa, xpb, w_router_t, b_router):
    n = x.shape[0]
    pairs = n * TOP_K
    tiles = pairs // ROW_TILE + N_EXPERTS
    eidx, rank, wtok, counts = _router(x, w_router_t, b_router)

    cnt = counts[:, 0].astype(I32)
    group_tiles = (cnt + ROW_TILE - 1) // ROW_TILE
    tile_end = jnp.cumsum(group_tiles)
    starts = (tile_end - group_tiles) * ROW_TILE
    tile_id = jnp.arange(tiles, dtype=I32)
    tile_expert = jnp.minimum(jnp.sum((tile_end[None, :] <= tile_id[:, None]).astype(I32), axis=1), N_EXPERTS - 1)

    slots_flat = _slots(starts, eidx, rank).reshape(1, pairs)
    xsa, xsb = _sc_dispatch([xpa, xpb], slots_flat, tiles * ROW_TILE)
    return dict(xsa=xsa, xsb=xsb, slots_flat=slots_flat, wtok=wtok, tile_expert=tile_expert,
                tiles_used=tile_end[N_EXPERTS - 1:])


class _IssueOrder:
    def __init__(self):
        self.last = None

    def before(self, operand):
        if self.last is None:
            return operand
        self.last, operand = lax.optimization_barrier((self.last, operand))
        return operand

    def after(self, result):
        self.last = result


def kernel(x, p, w_in, b_in, sg_ln_g, sg_ln_b, w_s, b_s, mh_g, w_pa, w_pb, w_out, ln1_g, ln1_b, w_router, b_router, w_gate_e, w_up_e, w_down_e, w_gate_s, w_up_s, w_down_s, w_pg, b_pg, w_p, ln2_g, ln2_b):
    batch, seq, _ = x.shape
    streams = STREAMS if batch % STREAMS == 0 else 1
    sb = batch // streams
    n = sb * seq
    x_all = x.reshape(batch * seq, D_MODEL)
    xf = [x_all] * streams
    x_row0 = [s * n for s in range(streams)]
    xb = [x_all[s * n:(s + 1) * n].astype(BF16) for s in range(streams)]
    p_all = p.reshape(DEPTH, batch * seq, P_DIM)

    wg_all = w_gate_e.reshape(DEPTH * N_EXPERTS, D_MODEL, D_EXPERT)
    wu_all = w_up_e.reshape(DEPTH * N_EXPERTS, D_MODEL, D_EXPERT)
    wd_all = w_down_e.reshape(DEPTH * N_EXPERTS, D_EXPERT, D_MODEL)

    c_uv = 2 * G_WIDTH
    c_qkv = c_uv + 2 * M_QK + M_V
    c_o = c_qkv + M_V
    c_if = c_o + 2 * M_HEADS
    c_gb = c_if + D_MODEL

    def layer_weights(l):
        w = w_in[l]
        b = b_in[l][None, :]
        return dict(
            sgu=(w[:, :G_WIDTH].astype(BF16), b[:, :G_WIDTH], w[:, G_WIDTH:c_uv].astype(BF16), b[:, G_WIDTH:c_uv],
                 sg_ln_g[l][None, :], sg_ln_b[l][None, :], w_s[l], b_s[l].T),
            mproj=(w[:, c_uv:c_qkv].astype(BF16), b[:, c_uv:c_qkv], w[:, c_qkv:c_o].astype(BF16), b[:, c_qkv:c_o],
                   jnp.pad(w[:, c_o:c_if], ((0, 0), (0, LANES - 2 * M_HEADS))).astype(BF16),
                   jnp.pad(b[:, c_o:c_if], ((0, 0), (0, LANES - 2 * M_HEADS)))),
            mhg=mh_g[l][None, :],
            merge=(w[:, c_if:c_gb].astype(BF16), b[:, c_if:c_gb], w[:, c_gb:].astype(BF16), b[:, c_gb:],
                   w_pa[l].astype(BF16), w_pb[l].astype(BF16), w_out[l].astype(BF16),
                   ln1_g[l][None, :], ln1_b[l][None, :]),
            router=(w_router[l].T, b_router[l][:, None]),
            finish=(w_gate_s[l].astype(BF16), w_up_s[l].astype(BF16), w_down_s[l].astype(BF16),
                    w_pg[l].astype(BF16), b_pg[l][None, :], w_p[l].astype(BF16),
                    ln2_g[l][None, :], ln2_b[l][None, :]))

    weights = [layer_weights(l) for l in range(DEPTH)]
    order = _IssueOrder()
    st = [dict() for _ in range(streams)]
    result = [None]

    def proj(s, l):
        st[s]["qkv"], st[s]["osig"], st[s]["gif"], st[s]["gif_rows"] = _mproj(
            order.before(xb[s]), *weights[l]["mproj"])
        order.after(st[s]["gif"])
        st[s]["a"] = _sgu(order.before(xb[s]), *weights[l]["sgu"])
        order.after(st[s]["a"])

    def recur(s, l):
        st[s]["mb"] = _mlstm(order.before(st[s]["qkv"]), st[s]["gif"], st[s]["gif_rows"], st[s]["osig"],
                             weights[l]["mhg"], sb, seq)
        order.after(st[s]["mb"])

    def merge(s, l):
        xf[s], xb[s], xpa, xpb = _merge(xf[s], order.before(xb[s]), st[s]["a"], st[s]["mb"], *weights[l]["merge"],
                                        x_row0=x_row0[s])
        x_row0[s] = 0
        st[s]["packed"] = (xpa, xpb)
        order.after(xpb)

    def route(s, l):
        st[s]["route"] = _route(order.before(xf[s]), *st[s]["packed"], *weights[l]["router"])
        order.after(st[s]["route"]["wtok"])

    def experts(s, l):
        r = st[s]["route"]
        ysa, ysb = _experts(r["tile_expert"], r["tiles_used"], order.before(r["xsa"]), r["xsb"], l,
                            wg_all, wu_all, wd_all)
        order.after(ysb)
        ga, gb = _sc_gather([ysa, ysb], r["slots_flat"])
        st[s]["gathered"] = (ga.reshape(TOP_K, n, PART), gb.reshape(TOP_K, n, PART))

    def finish(s, l):
        last = l == DEPTH - 1
        outs = _finish(order.before(xf[s]), xb[s], p_all, l, s, *st[s]["gathered"], st[s]["route"]["wtok"],
                       *weights[l]["finish"], out_base=result[0] if last else None,
                       total_rows=batch * seq if last else None)
        if last:
            result[0], = outs
        else:
            xf[s], xb[s] = outs
        order.after(outs[0])

    if streams == 1:
        for l in range(DEPTH):
            proj(0, l), recur(0, l), merge(0, l), route(0, l), experts(0, l), finish(0, l)
    else:
        proj(0, 0), recur(0, 0), merge(0, 0), route(0, 0), proj(1, 0)
        for l in range(DEPTH):
            more = l + 1 < DEPTH
            experts(0, l), recur(1, l), merge(1, l)
            if more:
                finish(0, l), route(1, l), proj(0, l + 1)
            else:
                route(1, l), finish(0, l)
            experts(1, l)
            if more:
                recur(0, l + 1), merge(0, l + 1)
            finish(1, l)
            if more:
                route(0, l + 1), proj(1, l + 1)
    return result[0].reshape(batch, seq, D_MODEL)
```

```python
import functools

import jax
import jax.numpy as jnp
from jax import lax
from jax.experimental import pallas as pl
from jax.experimental.pallas import tpu as pltpu
from jax.experimental.pallas import tpu_sc as plsc

D_MODEL = 1024
DEPTH = 4
CHUNK = 64
P_DIM = 256
G_WIDTH = 1024
G_GROUPS = 8
G_GROUP_DIM = G_WIDTH // G_GROUPS
G_BLOCK = 128
M_HEADS = 4
M_QK_DIM = 128
M_V_DIM = 256
M_QK = M_HEADS * M_QK_DIM
M_V = M_HEADS * M_V_DIM
N_EXPERTS = 64
TOP_K = 8
N_GROUPS = 8
TOPK_GROUPS = 4
GROUP_SIZE = N_EXPERTS // N_GROUPS
D_EXPERT = 256
D_SHARED = 256
ROUTE_SCALE = 2.5
ALPHA = (2 * DEPTH) ** 0.25
LN_EPS = 1e-5
RMS_EPS = 1e-6

LANES = 128
VMEM_LIMIT = 56 * 1024 * 1024
HALF = D_MODEL // 2
PART = HALF // 2
ROW_TILE = 1024
STREAMS = 2
MERGE_PARTS = 2
ROUTER_PART = 512
MLSTM_GROUP = 8
SC_WINDOW = 128

F32 = jnp.float32
BF16 = jnp.bfloat16
I32 = jnp.int32


def _params(*semantics):
    return pltpu.CompilerParams(dimension_semantics=semantics, vmem_limit_bytes=VMEM_LIMIT)


def _layer_norm(x, g, b):
    mu = jnp.mean(x, axis=-1, keepdims=True)
    xc = x - mu
    var = jnp.mean(xc * xc, axis=-1, keepdims=True)
    return xc * lax.rsqrt(var + LN_EPS) * g + b


def _gelu(x):
    return 0.5 * x * (1.0 + lax.erf(x * (2.0 ** -0.5)))


def _silu(x):
    return x * jax.nn.sigmoid(x)


def _pack_pairs(x):
    lo = lax.bitcast_convert_type(x[:, :PART].astype(BF16).astype(F32), I32)
    hi = lax.bitcast_convert_type(x[:, PART:].astype(BF16).astype(F32), I32)
    return lax.shift_right_logical(lo, 16) | (hi & jnp.int32(-65536))


def _pack_row(x):
    return _pack_pairs(x[:, :HALF]), _pack_pairs(x[:, HALF:])


def _unpack_pairs(w):
    lo = lax.bitcast_convert_type(lax.shift_left(w, 16), F32)
    hi = lax.bitcast_convert_type(w & jnp.int32(-65536), F32)
    return jnp.concatenate([lo, hi], axis=1)


def _mproj_kernel(x_ref, wqkv_ref, bqkv_ref, wo_ref, bo_ref, wif_ref, bif_ref, qkv_ref, osig_ref, gif_ref, gifr_ref):
    x = x_ref[...]
    qkv_ref[...] = (jnp.dot(x, wqkv_ref[...], preferred_element_type=F32) + bqkv_ref[...]).astype(qkv_ref.dtype)
    osig_ref[...] = jax.nn.sigmoid(
        jnp.dot(x, wo_ref[...], preferred_element_type=F32) + bo_ref[...]).astype(osig_ref.dtype)
    gif = jnp.dot(x, wif_ref[...], preferred_element_type=F32) + bif_ref[...]
    gif_ref[...] = gif
    gifr_ref[...] = gif.T[:2 * M_HEADS, :]


def _mproj(x, w_qkv, b_qkv, w_o, b_o, w_if, b_if, tm=1024):
    n, k = x.shape
    row = lambda i: (i, 0)
    const = lambda i: (0, 0)
    widths = (w_qkv.shape[1], w_o.shape[1], w_if.shape[1])
    in_specs = [pl.BlockSpec((tm, k), row)]
    for width in widths:
        in_specs += [pl.BlockSpec((k, width), const), pl.BlockSpec((1, width), const)]
    return pl.pallas_call(
        _mproj_kernel,
        grid=(n // tm,),
        in_specs=in_specs,
        out_specs=[pl.BlockSpec((tm, width), row) for width in widths]
        + [pl.BlockSpec((2 * M_HEADS, tm), lambda i: (0, i))],
        out_shape=[jax.ShapeDtypeStruct((n, widths[0]), BF16), jax.ShapeDtypeStruct((n, widths[1]), BF16),
                   jax.ShapeDtypeStruct((n, widths[2]), F32), jax.ShapeDtypeStruct((2 * M_HEADS, n), F32)],
        name="mlstm_proj",
        compiler_params=_params("parallel"),
    )(x, w_qkv, b_qkv, w_o, b_o, w_if, b_if)


def _sgu_kernel(x_ref, wu_ref, bu_ref, wv_ref, bv_ref, lng_ref, lnb_ref, ws_ref, bs_ref, a_ref, *, blocks):
    x = x_ref[...]
    v = _gelu(jnp.dot(x, wv_ref[...], preferred_element_type=F32) + bv_ref[...])
    vln = _layer_norm(v, lng_ref[...], lnb_ref[...]).astype(BF16)
    u = _gelu(jnp.dot(x, wu_ref[...], preferred_element_type=F32) + bu_ref[...])
    t_chunk = lax.broadcasted_iota(jnp.int32, (G_BLOCK, G_BLOCK), 0) // CHUNK
    s_chunk = lax.broadcasted_iota(jnp.int32, (G_BLOCK, G_BLOCK), 1) // CHUNK
    causal = s_chunk <= t_chunk
    for g in range(G_GROUPS):
        w = jnp.where(causal, ws_ref[g], 0.0).astype(BF16)
        bias = bs_ref[:, g:g + 1]
        cs = slice(g * G_GROUP_DIM, (g + 1) * G_GROUP_DIM)
        for blk in range(blocks):
            rs = slice(blk * G_BLOCK, (blk + 1) * G_BLOCK)
            mixed = jnp.dot(w, vln[rs, cs], preferred_element_type=F32) + bias
            a_ref[rs, cs] = (u[rs, cs] * mixed).astype(a_ref.dtype)


def _sgu(x, w_u, b_u, w_v, b_v, ln_g, ln_b, w_s, b_s_t, blocks=8):
    n = x.shape[0]
    tp = blocks * G_BLOCK
    const = lambda i: (0, 0)
    wspec = pl.BlockSpec((D_MODEL, G_WIDTH), const)
    vspec = pl.BlockSpec((1, G_WIDTH), const)
    return pl.pallas_call(
        functools.partial(_sgu_kernel, blocks=blocks),
        grid=(n // tp,),
        in_specs=[
            pl.BlockSpec((tp, D_MODEL), lambda i: (i, 0)),
            wspec, vspec, wspec, vspec, vspec, vspec,
            pl.BlockSpec((G_GROUPS, G_BLOCK, G_BLOCK), lambda i: (0, 0, 0)),
            pl.BlockSpec((G_BLOCK, G_GROUPS), const),
        ],
        out_specs=pl.BlockSpec((tp, G_WIDTH), lambda i: (i, 0)),
        out_shape=jax.ShapeDtypeStruct((n, G_WIDTH), BF16),
        name="spatial_gating",
        compiler_params=_params("parallel"),
    )(x, w_u, b_u, w_v, b_v, ln_g, ln_b, w_s, b_s_t)


def _segment_cumsum(x, axis, seg):
    pos = lax.broadcasted_iota(jnp.int32, x.shape, axis) % seg
    shift = 1
    while shift < seg:
        x = x + jnp.where(pos >= shift, pltpu.roll(x, shift, axis), 0.0)
        shift *= 2
    return x


def _mlstm_kernel(qkv_ref, gc_ref, gr_ref, osig_ref, mhg_ref, mb_ref, c_ref, n_ref, m_ref, *, chunks):
    @pl.when(pl.program_id(1) == 0)
    def _():
        c_ref[...] = jnp.zeros_like(c_ref)
        n_ref[...] = jnp.zeros_like(n_ref)
        m_ref[...] = jnp.zeros_like(m_ref)

    scale = M_QK_DIM ** -0.5
    gc = gc_ref[...]
    gr = gr_ref[...]
    b_cols = _segment_cumsum(jax.nn.log_sigmoid(gc), 0, CHUNK)
    b_rows = _segment_cumsum(jax.nn.log_sigmoid(gr), 1, CHUNK)
    t_idx = lax.broadcasted_iota(jnp.int32, (CHUNK, CHUNK), 0)
    s_idx = lax.broadcasted_iota(jnp.int32, (CHUNK, CHUNK), 1)
    tri = s_idx <= t_idx

    heads = range(M_HEADS)
    rows = lambda c: slice(c * CHUNK, (c + 1) * CHUNK)

    def local_terms(cs):
        units = [(c, h) for c in cs for h in heads]
        q = {u: qkv_ref[rows(u[0]), u[1] * M_QK_DIM:(u[1] + 1) * M_QK_DIM] for u in units}
        k = {u: qkv_ref[rows(u[0]), M_QK + u[1] * M_QK_DIM:M_QK + (u[1] + 1) * M_QK_DIM] for u in units}
        v = {u: qkv_ref[rows(u[0]), 2 * M_QK + u[1] * M_V_DIM:2 * M_QK + (u[1] + 1) * M_V_DIM] for u in units}
        i_col = {u: gc[rows(u[0]), u[1]:u[1] + 1] for u in units}
        b_col = {u: b_cols[rows(u[0]), M_HEADS + u[1]:M_HEADS + u[1] + 1] for u in units}
        ib_row = {u: (b_rows[M_HEADS + u[1]:M_HEADS + u[1] + 1, rows(u[0])]
                      - gr[u[1]:u[1] + 1, rows(u[0])]) for u in units}
        d = {u: jnp.where(tri, b_col[u] - ib_row[u], -jnp.inf) for u in units}
        d_max = {u: jnp.max(d[u], axis=1, keepdims=True) for u in units}
        qk = {u: lax.dot_general(q[u], k[u], (((1,), (1,)), ((), ())), preferred_element_type=F32) for u in units}
        s_loc = {u: qk[u] * scale * jnp.exp(d[u] - d_max[u]) for u in units}
        a_loc = {u: jnp.dot(s_loc[u].astype(BF16), v[u], preferred_element_type=F32) for u in units}
        r_loc = {u: jnp.sum(s_loc[u], axis=1, keepdims=True) for u in units}
        b_last = {u: b_col[u][CHUNK - 1:CHUNK, :] for u in units}
        g_col = {u: b_last[u] - b_col[u] + i_col[u] for u in units}
        g_max = {u: jnp.max(g_col[u], axis=0, keepdims=True) for u in units}
        wk = {u: jnp.exp(g_col[u] - g_max[u]) * k[u].astype(F32) for u in units}
        u_loc = {u: jnp.dot(wk[u].T.astype(BF16), v[u], preferred_element_type=F32) for u in units}
        nk_loc = {u: jnp.sum(wk[u], axis=0, keepdims=True) for u in units}
        return dict(q=q, b_col=b_col, d_max=d_max, a_loc=a_loc, r_loc=r_loc, b_last=b_last, g_max=g_max,
                    u_loc=u_loc, nk_loc=nk_loc)

    def carried_step(c, loc, c_state, n_state, m_state):
        inter = [loc["b_col"][(c, h)] + m_state[h] for h in heads]
        m_t = [jnp.maximum(inter[h], loc["d_max"][(c, h)]) for h in heads]
        w_intra = [jnp.exp(loc["d_max"][(c, h)] - m_t[h]) for h in heads]
        w_inter = [jnp.exp(inter[h] - m_t[h]) for h in heads]
        qc = [jnp.dot(loc["q"][(c, h)], c_state[h].astype(BF16), preferred_element_type=F32) * scale for h in heads]
        qn = [jnp.sum(loc["q"][(c, h)].astype(F32) * n_state[h], axis=1, keepdims=True) * scale for h in heads]
        m_new = [jnp.maximum(loc["b_last"][(c, h)] + m_state[h], loc["g_max"][(c, h)]) for h in heads]
        decay = [jnp.exp(loc["b_last"][(c, h)] + m_state[h] - m_new[h]) for h in heads]
        beta = [jnp.exp(loc["g_max"][(c, h)] - m_new[h]) for h in heads]
        c_next = [decay[h] * c_state[h] + beta[h] * loc["u_loc"][(c, h)] for h in heads]
        n_next = [decay[h] * n_state[h] + beta[h] * loc["nk_loc"][(c, h)] for h in heads]
        num = [w_intra[h] * loc["a_loc"][(c, h)] + w_inter[h] * qc[h] for h in heads]
        den = [w_intra[h] * loc["r_loc"][(c, h)] + w_inter[h] * qn[h] for h in heads]
        hv = [num[h] * (1.0 / jnp.maximum(jnp.abs(den[h]), jnp.exp(-m_t[h]))) for h in heads]
        hv = [hv[h] * lax.rsqrt(jnp.mean(hv[h] * hv[h], axis=1, keepdims=True) + RMS_EPS) for h in heads]
        for h in heads:
            vs = slice(h * M_V_DIM, (h + 1) * M_V_DIM)
            mb_ref[rows(c), vs] = (osig_ref[rows(c), vs].astype(F32) * (hv[h] * mhg_ref[:, vs])).astype(mb_ref.dtype)
        return c_next, n_next, m_new

    c_state = [c_ref[h] for h in heads]
    n_state = [n_ref[h] for h in heads]
    m_state = [m_ref[h][:, 0:1] for h in heads]
    groups = [list(range(g, min(g + MLSTM_GROUP, chunks))) for g in range(0, chunks, MLSTM_GROUP)]
    loc = local_terms(groups[0])
    for gi, group in enumerate(groups):
        loc_next = local_terms(groups[gi + 1]) if gi + 1 < len(groups) else None
        for c in group:
            c_state, n_state, m_state = carried_step(c, loc, c_state, n_state, m_state)
        loc = loc_next

    for h in heads:
        c_ref[h], n_ref[h] = c_state[h], n_state[h]
        m_ref[h] = jnp.broadcast_to(m_state[h], (1, LANES))


def _mlstm(qkv, gates_col, gates_row, osig_src, mh_g, batch, seq, chunks=8):
    ts = chunks * CHUNK
    tiles = seq // ts
    n = batch * seq
    return pl.pallas_call(
        functools.partial(_mlstm_kernel, chunks=chunks),
        grid=(batch, tiles),
        in_specs=[
            pl.BlockSpec((ts, 2 * M_QK + M_V), lambda b, t: (b * tiles + t, 0)),
            pl.BlockSpec((ts, LANES), lambda b, t: (b * tiles + t, 0)),
            pl.BlockSpec((2 * M_HEADS, ts), lambda b, t: (0, b * tiles + t)),
            pl.BlockSpec((ts, M_V), lambda b, t: (b * tiles + t, 0)),
            pl.BlockSpec((1, M_V), lambda b, t: (0, 0)),
        ],
        out_specs=pl.BlockSpec((ts, M_V), lambda b, t: (b * tiles + t, 0)),
        out_shape=jax.ShapeDtypeStruct((n, M_V), BF16),
        scratch_shapes=[
            pltpu.VMEM((M_HEADS, M_QK_DIM, M_V_DIM), F32),
            pltpu.VMEM((M_HEADS, 1, M_QK_DIM), F32),
            pltpu.VMEM((M_HEADS, 1, LANES), F32),
        ],
        name="mlstm",
        compiler_params=_params("parallel", "arbitrary"),
    )(qkv, gates_col, gates_row, osig_src, mh_g)


def _merge_kernel(x_ref, xin_ref, a_ref, mb_ref, wga_ref, bga_ref, wgb_ref, bgb_ref, wpa_ref, wpb_ref, wout_ref,
                  g_ref, b_ref, xo_ref, xob_ref, xpa_ref, xpb_ref):
    tm = x_ref.shape[0]
    halves = [slice(r, r + tm // MERGE_PARTS) for r in range(0, tm, tm // MERGE_PARTS)]
    dot = lambda lhs, w_ref: jnp.dot(lhs, w_ref[...], preferred_element_type=F32)
    ga = [jax.nn.sigmoid(dot(xin_ref[rs, :], wga_ref) + bga_ref[...]) for rs in halves]
    ya = [dot(a_ref[rs, :], wpa_ref) for rs in halves]
    gb = [jax.nn.sigmoid(dot(xin_ref[rs, :], wgb_ref) + bgb_ref[...]) for rs in halves]
    yb = [dot(mb_ref[rs, :], wpb_ref) for rs in halves]
    y = [(ga[h] * ya[h] + gb[h] * yb[h]).astype(BF16) for h in range(MERGE_PARTS)]
    mix = [dot(y[h], wout_ref) for h in range(MERGE_PARTS)]
    for h, rs in enumerate(halves):
        xn = _layer_norm(ALPHA * x_ref[rs, :] + mix[h], g_ref[...], b_ref[...])
        xo_ref[rs, :] = xn
        xob_ref[rs, :] = xn.astype(BF16)
        xpa_ref[rs, :], xpb_ref[rs, :] = _pack_row(xn)


def _merge(x, xin, a, mb, w_ga, b_ga, w_gb, b_gb, w_pa, w_pb, w_out, ln_g, ln_b, x_row0=0, tm=512):
    n = xin.shape[0]
    row = lambda i: (i, 0)
    const = lambda i: (0, 0)
    wspec = pl.BlockSpec((D_MODEL, D_MODEL), const)
    vspec = pl.BlockSpec((1, D_MODEL), const)
    return pl.pallas_call(
        _merge_kernel,
        grid=(n // tm,),
        in_specs=[
            pl.BlockSpec((tm, D_MODEL), lambda i: (x_row0 // tm + i, 0)),
            pl.BlockSpec((tm, D_MODEL), row),
            pl.BlockSpec((tm, G_WIDTH), row),
            pl.BlockSpec((tm, M_V), row),
            wspec, vspec, wspec, vspec, wspec, wspec, wspec, vspec, vspec,
        ],
        out_specs=[pl.BlockSpec((tm, D_MODEL), row), pl.BlockSpec((tm, D_MODEL), row),
                   pl.BlockSpec((tm, PART), row), pl.BlockSpec((tm, PART), row)],
        out_shape=[jax.ShapeDtypeStruct((n, D_MODEL), F32), jax.ShapeDtypeStruct((n, D_MODEL), BF16),
                   jax.ShapeDtypeStruct((n, PART), I32), jax.ShapeDtypeStruct((n, PART), I32)],
        name="merge_ln",
        compiler_params=_params("parallel"),
    )(x, xin, a, mb, w_ga, b_ga, w_gb, b_gb, w_pa, w_pb, w_out, ln_g, ln_b)


def _first_max(v, idx, axes, sentinel):
    m = jnp.max(v, axis=axes, keepdims=True)
    first = jnp.min(jnp.where(v == m, idx, sentinel), axis=axes, keepdims=True)
    return m, first


def _router_kernel(x_ref, wr_ref, br_ref, upper_ref, eidx_ref, rank_ref, wtok_ref, cnt_ref, run_ref, *, tm):
    @pl.when(pl.program_id(0) == 0)
    def _():
        run_ref[...] = jnp.zeros_like(run_ref)

    tw = ROUTER_PART
    parts = range(tm // tw)
    shape3 = (N_GROUPS, GROUP_SIZE, tw)
    neg = -jnp.inf

    def split(v):
        hi = v.astype(BF16)
        return hi, (v - hi.astype(F32)).astype(BF16)

    nt = lambda a, b: lax.dot_general(a, b, (((1,), (1,)), ((), ())), preferred_element_type=F32)
    w_hi, w_lo = split(wr_ref[...])
    xs = [split(x_ref[i * tw:(i + 1) * tw, :]) for i in parts]
    logits = [nt(w_hi, x_hi) + (nt(w_hi, x_lo) + nt(w_lo, x_hi)) for x_hi, x_lo in xs]
    scores = [jax.nn.sigmoid(lg) for lg in logits]
    sel = [(sc + br_ref[...]).reshape(shape3) for sc in scores]
    scores3 = [sc.reshape(shape3) for sc in scores]
    member = lax.broadcasted_iota(jnp.int32, shape3, 1)
    group = lax.broadcasted_iota(jnp.int32, shape3, 0)

    best = [_first_max(v, member, 1, GROUP_SIZE) for v in sel]
    second = [jnp.max(jnp.where(member == f1, neg, v), axis=1, keepdims=True) for v, (_, f1) in zip(sel, best)]
    gscore = [m1 + m2 for (m1, _), m2 in zip(best, second)]
    gid = lax.broadcasted_iota(jnp.int32, gscore[0].shape, 0)
    gmask = [jnp.zeros(gscore[0].shape, dtype=jnp.bool_) for _ in parts]
    for _ in range(TOPK_GROUPS):
        hits = [gid == _first_max(g, gid, 0, N_GROUPS)[1] for g in gscore]
        gmask = [m | h for m, h in zip(gmask, hits)]
        gscore = [jnp.where(h, neg, g) for h, g in zip(hits, gscore)]

    cand = [jnp.where(m, v, neg) for m, v in zip(gmask, sel)]
    eid = group * GROUP_SIZE + member
    chosen = [jnp.zeros(shape3, dtype=jnp.bool_) for _ in parts]
    picks = [[] for _ in parts]
    for _ in range(TOP_K):
        firsts = [_first_max(c, eid, (0, 1), N_EXPERTS)[1] for c in cand]
        hits = [eid == fe for fe in firsts]
        chosen = [c | h for c, h in zip(chosen, hits)]
        cand = [jnp.where(h, neg, c) for h, c in zip(hits, cand)]
        picked = [jnp.sum(jnp.where(h, s3, 0.0), axis=(0, 1), keepdims=True)[0] for h, s3 in zip(hits, scores3)]
        for i in parts:
            picks[i].append((firsts[i][0], hits[i], picked[i]))

    chosen2 = [jnp.where(c, 1.0, 0.0).reshape(N_EXPERTS, tw) for c in chosen]
    counts = [jnp.sum(c2, axis=1, keepdims=True) for c2 in chosen2]
    bases = []
    base = run_ref[:, 0:1]
    for i in parts:
        bases.append(base)
        base = base + counts[i]
    prefix3 = [(jnp.dot(c2.astype(BF16), upper_ref[...], preferred_element_type=F32) + b0).reshape(shape3)
               for c2, b0 in zip(chosen2, bases)]

    for i in parts:
        cols = slice(i * tw, (i + 1) * tw)
        total = picks[i][0][2]
        for _, _, wk in picks[i][1:]:
            total = total + wk
        eidx_ref[:, cols] = jnp.concatenate([fe for fe, _, _ in picks[i]], axis=0)
        rank_ref[:, cols] = jnp.concatenate(
            [jnp.sum(jnp.where(hit, prefix3[i], 0.0), axis=(0, 1), keepdims=True)[0] for _, hit, _ in picks[i]],
            axis=0).astype(I32)
        w_rows = jnp.concatenate([wk / total * ROUTE_SCALE for _, _, wk in picks[i]]
                                 + [jnp.zeros((LANES - TOP_K, tw), F32)], axis=0)
        wtok_ref[cols, :] = w_rows.T

    run = run_ref[...] + (base - run_ref[:, 0:1])
    run_ref[...] = run
    cnt_ref[...] = run


def _router(x, wr_t, br, tm=1024):
    n = x.shape[0]
    tok = lambda i: (0, i)
    upper = (jnp.arange(ROUTER_PART)[:, None] < jnp.arange(ROUTER_PART)[None, :]).astype(BF16)
    return pl.pallas_call(
        functools.partial(_router_kernel, tm=tm),
        grid=(n // tm,),
        in_specs=[
            pl.BlockSpec((tm, D_MODEL), lambda i: (i, 0)),
            pl.BlockSpec((N_EXPERTS, D_MODEL), lambda i: (0, 0)),
            pl.BlockSpec((N_EXPERTS, 1), lambda i: (0, 0)),
            pl.BlockSpec((ROUTER_PART, ROUTER_PART), lambda i: (0, 0)),
        ],
        out_specs=[
            pl.BlockSpec((TOP_K, tm), tok),
            pl.BlockSpec((TOP_K, tm), tok),
            pl.BlockSpec((tm, LANES), lambda i: (i, 0)),
            pl.BlockSpec((N_EXPERTS, LANES), lambda i: (0, 0)),
        ],
        out_shape=[
            jax.ShapeDtypeStruct((TOP_K, n), I32),
            jax.ShapeDtypeStruct((TOP_K, n), I32),
            jax.ShapeDtypeStruct((n, LANES), F32),
            jax.ShapeDtypeStruct((N_EXPERTS, LANES), F32),
        ],
        scratch_shapes=[pltpu.VMEM((N_EXPERTS, LANES), F32)],
        name="router",
        compiler_params=_params("arbitrary"),
    )(x, wr_t, br, upper)


def _slots_kernel(starts_ref, eidx_ref, rank_ref, slot_ref):
    eidx = eidx_ref[...]
    slot = rank_ref[...]
    for e in range(N_EXPERTS):
        slot = slot + jnp.where(eidx == e, starts_ref[e], 0)
    slot_ref[...] = slot


def _slots(starts, eidx, rank, tn=2048):
    n = eidx.shape[1]
    tn = min(tn, n)
    spec = pl.BlockSpec((TOP_K, tn), lambda i, s: (0, i))
    return pl.pallas_call(
        _slots_kernel,
        grid_spec=pltpu.PrefetchScalarGridSpec(
            num_scalar_prefetch=1, grid=(n // tn,), in_specs=[spec, spec], out_specs=spec),
        out_shape=jax.ShapeDtypeStruct((TOP_K, n), I32),
        name="slots",
        compiler_params=_params("parallel"),
    )(starts, eidx, rank)


def _sc_mesh():
    return plsc.VectorSubcoreMesh(core_axis_name="c", subcore_axis_name="s")


def _sc_dispatch(parts, slots_flat, rows):
    n, width = parts[0].shape
    blocks = n // SC_WINDOW
    out_type = [jax.ShapeDtypeStruct((rows, width), part.dtype) for part in parts]

    @functools.partial(pl.kernel, out_type=out_type, mesh=_sc_mesh(), scratch_types=[], name="sc_dispatch")
    def run(*refs):
        i_hbm = refs[len(parts)]
        for x_hbm, o_hbm in zip(refs[:len(parts)], refs[len(parts) + 1:]):
            def body(x_vmem, i_vmem, o_hbm=o_hbm):
                pltpu.sync_copy(x_vmem, o_hbm.at[i_vmem.at[0]])

            pltpu.emit_pipeline(
                body,
                grid=(blocks, TOP_K),
                in_specs=[pl.BlockSpec((SC_WINDOW, width), lambda i, k: (i, 0)),
                          pl.BlockSpec((1, SC_WINDOW), lambda i, k: (0, k * blocks + i))],
                out_specs=[],
                core_axis_name=("c", "s"),
                dimension_semantics=(pltpu.PARALLEL, pltpu.ARBITRARY),
                trace_scopes=False,
            )(x_hbm, i_hbm)

    return run(*parts, slots_flat)


def _sc_gather(tables, idx_flat):
    count = idx_flat.shape[1]
    width = tables[0].shape[1]
    out_type = [jax.ShapeDtypeStruct((count, width), table.dtype) for table in tables]

    @functools.partial(pl.kernel, out_type=out_type, mesh=_sc_mesh(), scratch_types=[], name="sc_gather")
    def run(*refs):
        i_hbm = refs[len(tables)]
        for t_hbm, o_hbm in zip(refs[:len(tables)], refs[len(tables) + 1:]):
            def body(i_vmem, o_vmem, t_hbm=t_hbm):
                pltpu.sync_copy(t_hbm.at[i_vmem.at[0]], o_vmem)

            pltpu.emit_pipeline(
                body,
                grid=(count // SC_WINDOW,),
                in_specs=[pl.BlockSpec((1, SC_WINDOW), lambda i: (0, i))],
                out_specs=[pl.BlockSpec((SC_WINDOW, width), lambda i: (i, 0))],
                core_axis_name=("c", "s"),
                dimension_semantics=(pltpu.PARALLEL,),
                trace_scopes=False,
            )(i_hbm, o_hbm)

    return run(*tables, idx_flat)


def _expert_kernel(te_ref, used_ref, xa_ref, xb_ref, wg_ref, wu_ref, wd_ref, oa_ref, ob_ref, wgb_ref, wub_ref, wdb_ref):
    j = pl.program_id(0)

    @pl.when((j == 0) | (te_ref[j] != te_ref[jnp.maximum(j - 1, 0)]))
    def _():
        wgb_ref[...] = wg_ref[0].astype(BF16)
        wub_ref[...] = wu_ref[0].astype(BF16)
        wdb_ref[...] = wd_ref[0].astype(BF16)

    @pl.when(j < used_ref[0])
    def _():
        x = jnp.concatenate([_unpack_pairs(xa_ref[...]), _unpack_pairs(xb_ref[...])], axis=1).astype(BF16)
        hg = jnp.dot(x, wgb_ref[...], preferred_element_type=F32)
        hu = jnp.dot(x, wub_ref[...], preferred_element_type=F32)
        hid = (_silu(hg) * hu).astype(BF16)
        oa_ref[...], ob_ref[...] = _pack_row(jnp.dot(hid, wdb_ref[...], preferred_element_type=F32))


def _experts(tile_expert, tiles_used, xa, xb, layer, wg, wu, wd):
    rows = xa.shape[0]
    row = lambda j, te, used: (j, 0)
    exp = lambda j, te, used: (layer * N_EXPERTS + te[j], 0, 0)
    return pl.pallas_call(
        _expert_kernel,
        grid_spec=pltpu.PrefetchScalarGridSpec(
            num_scalar_prefetch=2,
            grid=(rows // ROW_TILE,),
            in_specs=[
                pl.BlockSpec((ROW_TILE, PART), row),
                pl.BlockSpec((ROW_TILE, PART), row),
                pl.BlockSpec((1, D_MODEL, D_EXPERT), exp),
                pl.BlockSpec((1, D_MODEL, D_EXPERT), exp),
                pl.BlockSpec((1, D_EXPERT, D_MODEL), exp),
            ],
            out_specs=[pl.BlockSpec((ROW_TILE, PART), row), pl.BlockSpec((ROW_TILE, PART), row)],
            scratch_shapes=[pltpu.VMEM((D_MODEL, D_EXPERT), BF16), pltpu.VMEM((D_MODEL, D_EXPERT), BF16),
                            pltpu.VMEM((D_EXPERT, D_MODEL), BF16)],
        ),
        out_shape=[jax.ShapeDtypeStruct((rows, PART), I32), jax.ShapeDtypeStruct((rows, PART), I32)],
        name="experts",
        compiler_params=_params("arbitrary"),
    )(tile_expert, tiles_used, xa, xb, wg, wu, wd)


def _finish_kernel(x_ref, p_ref, ga_ref, gb_ref, wtok_ref, wgs_ref, wus_ref, wds_ref, wpg_ref, bpg_ref, wp_ref,
                   lng_ref, lnb_ref, *out_refs):
    xb = x_ref[...].astype(BF16)
    hs = _silu(jnp.dot(xb, wgs_ref[...], preferred_element_type=F32)) * jnp.dot(
        xb, wus_ref[...], preferred_element_type=F32)
    shared = jnp.dot(hs.astype(BF16), wds_ref[...], preferred_element_type=F32)
    gate = jax.nn.sigmoid(jnp.dot(xb, wpg_ref[...], preferred_element_type=F32) + bpg_ref[...])
    ple = gate * jnp.dot(p_ref[0].astype(BF16), wp_ref[...], preferred_element_type=F32)

    wtok = wtok_ref[...]
    r_a = None
    r_b = None
    for k in range(TOP_K):
        wk = wtok[:, k:k + 1]
        a = wk * _unpack_pairs(ga_ref[k])
        b = wk * _unpack_pairs(gb_ref[k])
        r_a = a if r_a is None else r_a + a
        r_b = b if r_b is None else r_b + b
    routed = jnp.concatenate([r_a, r_b], axis=1)

    xn = _layer_norm(ALPHA * x_ref[...] + (routed + shared + ple), lng_ref[...], lnb_ref[...])
    if len(out_refs) == 2 and out_refs[1].dtype == BF16:
        out_refs[0][...] = xn
        out_refs[1][...] = xn.astype(BF16)
    else:
        out_refs[-1][...] = xn


def _finish(x, p_all, layer, stream, ga, gb, wtok, wgs, wus, wds, wpg, bpg, wp, ln_g, ln_b,
            out_base=None, total_rows=None, tm=512):
    n = x.shape[0]
    p_block = stream * (n // tm)
    row = lambda i: (i, 0)
    const = lambda i: (0, 0)
    if total_rows is None:
        out_specs = [pl.BlockSpec((tm, D_MODEL), row), pl.BlockSpec((tm, D_MODEL), row)]
        out_shape = [jax.ShapeDtypeStruct((n, D_MODEL), F32), jax.ShapeDtypeStruct((n, D_MODEL), BF16)]
    else:
        out_specs = [pl.BlockSpec((tm, D_MODEL), lambda i: (p_block + i, 0))]
        out_shape = [jax.ShapeDtypeStruct((total_rows, D_MODEL), F32)]
    extra_specs = [] if out_base is None else [pl.BlockSpec(memory_space=pl.ANY)]
    extra_args = [] if out_base is None else [out_base]
    aliases = {} if out_base is None else {13: 0}
    return pl.pallas_call(
        _finish_kernel,
        grid=(n // tm,),
        input_output_aliases=aliases,
        in_specs=[
            pl.BlockSpec((tm, D_MODEL), row),
            pl.BlockSpec((1, tm, P_DIM), lambda i: (layer, p_block + i, 0)),
            pl.BlockSpec((TOP_K, tm, PART), lambda i: (0, i, 0)),
            pl.BlockSpec((TOP_K, tm, PART), lambda i: (0, i, 0)),
            pl.BlockSpec((tm, LANES), row),
            pl.BlockSpec((D_MODEL, D_SHARED), const),
            pl.BlockSpec((D_MODEL, D_SHARED), const),
            pl.BlockSpec((D_SHARED, D_MODEL), const),
            pl.BlockSpec((D_MODEL, D_MODEL), const),
            pl.BlockSpec((1, D_MODEL), const),
            pl.BlockSpec((P_DIM, D_MODEL), const),
            pl.BlockSpec((1, D_MODEL), const),
            pl.BlockSpec((1, D_MODEL), const),
        ] + extra_specs,
        out_specs=out_specs,
        out_shape=out_shape,
        name="moe_finish_ln",
        compiler_params=_params("parallel"),
    )(x, p_all, ga, gb, wtok, wgs, wus, wds, wpg, bpg, wp, ln_g, ln_b, *extra_args)


def _route(x, xpa, xpb, w_router_t, b_router):
    n = x.shape[0]
    pairs = n * TOP_K
    tiles = pairs // ROW_TILE + N_EXPERTS
    eidx, rank, wtok, counts = _router(x, w_router_t, b_router)

    cnt = counts[:, 0].astype(I32)
    group_tiles = (cnt + ROW_TILE - 1) // ROW_TILE
    tile_end = jnp.cumsum(group_tiles)
    starts = (tile_end - group_tiles) * ROW_TILE
    tile_id = jnp.arange(tiles, dtype=I32)
    tile_expert = jnp.minimum(jnp.sum((tile_end[None, :] <= tile_id[:, None]).astype(I32), axis=1), N_EXPERTS - 1)

    slots_flat = _slots(starts, eidx, rank).reshape(1, pairs)
    xsa, xsb = _sc_dispatch([xpa, xpb], slots_flat, tiles * ROW_TILE)
    return dict(xsa=xsa, xsb=xsb, slots_flat=slots_flat, wtok=wtok, tile_expert=tile_expert,
                tiles_used=tile_end[N_EXPERTS - 1:])


class _IssueOrder:
    def __init__(self):
        self.last = None

    def before(self, operand):
        if self.last is None:
            return operand
        self.last, operand = lax.optimization_barrier((self.last, operand))
        return operand

    def after(self, result):
        self.last = result


def kernel(x, p, w_in, b_in, sg_ln_g, sg_ln_b, w_s, b_s, mh_g, w_pa, w_pb, w_out, ln1_g, ln1_b, w_router, b_router, w_gate_e, w_up_e, w_down_e, w_gate_s, w_up_s, w_down_s, w_pg, b_pg, w_p, ln2_g, ln2_b):
    batch, seq, _ = x.shape
    streams = STREAMS if batch % STREAMS == 0 else 1
    sb = batch // streams
    n = sb * seq
    x_all = x.reshape(batch * seq, D_MODEL)
    xf = [x_all] * streams
    x_row0 = [s * n for s in range(streams)]
    xb = [x_all[s * n:(s + 1) * n].astype(BF16) for s in range(streams)]
    p_all = p.reshape(DEPTH, batch * seq, P_DIM)

    wg_all = w_gate_e.reshape(DEPTH * N_EXPERTS, D_MODEL, D_EXPERT)
    wu_all = w_up_e.reshape(DEPTH * N_EXPERTS, D_MODEL, D_EXPERT)
    wd_all = w_down_e.reshape(DEPTH * N_EXPERTS, D_EXPERT, D_MODEL)

    c_uv = 2 * G_WIDTH
    c_qkv = c_uv + 2 * M_QK + M_V
    c_o = c_qkv + M_V
    c_if = c_o + 2 * M_HEADS
    c_gb = c_if + D_MODEL

    w_in_bf = w_in.astype(BF16)

    def layer_weights(l):
        w = w_in_bf[l]
        b = b_in[l][None, :]
        return dict(
            sgu=(w[:, :G_WIDTH], b[:, :G_WIDTH], w[:, G_WIDTH:c_uv], b[:, G_WIDTH:c_uv],
                 sg_ln_g[l][None, :], sg_ln_b[l][None, :], w_s[l], b_s[l].T),
            mproj=(w[:, c_uv:c_qkv], b[:, c_uv:c_qkv], w[:, c_qkv:c_o], b[:, c_qkv:c_o],
                   jnp.pad(w[:, c_o:c_if], ((0, 0), (0, LANES - 2 * M_HEADS))),
                   jnp.pad(b[:, c_o:c_if], ((0, 0), (0, LANES - 2 * M_HEADS)))),
            mhg=mh_g[l][None, :],
            merge=(w[:, c_if:c_gb], b[:, c_if:c_gb], w[:, c_gb:], b[:, c_gb:],
                   w_pa[l].astype(BF16), w_pb[l].astype(BF16), w_out[l].astype(BF16),
                   ln1_g[l][None, :], ln1_b[l][None, :]),
            router=(w_router[l].T, b_router[l][:, None]),
            finish=(w_gate_s[l].astype(BF16), w_up_s[l].astype(BF16), w_down_s[l].astype(BF16),
                    w_pg[l].astype(BF16), b_pg[l][None, :], w_p[l].astype(BF16),
                    ln2_g[l][None, :], ln2_b[l][None, :]))

    weights = [layer_weights(l) for l in range(DEPTH)]
    order = _IssueOrder()
    st = [dict() for _ in range(streams)]
    result = [None]

    def proj(s, l):
        st[s]["qkv"], st[s]["osig"], st[s]["gif"], st[s]["gif_rows"] = _mproj(
            order.before(xb[s]), *weights[l]["mproj"])
        order.after(st[s]["gif"])
        st[s]["a"] = _sgu(order.before(xb[s]), *weights[l]["sgu"])
        order.after(st[s]["a"])

    def recur(s, l):
        st[s]["mb"] = _mlstm(order.before(st[s]["qkv"]), st[s]["gif"], st[s]["gif_rows"], st[s]["osig"],
                             weights[l]["mhg"], sb, seq)
        order.after(st[s]["mb"])

    def merge(s, l):
        xf[s], xb[s], xpa, xpb = _merge(xf[s], order.before(xb[s]), st[s]["a"], st[s]["mb"], *weights[l]["merge"],
                                        x_row0=x_row0[s])
        x_row0[s] = 0
        st[s]["packed"] = (xpa, xpb)
        order.after(xpb)

    def route(s, l):
        st[s]["route"] = _route(order.before(xf[s]), *st[s]["packed"], *weights[l]["router"])
        order.after(st[s]["route"]["wtok"])

    def experts(s, l):
        r = st[s]["route"]
        ysa, ysb = _experts(r["tile_expert"], r["tiles_used"], order.before(r["xsa"]), r["xsb"], l,
                            wg_all, wu_all, wd_all)
        order.after(ysb)
        ga, gb = _sc_gather([ysa, ysb], r["slots_flat"])
        st[s]["gathered"] = (ga.reshape(TOP_K, n, PART), gb.reshape(TOP_K, n, PART))

    def finish(s, l):
        last = l == DEPTH - 1
        outs = _finish(order.before(xf[s]), p_all, l, s, *st[s]["gathered"], st[s]["route"]["wtok"],
                       *weights[l]["finish"], out_base=result[0] if last else None,
                       total_rows=batch * seq if last else None)
        if last:
            result[0], = outs
        else:
            xf[s], xb[s] = outs
        order.after(outs[0])

    if streams == 1:
        for l in range(DEPTH):
            proj(0, l), recur(0, l), merge(0, l), route(0, l), experts(0, l), finish(0, l)
    else:
        proj(0, 0), recur(0, 0), merge(0, 0), route(0, 0), proj(1, 0)
        for l in range(DEPTH):
            more = l + 1 < DEPTH
            experts(0, l), recur(1, l), merge(1, l)
            if more:
                finish(0, l), route(1, l), proj(0, l + 1)
            else:
                route(1, l), finish(0, l)
            experts(1, l)
            if more:
                recur(0, l + 1), merge(0, l + 1)
            finish(1, l)
            if more:
                route(0, l + 1), proj(1, l + 1)
    return result[0].reshape(batch, seq, D_MODEL)
```

```python
import functools

import jax
import jax.numpy as jnp
from jax import lax
from jax.experimental import pallas as pl
from jax.experimental.pallas import tpu as pltpu
from jax.experimental.pallas import tpu_sc as plsc

D_MODEL = 1024
DEPTH = 4
CHUNK = 64
P_DIM = 256
G_WIDTH = 1024
G_GROUPS = 8
G_GROUP_DIM = G_WIDTH // G_GROUPS
G_BLOCK = 128
M_HEADS = 4
M_QK_DIM = 128
M_V_DIM = 256
M_QK = M_HEADS * M_QK_DIM
M_V = M_HEADS * M_V_DIM
N_EXPERTS = 64
TOP_K = 8
N_GROUPS = 8
TOPK_GROUPS = 4
GROUP_SIZE = N_EXPERTS // N_GROUPS
D_EXPERT = 256
D_SHARED = 256
ROUTE_SCALE = 2.5
ALPHA = (2 * DEPTH) ** 0.25
LN_EPS = 1e-5
RMS_EPS = 1e-6

LANES = 128
VMEM_LIMIT = 56 * 1024 * 1024
HALF = D_MODEL // 2
PART = HALF // 2
ROW_TILE = 1024
STREAMS = 2
MERGE_PARTS = 2
ROUTER_PART = 512
MLSTM_CHUNK = 128
MLSTM_GROUP = 8
SC_WINDOW = 128

F32 = jnp.float32
BF16 = jnp.bfloat16
I32 = jnp.int32


def _params(*semantics):
    return pltpu.CompilerParams(dimension_semantics=semantics, vmem_limit_bytes=VMEM_LIMIT)


def _layer_norm(x, g, b):
    mu = jnp.mean(x, axis=-1, keepdims=True)
    xc = x - mu
    var = jnp.mean(xc * xc, axis=-1, keepdims=True)
    return xc * lax.rsqrt(var + LN_EPS) * g + b


def _gelu(x):
    return 0.5 * x * (1.0 + lax.erf(x * (2.0 ** -0.5)))


def _silu(x):
    return x * jax.nn.sigmoid(x)


def _pack_pairs(x):
    lo = lax.bitcast_convert_type(x[:, :PART].astype(BF16).astype(F32), I32)
    hi = lax.bitcast_convert_type(x[:, PART:].astype(BF16).astype(F32), I32)
    return lax.shift_right_logical(lo, 16) | (hi & jnp.int32(-65536))


def _pack_row(x):
    return _pack_pairs(x[:, :HALF]), _pack_pairs(x[:, HALF:])


def _unpack_pairs(w):
    lo = lax.bitcast_convert_type(lax.shift_left(w, 16), F32)
    hi = lax.bitcast_convert_type(w & jnp.int32(-65536), F32)
    return jnp.concatenate([lo, hi], axis=1)


def _mproj_kernel(x_ref, wqkv_ref, bqkv_ref, wo_ref, bo_ref, wif_ref, bif_ref, qkv_ref, osig_ref, gif_ref, gifr_ref):
    x = x_ref[...]
    qkv_ref[...] = (jnp.dot(x, wqkv_ref[...], preferred_element_type=F32) + bqkv_ref[...]).astype(qkv_ref.dtype)
    osig_ref[...] = jax.nn.sigmoid(
        jnp.dot(x, wo_ref[...], preferred_element_type=F32) + bo_ref[...]).astype(osig_ref.dtype)
    gif = jnp.dot(x, wif_ref[...], preferred_element_type=F32) + bif_ref[...]
    gif_ref[...] = gif
    gifr_ref[...] = gif.T[:2 * M_HEADS, :]


def _mproj(x, w_qkv, b_qkv, w_o, b_o, w_if, b_if, tm=1024):
    n, k = x.shape
    row = lambda i: (i, 0)
    const = lambda i: (0, 0)
    widths = (w_qkv.shape[1], w_o.shape[1], w_if.shape[1])
    in_specs = [pl.BlockSpec((tm, k), row)]
    for width in widths:
        in_specs += [pl.BlockSpec((k, width), const), pl.BlockSpec((1, width), const)]
    return pl.pallas_call(
        _mproj_kernel,
        grid=(n // tm,),
        in_specs=in_specs,
        out_specs=[pl.BlockSpec((tm, width), row) for width in widths]
        + [pl.BlockSpec((2 * M_HEADS, tm), lambda i: (0, i))],
        out_shape=[jax.ShapeDtypeStruct((n, widths[0]), BF16), jax.ShapeDtypeStruct((n, widths[1]), BF16),
                   jax.ShapeDtypeStruct((n, widths[2]), F32), jax.ShapeDtypeStruct((2 * M_HEADS, n), F32)],
        name="mlstm_proj",
        compiler_params=_params("parallel"),
    )(x, w_qkv, b_qkv, w_o, b_o, w_if, b_if)


def _sgu_kernel(x_ref, wu_ref, bu_ref, wv_ref, bv_ref, lng_ref, lnb_ref, ws_ref, bs_ref, a_ref, *, blocks):
    x = x_ref[...]
    v = _gelu(jnp.dot(x, wv_ref[...], preferred_element_type=F32) + bv_ref[...])
    vln = _layer_norm(v, lng_ref[...], lnb_ref[...]).astype(BF16)
    u = _gelu(jnp.dot(x, wu_ref[...], preferred_element_type=F32) + bu_ref[...])
    t_chunk = lax.broadcasted_iota(jnp.int32, (G_BLOCK, G_BLOCK), 0) // CHUNK
    s_chunk = lax.broadcasted_iota(jnp.int32, (G_BLOCK, G_BLOCK), 1) // CHUNK
    causal = s_chunk <= t_chunk
    for g in range(G_GROUPS):
        w = jnp.where(causal, ws_ref[g], 0.0).astype(BF16)
        bias = bs_ref[:, g:g + 1]
        cs = slice(g * G_GROUP_DIM, (g + 1) * G_GROUP_DIM)
        for blk in range(blocks):
            rs = slice(blk * G_BLOCK, (blk + 1) * G_BLOCK)
            mixed = jnp.dot(w, vln[rs, cs], preferred_element_type=F32) + bias
            a_ref[rs, cs] = (u[rs, cs] * mixed).astype(a_ref.dtype)


def _sgu(x, w_u, b_u, w_v, b_v, ln_g, ln_b, w_s, b_s_t, blocks=8):
    n = x.shape[0]
    tp = blocks * G_BLOCK
    const = lambda i: (0, 0)
    wspec = pl.BlockSpec((D_MODEL, G_WIDTH), const)
    vspec = pl.BlockSpec((1, G_WIDTH), const)
    return pl.pallas_call(
        functools.partial(_sgu_kernel, blocks=blocks),
        grid=(n // tp,),
        in_specs=[
            pl.BlockSpec((tp, D_MODEL), lambda i: (i, 0)),
            wspec, vspec, wspec, vspec, vspec, vspec,
            pl.BlockSpec((G_GROUPS, G_BLOCK, G_BLOCK), lambda i: (0, 0, 0)),
            pl.BlockSpec((G_BLOCK, G_GROUPS), const),
        ],
        out_specs=pl.BlockSpec((tp, G_WIDTH), lambda i: (i, 0)),
        out_shape=jax.ShapeDtypeStruct((n, G_WIDTH), BF16),
        name="spatial_gating",
        compiler_params=_params("parallel"),
    )(x, w_u, b_u, w_v, b_v, ln_g, ln_b, w_s, b_s_t)


def _segment_cumsum(x, axis, seg):
    pos = lax.broadcasted_iota(jnp.int32, x.shape, axis) % seg
    shift = 1
    while shift < seg:
        x = x + jnp.where(pos >= shift, pltpu.roll(x, shift, axis), 0.0)
        shift *= 2
    return x


def _mlstm_kernel(qkv_ref, gc_ref, gr_ref, osig_ref, mhg_ref, mb_ref, c_ref, n_ref, m_ref, *, chunks):
    @pl.when(pl.program_id(1) == 0)
    def _():
        c_ref[...] = jnp.zeros_like(c_ref)
        n_ref[...] = jnp.zeros_like(n_ref)
        m_ref[...] = jnp.zeros_like(m_ref)

    scale = M_QK_DIM ** -0.5
    gc = gc_ref[...]
    gr = gr_ref[...]
    b_cols = _segment_cumsum(jax.nn.log_sigmoid(gc), 0, MLSTM_CHUNK)
    b_rows = _segment_cumsum(jax.nn.log_sigmoid(gr), 1, MLSTM_CHUNK)
    t_idx = lax.broadcasted_iota(jnp.int32, (MLSTM_CHUNK, MLSTM_CHUNK), 0)
    s_idx = lax.broadcasted_iota(jnp.int32, (MLSTM_CHUNK, MLSTM_CHUNK), 1)
    tri = s_idx <= t_idx

    heads = range(M_HEADS)
    rows = lambda c: slice(c * MLSTM_CHUNK, (c + 1) * MLSTM_CHUNK)

    def local_terms(cs):
        units = [(c, h) for c in cs for h in heads]
        q = {u: qkv_ref[rows(u[0]), u[1] * M_QK_DIM:(u[1] + 1) * M_QK_DIM] for u in units}
        k = {u: qkv_ref[rows(u[0]), M_QK + u[1] * M_QK_DIM:M_QK + (u[1] + 1) * M_QK_DIM] for u in units}
        v = {u: qkv_ref[rows(u[0]), 2 * M_QK + u[1] * M_V_DIM:2 * M_QK + (u[1] + 1) * M_V_DIM] for u in units}
        i_col = {u: gc[rows(u[0]), u[1]:u[1] + 1] for u in units}
        b_col = {u: b_cols[rows(u[0]), M_HEADS + u[1]:M_HEADS + u[1] + 1] for u in units}
        ib_row = {u: (b_rows[M_HEADS + u[1]:M_HEADS + u[1] + 1, rows(u[0])]
                      - gr[u[1]:u[1] + 1, rows(u[0])]) for u in units}
        d = {u: jnp.where(tri, b_col[u] - ib_row[u], -jnp.inf) for u in units}
        d_max = {u: jnp.max(d[u], axis=1, keepdims=True) for u in units}
        qk = {u: lax.dot_general(q[u], k[u], (((1,), (1,)), ((), ())), preferred_element_type=F32) for u in units}
        s_loc = {u: qk[u] * scale * jnp.exp(d[u] - d_max[u]) for u in units}
        a_loc = {u: jnp.dot(s_loc[u].astype(BF16), v[u], preferred_element_type=F32) for u in units}
        r_loc = {u: jnp.sum(s_loc[u], axis=1, keepdims=True) for u in units}
        b_last = {u: b_col[u][MLSTM_CHUNK - 1:MLSTM_CHUNK, :] for u in units}
        g_col = {u: b_last[u] - b_col[u] + i_col[u] for u in units}
        g_max = {u: jnp.max(g_col[u], axis=0, keepdims=True) for u in units}
        wk = {u: jnp.exp(g_col[u] - g_max[u]) * k[u].astype(F32) for u in units}
        u_loc = {u: jnp.dot(wk[u].T.astype(BF16), v[u], preferred_element_type=F32) for u in units}
        nk_loc = {u: jnp.sum(wk[u], axis=0, keepdims=True) for u in units}
        return dict(q=q, b_col=b_col, d_max=d_max, a_loc=a_loc, r_loc=r_loc, b_last=b_last, g_max=g_max,
                    u_loc=u_loc, nk_loc=nk_loc)

    def carried_step(c, loc, c_state, n_state, m_state):
        inter = [loc["b_col"][(c, h)] + m_state[h] for h in heads]
        m_t = [jnp.maximum(inter[h], loc["d_max"][(c, h)]) for h in heads]
        w_intra = [jnp.exp(loc["d_max"][(c, h)] - m_t[h]) for h in heads]
        w_inter = [jnp.exp(inter[h] - m_t[h]) for h in heads]
        qc = [jnp.dot(loc["q"][(c, h)], c_state[h].astype(BF16), preferred_element_type=F32) * scale for h in heads]
        qn = [jnp.sum(loc["q"][(c, h)].astype(F32) * n_state[h], axis=1, keepdims=True) * scale for h in heads]
        m_new = [jnp.maximum(loc["b_last"][(c, h)] + m_state[h], loc["g_max"][(c, h)]) for h in heads]
        decay = [jnp.exp(loc["b_last"][(c, h)] + m_state[h] - m_new[h]) for h in heads]
        beta = [jnp.exp(loc["g_max"][(c, h)] - m_new[h]) for h in heads]
        c_next = [decay[h] * c_state[h] + beta[h] * loc["u_loc"][(c, h)] for h in heads]
        n_next = [decay[h] * n_state[h] + beta[h] * loc["nk_loc"][(c, h)] for h in heads]
        num = [w_intra[h] * loc["a_loc"][(c, h)] + w_inter[h] * qc[h] for h in heads]
        den = [w_intra[h] * loc["r_loc"][(c, h)] + w_inter[h] * qn[h] for h in heads]
        hv = [num[h] * (1.0 / jnp.maximum(jnp.abs(den[h]), jnp.exp(-m_t[h]))) for h in heads]
        hv = [hv[h] * lax.rsqrt(jnp.mean(hv[h] * hv[h], axis=1, keepdims=True) + RMS_EPS) for h in heads]
        for h in heads:
            vs = slice(h * M_V_DIM, (h + 1) * M_V_DIM)
            mb_ref[rows(c), vs] = (osig_ref[rows(c), vs].astype(F32) * (hv[h] * mhg_ref[:, vs])).astype(mb_ref.dtype)
        return c_next, n_next, m_new

    c_state = [c_ref[h] for h in heads]
    n_state = [n_ref[h] for h in heads]
    m_state = [m_ref[h][:, 0:1] for h in heads]
    groups = [list(range(g, min(g + MLSTM_GROUP, chunks))) for g in range(0, chunks, MLSTM_GROUP)]
    loc = local_terms(groups[0])
    for gi, group in enumerate(groups):
        loc_next = local_terms(groups[gi + 1]) if gi + 1 < len(groups) else None
        for c in group:
            c_state, n_state, m_state = carried_step(c, loc, c_state, n_state, m_state)
        loc = loc_next

    for h in heads:
        c_ref[h], n_ref[h] = c_state[h], n_state[h]
        m_ref[h] = jnp.broadcast_to(m_state[h], (1, LANES))


def _mlstm(qkv, gates_col, gates_row, osig_src, mh_g, batch, seq, chunks=8):
    ts = chunks * MLSTM_CHUNK
    tiles = seq // ts
    n = batch * seq
    return pl.pallas_call(
        functools.partial(_mlstm_kernel, chunks=chunks),
        grid=(batch, tiles),
        in_specs=[
            pl.BlockSpec((ts, 2 * M_QK + M_V), lambda b, t: (b * tiles + t, 0)),
            pl.BlockSpec((ts, LANES), lambda b, t: (b * tiles + t, 0)),
            pl.BlockSpec((2 * M_HEADS, ts), lambda b, t: (0, b * tiles + t)),
            pl.BlockSpec((ts, M_V), lambda b, t: (b * tiles + t, 0)),
            pl.BlockSpec((1, M_V), lambda b, t: (0, 0)),
        ],
        out_specs=pl.BlockSpec((ts, M_V), lambda b, t: (b * tiles + t, 0)),
        out_shape=jax.ShapeDtypeStruct((n, M_V), BF16),
        scratch_shapes=[
            pltpu.VMEM((M_HEADS, M_QK_DIM, M_V_DIM), F32),
            pltpu.VMEM((M_HEADS, 1, M_QK_DIM), F32),
            pltpu.VMEM((M_HEADS, 1, LANES), F32),
        ],
        name="mlstm",
        compiler_params=_params("parallel", "arbitrary"),
    )(qkv, gates_col, gates_row, osig_src, mh_g)


def _merge_kernel(x_ref, xin_ref, a_ref, mb_ref, wga_ref, bga_ref, wgb_ref, bgb_ref, wpa_ref, wpb_ref, wout_ref,
                  g_ref, b_ref, xo_ref, xob_ref, xpa_ref, xpb_ref):
    tm = x_ref.shape[0]
    halves = [slice(r, r + tm // MERGE_PARTS) for r in range(0, tm, tm // MERGE_PARTS)]
    dot = lambda lhs, w_ref: jnp.dot(lhs, w_ref[...], preferred_element_type=F32)
    ga = [jax.nn.sigmoid(dot(xin_ref[rs, :], wga_ref) + bga_ref[...]) for rs in halves]
    ya = [dot(a_ref[rs, :], wpa_ref) for rs in halves]
    gb = [jax.nn.sigmoid(dot(xin_ref[rs, :], wgb_ref) + bgb_ref[...]) for rs in halves]
    yb = [dot(mb_ref[rs, :], wpb_ref) for rs in halves]
    y = [(ga[h] * ya[h] + gb[h] * yb[h]).astype(BF16) for h in range(MERGE_PARTS)]
    mix = [dot(y[h], wout_ref) for h in range(MERGE_PARTS)]
    for h, rs in enumerate(halves):
        xn = _layer_norm(ALPHA * x_ref[rs, :] + mix[h], g_ref[...], b_ref[...])
        xo_ref[rs, :] = xn
        xob_ref[rs, :] = xn.astype(BF16)
        xpa_ref[rs, :], xpb_ref[rs, :] = _pack_row(xn)


def _merge(x, xin, a, mb, w_ga, b_ga, w_gb, b_gb, w_pa, w_pb, w_out, ln_g, ln_b, x_row0=0, tm=512):
    n = xin.shape[0]
    row = lambda i: (i, 0)
    const = lambda i: (0, 0)
    wspec = pl.BlockSpec((D_MODEL, D_MODEL), const)
    vspec = pl.BlockSpec((1, D_MODEL), const)
    return pl.pallas_call(
        _merge_kernel,
        grid=(n // tm,),
        in_specs=[
            pl.BlockSpec((tm, D_MODEL), lambda i: (x_row0 // tm + i, 0)),
            pl.BlockSpec((tm, D_MODEL), row),
            pl.BlockSpec((tm, G_WIDTH), row),
            pl.BlockSpec((tm, M_V), row),
            wspec, vspec, wspec, vspec, wspec, wspec, wspec, vspec, vspec,
        ],
        out_specs=[pl.BlockSpec((tm, D_MODEL), row), pl.BlockSpec((tm, D_MODEL), row),
                   pl.BlockSpec((tm, PART), row), pl.BlockSpec((tm, PART), row)],
        out_shape=[jax.ShapeDtypeStruct((n, D_MODEL), F32), jax.ShapeDtypeStruct((n, D_MODEL), BF16),
                   jax.ShapeDtypeStruct((n, PART), I32), jax.ShapeDtypeStruct((n, PART), I32)],
        name="merge_ln",
        compiler_params=_params("parallel"),
    )(x, xin, a, mb, w_ga, b_ga, w_gb, b_gb, w_pa, w_pb, w_out, ln_g, ln_b)


def _first_max(v, idx, axes, sentinel):
    m = jnp.max(v, axis=axes, keepdims=True)
    first = jnp.min(jnp.where(v == m, idx, sentinel), axis=axes, keepdims=True)
    return m, first


def _router_kernel(x_ref, wr_ref, br_ref, upper_ref, eidx_ref, rank_ref, wtok_ref, cnt_ref, run_ref, *, tm):
    @pl.when(pl.program_id(0) == 0)
    def _():
        run_ref[...] = jnp.zeros_like(run_ref)

    tw = ROUTER_PART
    parts = range(tm // tw)
    shape3 = (N_GROUPS, GROUP_SIZE, tw)
    neg = -jnp.inf

    def split(v):
        hi = v.astype(BF16)
        return hi, (v - hi.astype(F32)).astype(BF16)

    nt = lambda a, b: lax.dot_general(a, b, (((1,), (1,)), ((), ())), preferred_element_type=F32)
    w_hi, w_lo = split(wr_ref[...])
    xs = [split(x_ref[i * tw:(i + 1) * tw, :]) for i in parts]
    logits = [nt(w_hi, x_hi) + (nt(w_hi, x_lo) + nt(w_lo, x_hi)) for x_hi, x_lo in xs]
    scores = [jax.nn.sigmoid(lg) for lg in logits]
    sel = [(sc + br_ref[...]).reshape(shape3) for sc in scores]
    scores3 = [sc.reshape(shape3) for sc in scores]
    member = lax.broadcasted_iota(jnp.int32, shape3, 1)
    group = lax.broadcasted_iota(jnp.int32, shape3, 0)

    best = [_first_max(v, member, 1, GROUP_SIZE) for v in sel]
    second = [jnp.max(jnp.where(member == f1, neg, v), axis=1, keepdims=True) for v, (_, f1) in zip(sel, best)]
    gscore = [m1 + m2 for (m1, _), m2 in zip(best, second)]
    gid = lax.broadcasted_iota(jnp.int32, gscore[0].shape, 0)
    gmask = [jnp.zeros(gscore[0].shape, dtype=jnp.bool_) for _ in parts]
    for _ in range(TOPK_GROUPS):
        hits = [gid == _first_max(g, gid, 0, N_GROUPS)[1] for g in gscore]
        gmask = [m | h for m, h in zip(gmask, hits)]
        gscore = [jnp.where(h, neg, g) for h, g in zip(hits, gscore)]

    cand = [jnp.where(m, v, neg) for m, v in zip(gmask, sel)]
    eid = group * GROUP_SIZE + member
    chosen = [jnp.zeros(shape3, dtype=jnp.bool_) for _ in parts]
    picks = [[] for _ in parts]
    for _ in range(TOP_K):
        firsts = [_first_max(c, eid, (0, 1), N_EXPERTS)[1] for c in cand]
        hits = [eid == fe for fe in firsts]
        chosen = [c | h for c, h in zip(chosen, hits)]
        cand = [jnp.where(h, neg, c) for h, c in zip(hits, cand)]
        picked = [jnp.sum(jnp.where(h, s3, 0.0), axis=(0, 1), keepdims=True)[0] for h, s3 in zip(hits, scores3)]
        for i in parts:
            picks[i].append((firsts[i][0], hits[i], picked[i]))

    chosen2 = [jnp.where(c, 1.0, 0.0).reshape(N_EXPERTS, tw) for c in chosen]
    counts = [jnp.sum(c2, axis=1, keepdims=True) for c2 in chosen2]
    bases = []
    base = run_ref[:, 0:1]
    for i in parts:
        bases.append(base)
        base = base + counts[i]
    prefix3 = [(jnp.dot(c2.astype(BF16), upper_ref[...], preferred_element_type=F32) + b0).reshape(shape3)
               for c2, b0 in zip(chosen2, bases)]

    for i in parts:
        cols = slice(i * tw, (i + 1) * tw)
        total = picks[i][0][2]
        for _, _, wk in picks[i][1:]:
            total = total + wk
        eidx_ref[:, cols] = jnp.concatenate([fe for fe, _, _ in picks[i]], axis=0)
        rank_ref[:, cols] = jnp.concatenate(
            [jnp.sum(jnp.where(hit, prefix3[i], 0.0), axis=(0, 1), keepdims=True)[0] for _, hit, _ in picks[i]],
            axis=0).astype(I32)
        w_rows = jnp.concatenate([wk / total * ROUTE_SCALE for _, _, wk in picks[i]]
                                 + [jnp.zeros((LANES - TOP_K, tw), F32)], axis=0)
        wtok_ref[cols, :] = w_rows.T

    run = run_ref[...] + (base - run_ref[:, 0:1])
    run_ref[...] = run
    cnt_ref[...] = run


def _router(x, wr_t, br, tm=1024):
    n = x.shape[0]
    tok = lambda i: (0, i)
    upper = (jnp.arange(ROUTER_PART)[:, None] < jnp.arange(ROUTER_PART)[None, :]).astype(BF16)
    return pl.pallas_call(
        functools.partial(_router_kernel, tm=tm),
        grid=(n // tm,),
        in_specs=[
            pl.BlockSpec((tm, D_MODEL), lambda i: (i, 0)),
            pl.BlockSpec((N_EXPERTS, D_MODEL), lambda i: (0, 0)),
            pl.BlockSpec((N_EXPERTS, 1), lambda i: (0, 0)),
            pl.BlockSpec((ROUTER_PART, ROUTER_PART), lambda i: (0, 0)),
        ],
        out_specs=[
            pl.BlockSpec((TOP_K, tm), tok),
            pl.BlockSpec((TOP_K, tm), tok),
            pl.BlockSpec((tm, LANES), lambda i: (i, 0)),
            pl.BlockSpec((N_EXPERTS, LANES), lambda i: (0, 0)),
        ],
        out_shape=[
            jax.ShapeDtypeStruct((TOP_K, n), I32),
            jax.ShapeDtypeStruct((TOP_K, n), I32),
            jax.ShapeDtypeStruct((n, LANES), F32),
            jax.ShapeDtypeStruct((N_EXPERTS, LANES), F32),
        ],
        scratch_shapes=[pltpu.VMEM((N_EXPERTS, LANES), F32)],
        name="router",
        compiler_params=_params("arbitrary"),
    )(x, wr_t, br, upper)


def _slots_kernel(starts_ref, eidx_ref, rank_ref, slot_ref):
    eidx = eidx_ref[...]
    slot = rank_ref[...]
    for e in range(N_EXPERTS):
        slot = slot + jnp.where(eidx == e, starts_ref[e], 0)
    slot_ref[...] = slot


def _slots(starts, eidx, rank, tn=2048):
    n = eidx.shape[1]
    tn = min(tn, n)
    spec = pl.BlockSpec((TOP_K, tn), lambda i, s: (0, i))
    return pl.pallas_call(
        _slots_kernel,
        grid_spec=pltpu.PrefetchScalarGridSpec(
            num_scalar_prefetch=1, grid=(n // tn,), in_specs=[spec, spec], out_specs=spec),
        out_shape=jax.ShapeDtypeStruct((TOP_K, n), I32),
        name="slots",
        compiler_params=_params("parallel"),
    )(starts, eidx, rank)


def _sc_mesh():
    return plsc.VectorSubcoreMesh(core_axis_name="c", subcore_axis_name="s")


def _sc_dispatch(parts, slots_flat, rows):
    n, width = parts[0].shape
    blocks = n // SC_WINDOW
    out_type = [jax.ShapeDtypeStruct((rows, width), part.dtype) for part in parts]

    @functools.partial(pl.kernel, out_type=out_type, mesh=_sc_mesh(), scratch_types=[], name="sc_dispatch")
    def run(*refs):
        i_hbm = refs[len(parts)]
        for x_hbm, o_hbm in zip(refs[:len(parts)], refs[len(parts) + 1:]):
            def body(x_vmem, i_vmem, o_hbm=o_hbm):
                pltpu.sync_copy(x_vmem, o_hbm.at[i_vmem.at[0]])

            pltpu.emit_pipeline(
                body,
                grid=(blocks, TOP_K),
                in_specs=[pl.BlockSpec((SC_WINDOW, width), lambda i, k: (i, 0)),
                          pl.BlockSpec((1, SC_WINDOW), lambda i, k: (0, k * blocks + i))],
                out_specs=[],
                core_axis_name=("c", "s"),
                dimension_semantics=(pltpu.PARALLEL, pltpu.ARBITRARY),
                trace_scopes=False,
            )(x_hbm, i_hbm)

    return run(*parts, slots_flat)


def _sc_gather(tables, idx_flat):
    count = idx_flat.shape[1]
    width = tables[0].shape[1]
    out_type = [jax.ShapeDtypeStruct((count, width), table.dtype) for table in tables]

    @functools.partial(pl.kernel, out_type=out_type, mesh=_sc_mesh(), scratch_types=[], name="sc_gather")
    def run(*refs):
        i_hbm = refs[len(tables)]
        for t_hbm, o_hbm in zip(refs[:len(tables)], refs[len(tables) + 1:]):
            def body(i_vmem, o_vmem, t_hbm=t_hbm):
                pltpu.sync_copy(t_hbm.at[i_vmem.at[0]], o_vmem)

            pltpu.emit_pipeline(
                body,
                grid=(count // SC_WINDOW,),
                in_specs=[pl.BlockSpec((1, SC_WINDOW), lambda i: (0, i))],
                out_specs=[pl.BlockSpec((SC_WINDOW, width), lambda i: (i, 0))],
                core_axis_name=("c", "s"),
                dimension_semantics=(pltpu.PARALLEL,),
                trace_scopes=False,
            )(i_hbm, o_hbm)

    return run(*tables, idx_flat)


def _expert_kernel(te_ref, used_ref, xa_ref, xb_ref, wg_ref, wu_ref, wd_ref, oa_ref, ob_ref, wgb_ref, wub_ref, wdb_ref):
    j = pl.program_id(0)

    @pl.when((j == 0) | (te_ref[j] != te_ref[jnp.maximum(j - 1, 0)]))
    def _():
        wgb_ref[...] = wg_ref[0].astype(BF16)
        wub_ref[...] = wu_ref[0].astype(BF16)
        wdb_ref[...] = wd_ref[0].astype(BF16)

    @pl.when(j < used_ref[0])
    def _():
        x = jnp.concatenate([_unpack_pairs(xa_ref[...]), _unpack_pairs(xb_ref[...])], axis=1).astype(BF16)
        hg = jnp.dot(x, wgb_ref[...], preferred_element_type=F32)
        hu = jnp.dot(x, wub_ref[...], preferred_element_type=F32)
        hid = (_silu(hg) * hu).astype(BF16)
        oa_ref[...], ob_ref[...] = _pack_row(jnp.dot(hid, wdb_ref[...], preferred_element_type=F32))


def _experts(tile_expert, tiles_used, xa, xb, layer, wg, wu, wd):
    rows = xa.shape[0]
    row = lambda j, te, used: (j, 0)
    exp = lambda j, te, used: (layer * N_EXPERTS + te[j], 0, 0)
    return pl.pallas_call(
        _expert_kernel,
        grid_spec=pltpu.PrefetchScalarGridSpec(
            num_scalar_prefetch=2,
            grid=(rows // ROW_TILE,),
            in_specs=[
                pl.BlockSpec((ROW_TILE, PART), row),
                pl.BlockSpec((ROW_TILE, PART), row),
                pl.BlockSpec((1, D_MODEL, D_EXPERT), exp),
                pl.BlockSpec((1, D_MODEL, D_EXPERT), exp),
                pl.BlockSpec((1, D_EXPERT, D_MODEL), exp),
            ],
            out_specs=[pl.BlockSpec((ROW_TILE, PART), row), pl.BlockSpec((ROW_TILE, PART), row)],
            scratch_shapes=[pltpu.VMEM((D_MODEL, D_EXPERT), BF16), pltpu.VMEM((D_MODEL, D_EXPERT), BF16),
                            pltpu.VMEM((D_EXPERT, D_MODEL), BF16)],
        ),
        out_shape=[jax.ShapeDtypeStruct((rows, PART), I32), jax.ShapeDtypeStruct((rows, PART), I32)],
        name="experts",
        compiler_params=_params("arbitrary"),
    )(tile_expert, tiles_used, xa, xb, wg, wu, wd)


def _finish_kernel(x_ref, p_ref, ga_ref, gb_ref, wtok_ref, wgs_ref, wus_ref, wds_ref, wpg_ref, bpg_ref, wp_ref,
                   lng_ref, lnb_ref, *out_refs):
    xb = x_ref[...].astype(BF16)
    hs = _silu(jnp.dot(xb, wgs_ref[...], preferred_element_type=F32)) * jnp.dot(
        xb, wus_ref[...], preferred_element_type=F32)
    shared = jnp.dot(hs.astype(BF16), wds_ref[...], preferred_element_type=F32)
    gate = jax.nn.sigmoid(jnp.dot(xb, wpg_ref[...], preferred_element_type=F32) + bpg_ref[...])
    ple = gate * jnp.dot(p_ref[0].astype(BF16), wp_ref[...], preferred_element_type=F32)

    wtok = wtok_ref[...]
    r_a = None
    r_b = None
    for k in range(TOP_K):
        wk = wtok[:, k:k + 1]
        a = wk * _unpack_pairs(ga_ref[k])
        b = wk * _unpack_pairs(gb_ref[k])
        r_a = a if r_a is None else r_a + a
        r_b = b if r_b is None else r_b + b
    routed = jnp.concatenate([r_a, r_b], axis=1)

    xn = _layer_norm(ALPHA * x_ref[...] + (routed + shared + ple), lng_ref[...], lnb_ref[...])
    if len(out_refs) == 2 and out_refs[1].dtype == BF16:
        out_refs[0][...] = xn
        out_refs[1][...] = xn.astype(BF16)
    else:
        out_refs[-1][...] = xn


def _finish(x, p_all, layer, stream, ga, gb, wtok, wgs, wus, wds, wpg, bpg, wp, ln_g, ln_b,
            out_base=None, total_rows=None, tm=512):
    n = x.shape[0]
    p_block = stream * (n // tm)
    row = lambda i: (i, 0)
    const = lambda i: (0, 0)
    if total_rows is None:
        out_specs = [pl.BlockSpec((tm, D_MODEL), row), pl.BlockSpec((tm, D_MODEL), row)]
        out_shape = [jax.ShapeDtypeStruct((n, D_MODEL), F32), jax.ShapeDtypeStruct((n, D_MODEL), BF16)]
    else:
        out_specs = [pl.BlockSpec((tm, D_MODEL), lambda i: (p_block + i, 0))]
        out_shape = [jax.ShapeDtypeStruct((total_rows, D_MODEL), F32)]
    extra_specs = [] if out_base is None else [pl.BlockSpec(memory_space=pl.ANY)]
    extra_args = [] if out_base is None else [out_base]
    aliases = {} if out_base is None else {13: 0}
    return pl.pallas_call(
        _finish_kernel,
        grid=(n // tm,),
        input_output_aliases=aliases,
        in_specs=[
            pl.BlockSpec((tm, D_MODEL), row),
            pl.BlockSpec((1, tm, P_DIM), lambda i: (layer, p_block + i, 0)),
            pl.BlockSpec((TOP_K, tm, PART), lambda i: (0, i, 0)),
            pl.BlockSpec((TOP_K, tm, PART), lambda i: (0, i, 0)),
            pl.BlockSpec((tm, LANES), row),
            pl.BlockSpec((D_MODEL, D_SHARED), const),
            pl.BlockSpec((D_MODEL, D_SHARED), const),
            pl.BlockSpec((D_SHARED, D_MODEL), const),
            pl.BlockSpec((D_MODEL, D_MODEL), const),
            pl.BlockSpec((1, D_MODEL), const),
            pl.BlockSpec((P_DIM, D_MODEL), const),
            pl.BlockSpec((1, D_MODEL), const),
            pl.BlockSpec((1, D_MODEL), const),
        ] + extra_specs,
        out_specs=out_specs,
        out_shape=out_shape,
        name="moe_finish_ln",
        compiler_params=_params("parallel"),
    )(x, p_all, ga, gb, wtok, wgs, wus, wds, wpg, bpg, wp, ln_g, ln_b, *extra_args)


def _route(x, xpa, xpb, w_router_t, b_router):
    n = x.shape[0]
    pairs = n * TOP_K
    tiles = pairs // ROW_TILE + N_EXPERTS
    eidx, rank, wtok, counts = _router(x, w_router_t, b_router)

    cnt = counts[:, 0].astype(I32)
    group_tiles = (cnt + ROW_TILE - 1) // ROW_TILE
    tile_end = jnp.cumsum(group_tiles)
    starts = (tile_end - group_tiles) * ROW_TILE
    tile_id = jnp.arange(tiles, dtype=I32)
    tile_expert = jnp.minimum(jnp.sum((tile_end[None, :] <= tile_id[:, None]).astype(I32), axis=1), N_EXPERTS - 1)

    slots_flat = _slots(starts, eidx, rank).reshape(1, pairs)
    xsa, xsb = _sc_dispatch([xpa, xpb], slots_flat, tiles * ROW_TILE)
    return dict(xsa=xsa, xsb=xsb, slots_flat=slots_flat, wtok=wtok, tile_expert=tile_expert,
                tiles_used=tile_end[N_EXPERTS - 1:])


class _IssueOrder:
    def __init__(self):
        self.last = None

    def before(self, operand):
        if self.last is None:
            return operand
        self.last, operand = lax.optimization_barrier((self.last, operand))
        return operand

    def after(self, result):
        self.last = result


def kernel(x, p, w_in, b_in, sg_ln_g, sg_ln_b, w_s, b_s, mh_g, w_pa, w_pb, w_out, ln1_g, ln1_b, w_router, b_router, w_gate_e, w_up_e, w_down_e, w_gate_s, w_up_s, w_down_s, w_pg, b_pg, w_p, ln2_g, ln2_b):
    batch, seq, _ = x.shape
    streams = STREAMS if batch % STREAMS == 0 else 1
    sb = batch // streams
    n = sb * seq
    x_all = x.reshape(batch * seq, D_MODEL)
    xf = [x_all] * streams
    x_row0 = [s * n for s in range(streams)]
    xb = [x_all[s * n:(s + 1) * n].astype(BF16) for s in range(streams)]
    p_all = p.reshape(DEPTH, batch * seq, P_DIM)

    wg_all = w_gate_e.reshape(DEPTH * N_EXPERTS, D_MODEL, D_EXPERT)
    wu_all = w_up_e.reshape(DEPTH * N_EXPERTS, D_MODEL, D_EXPERT)
    wd_all = w_down_e.reshape(DEPTH * N_EXPERTS, D_EXPERT, D_MODEL)

    c_uv = 2 * G_WIDTH
    c_qkv = c_uv + 2 * M_QK + M_V
    c_o = c_qkv + M_V
    c_if = c_o + 2 * M_HEADS
    c_gb = c_if + D_MODEL

    w_in_bf = w_in.astype(BF16)

    def layer_weights(l):
        w = w_in_bf[l]
        b = b_in[l][None, :]
        return dict(
            sgu=(w[:, :G_WIDTH], b[:, :G_WIDTH], w[:, G_WIDTH:c_uv], b[:, G_WIDTH:c_uv],
                 sg_ln_g[l][None, :], sg_ln_b[l][None, :], w_s[l], b_s[l].T),
            mproj=(w[:, c_uv:c_qkv], b[:, c_uv:c_qkv], w[:, c_qkv:c_o], b[:, c_qkv:c_o],
                   jnp.pad(w[:, c_o:c_if], ((0, 0), (0, LANES - 2 * M_HEADS))),
                   jnp.pad(b[:, c_o:c_if], ((0, 0), (0, LANES - 2 * M_HEADS)))),
            mhg=mh_g[l][None, :],
            merge=(w[:, c_if:c_gb], b[:, c_if:c_gb], w[:, c_gb:], b[:, c_gb:],
                   w_pa[l].astype(BF16), w_pb[l].astype(BF16), w_out[l].astype(BF16),
                   ln1_g[l][None, :], ln1_b[l][None, :]),
            router=(w_router[l].T, b_router[l][:, None]),
            finish=(w_gate_s[l].astype(BF16), w_up_s[l].astype(BF16), w_down_s[l].astype(BF16),
                    w_pg[l].astype(BF16), b_pg[l][None, :], w_p[l].astype(BF16),
                    ln2_g[l][None, :], ln2_b[l][None, :]))

    weights = [layer_weights(l) for l in range(DEPTH)]
    order = _IssueOrder()
    st = [dict() for _ in range(streams)]
    result = [None]

    def proj(s, l):
        st[s]["qkv"], st[s]["osig"], st[s]["gif"], st[s]["gif_rows"] = _mproj(
            order.before(xb[s]), *weights[l]["mproj"])
        order.after(st[s]["gif"])
        st[s]["a"] = _sgu(order.before(xb[s]), *weights[l]["sgu"])
        order.after(st[s]["a"])

    def recur(s, l):
        st[s]["mb"] = _mlstm(order.before(st[s]["qkv"]), st[s]["gif"], st[s]["gif_rows"], st[s]["osig"],
                             weights[l]["mhg"], sb, seq)
        order.after(st[s]["mb"])

    def merge(s, l):
        xf[s], xb[s], xpa, xpb = _merge(xf[s], order.before(xb[s]), st[s]["a"], st[s]["mb"], *weights[l]["merge"],
                                        x_row0=x_row0[s])
        x_row0[s] = 0
        st[s]["packed"] = (xpa, xpb)
        order.after(xpb)

    def route(s, l):
        st[s]["route"] = _route(order.before(xf[s]), *st[s]["packed"], *weights[l]["router"])
        order.after(st[s]["route"]["wtok"])

    def experts(s, l):
        r = st[s]["route"]
        ysa, ysb = _experts(r["tile_expert"], r["tiles_used"], order.before(r["xsa"]), r["xsb"], l,
                            wg_all, wu_all, wd_all)
        order.after(ysb)
        ga, gb = _sc_gather([ysa, ysb], r["slots_flat"])
        st[s]["gathered"] = (ga.reshape(TOP_K, n, PART), gb.reshape(TOP_K, n, PART))

    def finish(s, l):
        last = l == DEPTH - 1
        outs = _finish(order.before(xf[s]), p_all, l, s, *st[s]["gathered"], st[s]["route"]["wtok"],
                       *weights[l]["finish"], out_base=result[0] if last else None,
                       total_rows=batch * seq if last else None)
        if last:
            result[0], = outs
        else:
            xf[s], xb[s] = outs
        order.after(outs[0])

    if streams == 1:
        for l in range(DEPTH):
            proj(0, l), recur(0, l), merge(0, l), route(0, l), experts(0, l), finish(0, l)
    else:
        proj(0, 0), recur(0, 0), merge(0, 0), route(0, 0), proj(1, 0)
        for l in range(DEPTH):
            more = l + 1 < DEPTH
            experts(0, l), recur(1, l), merge(1, l)
            if more:
                finish(0, l), route(1, l), proj(0, l + 1)
            else:
                route(1, l), finish(0, l)
            experts(1, l)
            if more:
                recur(0, l + 1), merge(0, l + 1)
            finish(1, l)
            if more:
                route(0, l + 1), proj(1, l + 1)
    return result[0].reshape(batch, seq, D_MODEL)
```

```python
import functools

import jax
import jax.numpy as jnp
from jax import lax
from jax.experimental import pallas as pl
from jax.experimental.pallas import tpu as pltpu
from jax.experimental.pallas import tpu_sc as plsc

D_MODEL = 1024
DEPTH = 4
CHUNK = 64
P_DIM = 256
G_WIDTH = 1024
G_GROUPS = 8
G_GROUP_DIM = G_WIDTH // G_GROUPS
G_BLOCK = 128
M_HEADS = 4
M_QK_DIM = 128
M_V_DIM = 256
M_QK = M_HEADS * M_QK_DIM
M_V = M_HEADS * M_V_DIM
N_EXPERTS = 64
TOP_K = 8
N_GROUPS = 8
TOPK_GROUPS = 4
GROUP_SIZE = N_EXPERTS // N_GROUPS
D_EXPERT = 256
D_SHARED = 256
ROUTE_SCALE = 2.5
ALPHA = (2 * DEPTH) ** 0.25
LN_EPS = 1e-5
RMS_EPS = 1e-6

LANES = 128
VMEM_LIMIT = 56 * 1024 * 1024
HALF = D_MODEL // 2
PART = HALF // 2
ROW_TILE = 1024
STREAMS = 2
MERGE_PARTS = 2
ROUTER_PART = 512
MLSTM_CHUNK = 128
MLSTM_GROUP = 8
SC_WINDOW = 128

F32 = jnp.float32
BF16 = jnp.bfloat16
I32 = jnp.int32


def _params(*semantics):
    return pltpu.CompilerParams(dimension_semantics=semantics, vmem_limit_bytes=VMEM_LIMIT)


def _layer_norm(x, g, b):
    mu = jnp.mean(x, axis=-1, keepdims=True)
    xc = x - mu
    var = jnp.mean(xc * xc, axis=-1, keepdims=True)
    return xc * lax.rsqrt(var + LN_EPS) * g + b


def _gelu(x):
    return 0.5 * x * (1.0 + lax.erf(x * (2.0 ** -0.5)))


def _silu(x):
    return x * jax.nn.sigmoid(x)


def _pack_pairs(x):
    lo = lax.bitcast_convert_type(x[:, :PART].astype(BF16).astype(F32), I32)
    hi = lax.bitcast_convert_type(x[:, PART:].astype(BF16).astype(F32), I32)
    return lax.shift_right_logical(lo, 16) | (hi & jnp.int32(-65536))


def _pack_row(x):
    return _pack_pairs(x[:, :HALF]), _pack_pairs(x[:, HALF:])


def _unpack_pairs(w):
    lo = lax.bitcast_convert_type(lax.shift_left(w, 16), F32)
    hi = lax.bitcast_convert_type(w & jnp.int32(-65536), F32)
    return jnp.concatenate([lo, hi], axis=1)


def _mproj_kernel(x_ref, wqkv_ref, bqkv_ref, wo_ref, bo_ref, wif_ref, bif_ref, qkv_ref, osig_ref, gif_ref, gifr_ref):
    x = x_ref[...]
    qkv_ref[...] = (jnp.dot(x, wqkv_ref[...], preferred_element_type=F32) + bqkv_ref[...]).astype(qkv_ref.dtype)
    osig_ref[...] = jax.nn.sigmoid(
        jnp.dot(x, wo_ref[...], preferred_element_type=F32) + bo_ref[...]).astype(osig_ref.dtype)
    gif = jnp.dot(x, wif_ref[...], preferred_element_type=F32) + bif_ref[...]
    gif_ref[...] = gif
    gifr_ref[...] = gif.T[:2 * M_HEADS, :]


def _mproj(x, w_qkv, b_qkv, w_o, b_o, w_if, b_if, tm=1024):
    n, k = x.shape
    row = lambda i: (i, 0)
    const = lambda i: (0, 0)
    widths = (w_qkv.shape[1], w_o.shape[1], w_if.shape[1])
    in_specs = [pl.BlockSpec((tm, k), row)]
    for width in widths:
        in_specs += [pl.BlockSpec((k, width), const), pl.BlockSpec((1, width), const)]
    return pl.pallas_call(
        _mproj_kernel,
        grid=(n // tm,),
        in_specs=in_specs,
        out_specs=[pl.BlockSpec((tm, width), row) for width in widths]
        + [pl.BlockSpec((2 * M_HEADS, tm), lambda i: (0, i))],
        out_shape=[jax.ShapeDtypeStruct((n, widths[0]), BF16), jax.ShapeDtypeStruct((n, widths[1]), BF16),
                   jax.ShapeDtypeStruct((n, widths[2]), F32), jax.ShapeDtypeStruct((2 * M_HEADS, n), F32)],
        name="mlstm_proj",
        compiler_params=_params("parallel"),
    )(x, w_qkv, b_qkv, w_o, b_o, w_if, b_if)


def _sgu_kernel(x_ref, wu_ref, bu_ref, wv_ref, bv_ref, lng_ref, lnb_ref, ws_ref, bs_ref, a_ref, *, blocks):
    x = x_ref[...]
    v = _gelu(jnp.dot(x, wv_ref[...], preferred_element_type=F32) + bv_ref[...])
    vln = _layer_norm(v, lng_ref[...], lnb_ref[...]).astype(BF16)
    u = _gelu(jnp.dot(x, wu_ref[...], preferred_element_type=F32) + bu_ref[...])
    t_chunk = lax.broadcasted_iota(jnp.int32, (G_BLOCK, G_BLOCK), 0) // CHUNK
    s_chunk = lax.broadcasted_iota(jnp.int32, (G_BLOCK, G_BLOCK), 1) // CHUNK
    causal = s_chunk <= t_chunk
    for g in range(G_GROUPS):
        w = jnp.where(causal, ws_ref[g], 0.0).astype(BF16)
        bias = bs_ref[:, g:g + 1]
        cs = slice(g * G_GROUP_DIM, (g + 1) * G_GROUP_DIM)
        for blk in range(blocks):
            rs = slice(blk * G_BLOCK, (blk + 1) * G_BLOCK)
            mixed = jnp.dot(w, vln[rs, cs], preferred_element_type=F32) + bias
            a_ref[rs, cs] = (u[rs, cs] * mixed).astype(a_ref.dtype)


def _sgu(x, w_u, b_u, w_v, b_v, ln_g, ln_b, w_s, b_s_t, blocks=8):
    n = x.shape[0]
    tp = blocks * G_BLOCK
    const = lambda i: (0, 0)
    wspec = pl.BlockSpec((D_MODEL, G_WIDTH), const)
    vspec = pl.BlockSpec((1, G_WIDTH), const)
    return pl.pallas_call(
        functools.partial(_sgu_kernel, blocks=blocks),
        grid=(n // tp,),
        in_specs=[
            pl.BlockSpec((tp, D_MODEL), lambda i: (i, 0)),
            wspec, vspec, wspec, vspec, vspec, vspec,
            pl.BlockSpec((G_GROUPS, G_BLOCK, G_BLOCK), lambda i: (0, 0, 0)),
            pl.BlockSpec((G_BLOCK, G_GROUPS), const),
        ],
        out_specs=pl.BlockSpec((tp, G_WIDTH), lambda i: (i, 0)),
        out_shape=jax.ShapeDtypeStruct((n, G_WIDTH), BF16),
        name="spatial_gating",
        compiler_params=_params("parallel"),
    )(x, w_u, b_u, w_v, b_v, ln_g, ln_b, w_s, b_s_t)


def _segment_cumsum(x, axis, seg):
    pos = lax.broadcasted_iota(jnp.int32, x.shape, axis) % seg
    shift = 1
    while shift < seg:
        x = x + jnp.where(pos >= shift, pltpu.roll(x, shift, axis), 0.0)
        shift *= 2
    return x


def _mlstm_kernel(qkv_ref, gc_ref, gr_ref, osig_ref, mhg_ref, mb_ref, c_ref, n_ref, m_ref, *, chunks):
    @pl.when(pl.program_id(1) == 0)
    def _():
        c_ref[...] = jnp.zeros_like(c_ref)
        n_ref[...] = jnp.zeros_like(n_ref)
        m_ref[...] = jnp.zeros_like(m_ref)

    scale = M_QK_DIM ** -0.5
    gc = gc_ref[...]
    gr = gr_ref[...]
    b_cols = _segment_cumsum(jax.nn.log_sigmoid(gc), 0, MLSTM_CHUNK)
    b_rows = _segment_cumsum(jax.nn.log_sigmoid(gr), 1, MLSTM_CHUNK)
    t_idx = lax.broadcasted_iota(jnp.int32, (MLSTM_CHUNK, MLSTM_CHUNK), 0)
    s_idx = lax.broadcasted_iota(jnp.int32, (MLSTM_CHUNK, MLSTM_CHUNK), 1)
    tri = s_idx <= t_idx

    heads = range(M_HEADS)
    rows = lambda c: slice(c * MLSTM_CHUNK, (c + 1) * MLSTM_CHUNK)

    def local_terms(cs):
        units = [(c, h) for c in cs for h in heads]
        q = {u: qkv_ref[rows(u[0]), u[1] * M_QK_DIM:(u[1] + 1) * M_QK_DIM] for u in units}
        k = {u: qkv_ref[rows(u[0]), M_QK + u[1] * M_QK_DIM:M_QK + (u[1] + 1) * M_QK_DIM] for u in units}
        v = {u: qkv_ref[rows(u[0]), 2 * M_QK + u[1] * M_V_DIM:2 * M_QK + (u[1] + 1) * M_V_DIM] for u in units}
        i_col = {u: gc[rows(u[0]), u[1]:u[1] + 1] for u in units}
        b_col = {u: b_cols[rows(u[0]), M_HEADS + u[1]:M_HEADS + u[1] + 1] for u in units}
        ib_row = {u: (b_rows[M_HEADS + u[1]:M_HEADS + u[1] + 1, rows(u[0])]
                      - gr[u[1]:u[1] + 1, rows(u[0])]) for u in units}
        d = {u: jnp.where(tri, b_col[u] - ib_row[u], -jnp.inf) for u in units}
        d_max = {u: jnp.max(d[u], axis=1, keepdims=True) for u in units}
        qk = {u: lax.dot_general(q[u], k[u], (((1,), (1,)), ((), ())), preferred_element_type=F32) for u in units}
        s_loc = {u: qk[u] * scale * jnp.exp(d[u] - d_max[u]) for u in units}
        a_loc = {u: jnp.dot(s_loc[u].astype(BF16), v[u], preferred_element_type=F32) for u in units}
        r_loc = {u: jnp.sum(s_loc[u], axis=1, keepdims=True) for u in units}
        b_last = {u: b_col[u][MLSTM_CHUNK - 1:MLSTM_CHUNK, :] for u in units}
        g_col = {u: b_last[u] - b_col[u] + i_col[u] for u in units}
        g_max = {u: jnp.max(g_col[u], axis=0, keepdims=True) for u in units}
        wk = {u: jnp.exp(g_col[u] - g_max[u]) * k[u].astype(F32) for u in units}
        u_loc = {u: jnp.dot(wk[u].T.astype(BF16), v[u], preferred_element_type=F32) for u in units}
        nk_loc = {u: jnp.sum(wk[u], axis=0, keepdims=True) for u in units}
        return dict(q=q, b_col=b_col, d_max=d_max, a_loc=a_loc, r_loc=r_loc, b_last=b_last, g_max=g_max,
                    u_loc=u_loc, nk_loc=nk_loc)

    def carried_step(c, loc, c_state, n_state, m_state):
        inter = [loc["b_col"][(c, h)] + m_state[h] for h in heads]
        m_t = [jnp.maximum(inter[h], loc["d_max"][(c, h)]) for h in heads]
        w_intra = [jnp.exp(loc["d_max"][(c, h)] - m_t[h]) for h in heads]
        w_inter = [jnp.exp(inter[h] - m_t[h]) for h in heads]
        qc = [jnp.dot(loc["q"][(c, h)], c_state[h].astype(BF16), preferred_element_type=F32) * scale for h in heads]
        qn = [jnp.sum(loc["q"][(c, h)].astype(F32) * n_state[h], axis=1, keepdims=True) * scale for h in heads]
        m_new = [jnp.maximum(loc["b_last"][(c, h)] + m_state[h], loc["g_max"][(c, h)]) for h in heads]
        decay = [jnp.exp(loc["b_last"][(c, h)] + m_state[h] - m_new[h]) for h in heads]
        beta = [jnp.exp(loc["g_max"][(c, h)] - m_new[h]) for h in heads]
        c_next = [decay[h] * c_state[h] + beta[h] * loc["u_loc"][(c, h)] for h in heads]
        n_next = [decay[h] * n_state[h] + beta[h] * loc["nk_loc"][(c, h)] for h in heads]
        num = [w_intra[h] * loc["a_loc"][(c, h)] + w_inter[h] * qc[h] for h in heads]
        den = [w_intra[h] * loc["r_loc"][(c, h)] + w_inter[h] * qn[h] for h in heads]
        hv = [num[h] * (1.0 / jnp.maximum(jnp.abs(den[h]), jnp.exp(-m_t[h]))) for h in heads]
        hv = [hv[h] * lax.rsqrt(jnp.mean(hv[h] * hv[h], axis=1, keepdims=True) + RMS_EPS) for h in heads]
        for h in heads:
            vs = slice(h * M_V_DIM, (h + 1) * M_V_DIM)
            mb_ref[rows(c), vs] = (osig_ref[rows(c), vs].astype(F32) * (hv[h] * mhg_ref[:, vs])).astype(mb_ref.dtype)
        return c_next, n_next, m_new

    c_state = [c_ref[h] for h in heads]
    n_state = [n_ref[h] for h in heads]
    m_state = [m_ref[h][:, 0:1] for h in heads]
    groups = [list(range(g, min(g + MLSTM_GROUP, chunks))) for g in range(0, chunks, MLSTM_GROUP)]
    loc = local_terms(groups[0])
    for gi, group in enumerate(groups):
        loc_next = local_terms(groups[gi + 1]) if gi + 1 < len(groups) else None
        for c in group:
            c_state, n_state, m_state = carried_step(c, loc, c_state, n_state, m_state)
        loc = loc_next

    for h in heads:
        c_ref[h], n_ref[h] = c_state[h], n_state[h]
        m_ref[h] = jnp.broadcast_to(m_state[h], (1, LANES))


def _mlstm(qkv, gates_col, gates_row, osig_src, mh_g, batch, seq, chunks=8):
    ts = chunks * MLSTM_CHUNK
    tiles = seq // ts
    n = batch * seq
    return pl.pallas_call(
        functools.partial(_mlstm_kernel, chunks=chunks),
        grid=(batch, tiles),
        in_specs=[
            pl.BlockSpec((ts, 2 * M_QK + M_V), lambda b, t: (b * tiles + t, 0)),
            pl.BlockSpec((ts, LANES), lambda b, t: (b * tiles + t, 0)),
            pl.BlockSpec((2 * M_HEADS, ts), lambda b, t: (0, b * tiles + t)),
            pl.BlockSpec((ts, M_V), lambda b, t: (b * tiles + t, 0)),
            pl.BlockSpec((1, M_V), lambda b, t: (0, 0)),
        ],
        out_specs=pl.BlockSpec((ts, M_V), lambda b, t: (b * tiles + t, 0)),
        out_shape=jax.ShapeDtypeStruct((n, M_V), BF16),
        scratch_shapes=[
            pltpu.VMEM((M_HEADS, M_QK_DIM, M_V_DIM), F32),
            pltpu.VMEM((M_HEADS, 1, M_QK_DIM), F32),
            pltpu.VMEM((M_HEADS, 1, LANES), F32),
        ],
        name="mlstm",
        compiler_params=_params("parallel", "arbitrary"),
    )(qkv, gates_col, gates_row, osig_src, mh_g)


def _merge_kernel(x_ref, xin_ref, a_ref, mb_ref, wga_ref, bga_ref, wgb_ref, bgb_ref, wpa_ref, wpb_ref, wout_ref,
                  g_ref, b_ref, xo_ref, xob_ref, xpa_ref, xpb_ref):
    tm = x_ref.shape[0]
    halves = [slice(r, r + tm // MERGE_PARTS) for r in range(0, tm, tm // MERGE_PARTS)]
    dot = lambda lhs, w_ref: jnp.dot(lhs, w_ref[...], preferred_element_type=F32)
    ga = [jax.nn.sigmoid(dot(xin_ref[rs, :], wga_ref) + bga_ref[...]) for rs in halves]
    ya = [dot(a_ref[rs, :], wpa_ref) for rs in halves]
    gb = [jax.nn.sigmoid(dot(xin_ref[rs, :], wgb_ref) + bgb_ref[...]) for rs in halves]
    yb = [dot(mb_ref[rs, :], wpb_ref) for rs in halves]
    y = [(ga[h] * ya[h] + gb[h] * yb[h]).astype(BF16) for h in range(MERGE_PARTS)]
    mix = [dot(y[h], wout_ref) for h in range(MERGE_PARTS)]
    for h, rs in enumerate(halves):
        xn = _layer_norm(ALPHA * x_ref[rs, :] + mix[h], g_ref[...], b_ref[...])
        xo_ref[rs, :] = xn
        xob_ref[rs, :] = xn.astype(BF16)
        xpa_ref[rs, :], xpb_ref[rs, :] = _pack_row(xn)


def _merge(x, xin, a, mb, w_ga, b_ga, w_gb, b_gb, w_pa, w_pb, w_out, ln_g, ln_b, x_row0=0, tm=512):
    n = xin.shape[0]
    row = lambda i: (i, 0)
    const = lambda i: (0, 0)
    wspec = pl.BlockSpec((D_MODEL, D_MODEL), const)
    vspec = pl.BlockSpec((1, D_MODEL), const)
    return pl.pallas_call(
        _merge_kernel,
        grid=(n // tm,),
        in_specs=[
            pl.BlockSpec((tm, D_MODEL), lambda i: (x_row0 // tm + i, 0)),
            pl.BlockSpec((tm, D_MODEL), row),
            pl.BlockSpec((tm, G_WIDTH), row),
            pl.BlockSpec((tm, M_V), row),
            wspec, vspec, wspec, vspec, wspec, wspec, wspec, vspec, vspec,
        ],
        out_specs=[pl.BlockSpec((tm, D_MODEL), row), pl.BlockSpec((tm, D_MODEL), row),
                   pl.BlockSpec((tm, PART), row), pl.BlockSpec((tm, PART), row)],
        out_shape=[jax.ShapeDtypeStruct((n, D_MODEL), F32), jax.ShapeDtypeStruct((n, D_MODEL), BF16),
                   jax.ShapeDtypeStruct((n, PART), I32), jax.ShapeDtypeStruct((n, PART), I32)],
        name="merge_ln",
        compiler_params=_params("parallel"),
    )(x, xin, a, mb, w_ga, b_ga, w_gb, b_gb, w_pa, w_pb, w_out, ln_g, ln_b)


def _first_max(v, idx, axes, sentinel):
    m = jnp.max(v, axis=axes, keepdims=True)
    first = jnp.min(jnp.where(v == m, idx, sentinel), axis=axes, keepdims=True)
    return m, first


def _router_kernel(x_ref, wr_ref, br_ref, upper_ref, eidx_ref, rank_ref, wtok_ref, cnt_ref, run_ref, *, tm):
    @pl.when(pl.program_id(0) == 0)
    def _():
        run_ref[...] = jnp.zeros_like(run_ref)

    tw = ROUTER_PART
    parts = range(tm // tw)
    shape3 = (N_GROUPS, GROUP_SIZE, tw)
    neg = -jnp.inf

    def split(v):
        hi = v.astype(BF16)
        return hi, (v - hi.astype(F32)).astype(BF16)

    nt = lambda a, b: lax.dot_general(a, b, (((1,), (1,)), ((), ())), preferred_element_type=F32)
    w_hi, w_lo = split(wr_ref[...])
    xs = [split(x_ref[i * tw:(i + 1) * tw, :]) for i in parts]
    logits = [nt(w_hi, x_hi) + (nt(w_hi, x_lo) + nt(w_lo, x_hi)) for x_hi, x_lo in xs]
    scores = [jax.nn.sigmoid(lg) for lg in logits]
    sel = [(sc + br_ref[...]).reshape(shape3) for sc in scores]
    scores3 = [sc.reshape(shape3) for sc in scores]
    member = lax.broadcasted_iota(jnp.int32, shape3, 1)
    group = lax.broadcasted_iota(jnp.int32, shape3, 0)

    best = [_first_max(v, member, 1, GROUP_SIZE) for v in sel]
    second = [jnp.max(jnp.where(member == f1, neg, v), axis=1, keepdims=True) for v, (_, f1) in zip(sel, best)]
    gscore = [m1 + m2 for (m1, _), m2 in zip(best, second)]
    gid = lax.broadcasted_iota(jnp.int32, gscore[0].shape, 0)
    gmask = [jnp.zeros(gscore[0].shape, dtype=jnp.bool_) for _ in parts]
    for _ in range(TOPK_GROUPS):
        hits = [gid == _first_max(g, gid, 0, N_GROUPS)[1] for g in gscore]
        gmask = [m | h for m, h in zip(gmask, hits)]
        gscore = [jnp.where(h, neg, g) for h, g in zip(hits, gscore)]

    cand = [jnp.where(m, v, neg) for m, v in zip(gmask, sel)]
    eid = group * GROUP_SIZE + member
    chosen = [jnp.zeros(shape3, dtype=jnp.bool_) for _ in parts]
    picks = [[] for _ in parts]
    for _ in range(TOP_K):
        firsts = [_first_max(c, eid, (0, 1), N_EXPERTS)[1] for c in cand]
        hits = [eid == fe for fe in firsts]
        chosen = [c | h for c, h in zip(chosen, hits)]
        cand = [jnp.where(h, neg, c) for h, c in zip(hits, cand)]
        picked = [jnp.sum(jnp.where(h, s3, 0.0), axis=(0, 1), keepdims=True)[0] for h, s3 in zip(hits, scores3)]
        for i in parts:
            picks[i].append((firsts[i][0], hits[i], picked[i]))

    chosen2 = [jnp.where(c, 1.0, 0.0).reshape(N_EXPERTS, tw) for c in chosen]
    counts = [jnp.sum(c2, axis=1, keepdims=True) for c2 in chosen2]
    bases = []
    base = run_ref[:, 0:1]
    for i in parts:
        bases.append(base)
        base = base + counts[i]
    prefix3 = [(jnp.dot(c2.astype(BF16), upper_ref[...], preferred_element_type=F32) + b0).reshape(shape3)
               for c2, b0 in zip(chosen2, bases)]

    for i in parts:
        cols = slice(i * tw, (i + 1) * tw)
        total = picks[i][0][2]
        for _, _, wk in picks[i][1:]:
            total = total + wk
        eidx_ref[:, cols] = jnp.concatenate([fe for fe, _, _ in picks[i]], axis=0)
        rank_ref[:, cols] = jnp.concatenate(
            [jnp.sum(jnp.where(hit, prefix3[i], 0.0), axis=(0, 1), keepdims=True)[0] for _, hit, _ in picks[i]],
            axis=0).astype(I32)
        w_rows = jnp.concatenate([wk / total * ROUTE_SCALE for _, _, wk in picks[i]]
                                 + [jnp.zeros((LANES - TOP_K, tw), F32)], axis=0)
        wtok_ref[cols, :] = w_rows.T

    run = run_ref[...] + (base - run_ref[:, 0:1])
    run_ref[...] = run
    cnt_ref[...] = run


def _router(x, wr_t, br, tm=1024):
    n = x.shape[0]
    tok = lambda i: (0, i)
    upper = (jnp.arange(ROUTER_PART)[:, None] < jnp.arange(ROUTER_PART)[None, :]).astype(BF16)
    return pl.pallas_call(
        functools.partial(_router_kernel, tm=tm),
        grid=(n // tm,),
        in_specs=[
            pl.BlockSpec((tm, D_MODEL), lambda i: (i, 0)),
            pl.BlockSpec((N_EXPERTS, D_MODEL), lambda i: (0, 0)),
            pl.BlockSpec((N_EXPERTS, 1), lambda i: (0, 0)),
            pl.BlockSpec((ROUTER_PART, ROUTER_PART), lambda i: (0, 0)),
        ],
        out_specs=[
            pl.BlockSpec((TOP_K, tm), tok),
            pl.BlockSpec((TOP_K, tm), tok),
            pl.BlockSpec((tm, LANES), lambda i: (i, 0)),
            pl.BlockSpec((N_EXPERTS, LANES), lambda i: (0, 0)),
        ],
        out_shape=[
            jax.ShapeDtypeStruct((TOP_K, n), I32),
            jax.ShapeDtypeStruct((TOP_K, n), I32),
            jax.ShapeDtypeStruct((n, LANES), F32),
            jax.ShapeDtypeStruct((N_EXPERTS, LANES), F32),
        ],
        scratch_shapes=[pltpu.VMEM((N_EXPERTS, LANES), F32)],
        name="router",
        compiler_params=_params("arbitrary"),
    )(x, wr_t, br, upper)


def _slots_kernel(starts_ref, eidx_ref, rank_ref, slot_ref):
    eidx = eidx_ref[...]
    slot = rank_ref[...]
    for e in range(N_EXPERTS):
        slot = slot + jnp.where(eidx == e, starts_ref[e], 0)
    slot_ref[...] = slot


def _slots(starts, eidx, rank, tn=2048):
    n = eidx.shape[1]
    tn = min(tn, n)
    spec = pl.BlockSpec((TOP_K, tn), lambda i, s: (0, i))
    return pl.pallas_call(
        _slots_kernel,
        grid_spec=pltpu.PrefetchScalarGridSpec(
            num_scalar_prefetch=1, grid=(n // tn,), in_specs=[spec, spec], out_specs=spec),
        out_shape=jax.ShapeDtypeStruct((TOP_K, n), I32),
        name="slots",
        compiler_params=_params("parallel"),
    )(starts, eidx, rank)


def _sc_mesh():
    return plsc.VectorSubcoreMesh(core_axis_name="c", subcore_axis_name="s")


def _sc_dispatch(parts, slots_flat, rows):
    n, width = parts[0].shape
    blocks = n // SC_WINDOW
    out_type = [jax.ShapeDtypeStruct((rows, width), part.dtype) for part in parts]

    @functools.partial(pl.kernel, out_type=out_type, mesh=_sc_mesh(), scratch_types=[], name="sc_dispatch")
    def run(*refs):
        i_hbm = refs[len(parts)]
        for x_hbm, o_hbm in zip(refs[:len(parts)], refs[len(parts) + 1:]):
            def body(x_vmem, i_vmem, o_hbm=o_hbm):
                pltpu.sync_copy(x_vmem, o_hbm.at[i_vmem.at[0]])

            pltpu.emit_pipeline(
                body,
                grid=(blocks, TOP_K),
                in_specs=[pl.BlockSpec((SC_WINDOW, width), lambda i, k: (i, 0)),
                          pl.BlockSpec((1, SC_WINDOW), lambda i, k: (0, k * blocks + i))],
                out_specs=[],
                core_axis_name=("c", "s"),
                dimension_semantics=(pltpu.PARALLEL, pltpu.ARBITRARY),
                trace_scopes=False,
            )(x_hbm, i_hbm)

    return run(*parts, slots_flat)


def _sc_gather(tables, idx_flat):
    count = idx_flat.shape[1]
    width = tables[0].shape[1]
    out_type = [jax.ShapeDtypeStruct((count, width), table.dtype) for table in tables]

    @functools.partial(pl.kernel, out_type=out_type, mesh=_sc_mesh(), scratch_types=[], name="sc_gather")
    def run(*refs):
        i_hbm = refs[len(tables)]
        for t_hbm, o_hbm in zip(refs[:len(tables)], refs[len(tables) + 1:]):
            def body(i_vmem, o_vmem, t_hbm=t_hbm):
                pltpu.sync_copy(t_hbm.at[i_vmem.at[0]], o_vmem)

            pltpu.emit_pipeline(
                body,
                grid=(count // SC_WINDOW,),
                in_specs=[pl.BlockSpec((1, SC_WINDOW), lambda i: (0, i))],
                out_specs=[pl.BlockSpec((SC_WINDOW, width), lambda i: (i, 0))],
                core_axis_name=("c", "s"),
                dimension_semantics=(pltpu.PARALLEL,),
                trace_scopes=False,
            )(i_hbm, o_hbm)

    return run(*tables, idx_flat)


def _expert_kernel(te_ref, used_ref, xa_ref, xb_ref, wg_ref, wu_ref, wd_ref, oa_ref, ob_ref, wgb_ref, wub_ref, wdb_ref):
    j = pl.program_id(0)

    @pl.when((j == 0) | (te_ref[j] != te_ref[jnp.maximum(j - 1, 0)]))
    def _():
        wgb_ref[...] = wg_ref[0].astype(BF16)
        wub_ref[...] = wu_ref[0].astype(BF16)
        wdb_ref[...] = wd_ref[0].astype(BF16)

    @pl.when(j < used_ref[0])
    def _():
        x = jnp.concatenate([_unpack_pairs(xa_ref[...]), _unpack_pairs(xb_ref[...])], axis=1).astype(BF16)
        hg = jnp.dot(x, wgb_ref[...], preferred_element_type=F32)
        hu = jnp.dot(x, wub_ref[...], preferred_element_type=F32)
        hid = (_silu(hg) * hu).astype(BF16)
        oa_ref[...], ob_ref[...] = _pack_row(jnp.dot(hid, wdb_ref[...], preferred_element_type=F32))


def _experts(tile_expert, tiles_used, xa, xb, layer, wg, wu, wd):
    rows = xa.shape[0]
    last = lambda j, used: jnp.minimum(j, used[0] - 1)
    row = lambda j, te, used: (last(j, used), 0)
    exp = lambda j, te, used: (layer * N_EXPERTS + te[last(j, used)], 0, 0)
    return pl.pallas_call(
        _expert_kernel,
        grid_spec=pltpu.PrefetchScalarGridSpec(
            num_scalar_prefetch=2,
            grid=(rows // ROW_TILE,),
            in_specs=[
                pl.BlockSpec((ROW_TILE, PART), row),
                pl.BlockSpec((ROW_TILE, PART), row),
                pl.BlockSpec((1, D_MODEL, D_EXPERT), exp),
                pl.BlockSpec((1, D_MODEL, D_EXPERT), exp),
                pl.BlockSpec((1, D_EXPERT, D_MODEL), exp),
            ],
            out_specs=[pl.BlockSpec((ROW_TILE, PART), row), pl.BlockSpec((ROW_TILE, PART), row)],
            scratch_shapes=[pltpu.VMEM((D_MODEL, D_EXPERT), BF16), pltpu.VMEM((D_MODEL, D_EXPERT), BF16),
                            pltpu.VMEM((D_EXPERT, D_MODEL), BF16)],
        ),
        out_shape=[jax.ShapeDtypeStruct((rows, PART), I32), jax.ShapeDtypeStruct((rows, PART), I32)],
        name="experts",
        compiler_params=_params("arbitrary"),
    )(tile_expert, tiles_used, xa, xb, wg, wu, wd)


def _finish_kernel(x_ref, p_ref, ga_ref, gb_ref, wtok_ref, wgs_ref, wus_ref, wds_ref, wpg_ref, bpg_ref, wp_ref,
                   lng_ref, lnb_ref, *out_refs):
    xb = x_ref[...].astype(BF16)
    hs = _silu(jnp.dot(xb, wgs_ref[...], preferred_element_type=F32)) * jnp.dot(
        xb, wus_ref[...], preferred_element_type=F32)
    shared = jnp.dot(hs.astype(BF16), wds_ref[...], preferred_element_type=F32)
    gate = jax.nn.sigmoid(jnp.dot(xb, wpg_ref[...], preferred_element_type=F32) + bpg_ref[...])
    ple = gate * jnp.dot(p_ref[0].astype(BF16), wp_ref[...], preferred_element_type=F32)

    wtok = wtok_ref[...]
    r_a = None
    r_b = None
    for k in range(TOP_K):
        wk = wtok[:, k:k + 1]
        a = wk * _unpack_pairs(ga_ref[k])
        b = wk * _unpack_pairs(gb_ref[k])
        r_a = a if r_a is None else r_a + a
        r_b = b if r_b is None else r_b + b
    routed = jnp.concatenate([r_a, r_b], axis=1)

    xn = _layer_norm(ALPHA * x_ref[...] + (routed + shared + ple), lng_ref[...], lnb_ref[...])
    if len(out_refs) == 2 and out_refs[1].dtype == BF16:
        out_refs[0][...] = xn
        out_refs[1][...] = xn.astype(BF16)
    else:
        out_refs[-1][...] = xn


def _finish(x, p_all, layer, stream, ga, gb, wtok, wgs, wus, wds, wpg, bpg, wp, ln_g, ln_b,
            out_base=None, total_rows=None, tm=512):
    n = x.shape[0]
    p_block = stream * (n // tm)
    row = lambda i: (i, 0)
    const = lambda i: (0, 0)
    if total_rows is None:
        out_specs = [pl.BlockSpec((tm, D_MODEL), row), pl.BlockSpec((tm, D_MODEL), row)]
        out_shape = [jax.ShapeDtypeStruct((n, D_MODEL), F32), jax.ShapeDtypeStruct((n, D_MODEL), BF16)]
    else:
        out_specs = [pl.BlockSpec((tm, D_MODEL), lambda i: (p_block + i, 0))]
        out_shape = [jax.ShapeDtypeStruct((total_rows, D_MODEL), F32)]
    extra_specs = [] if out_base is None else [pl.BlockSpec(memory_space=pl.ANY)]
    extra_args = [] if out_base is None else [out_base]
    aliases = {} if out_base is None else {13: 0}
    return pl.pallas_call(
        _finish_kernel,
        grid=(n // tm,),
        input_output_aliases=aliases,
        in_specs=[
            pl.BlockSpec((tm, D_MODEL), row),
            pl.BlockSpec((1, tm, P_DIM), lambda i: (layer, p_block + i, 0)),
            pl.BlockSpec((TOP_K, tm, PART), lambda i: (0, i, 0)),
            pl.BlockSpec((TOP_K, tm, PART), lambda i: (0, i, 0)),
            pl.BlockSpec((tm, LANES), row),
            pl.BlockSpec((D_MODEL, D_SHARED), const),
            pl.BlockSpec((D_MODEL, D_SHARED), const),
            pl.BlockSpec((D_SHARED, D_MODEL), const),
            pl.BlockSpec((D_MODEL, D_MODEL), const),
            pl.BlockSpec((1, D_MODEL), const),
            pl.BlockSpec((P_DIM, D_MODEL), const),
            pl.BlockSpec((1, D_MODEL), const),
            pl.BlockSpec((1, D_MODEL), const),
        ] + extra_specs,
        out_specs=out_specs,
        out_shape=out_shape,
        name="moe_finish_ln",
        compiler_params=_params("parallel"),
    )(x, p_all, ga, gb, wtok, wgs, wus, wds, wpg, bpg, wp, ln_g, ln_b, *extra_args)


def _route(x, xpa, xpb, w_router_t, b_router):
    n = x.shape[0]
    pairs = n * TOP_K
    tiles = pairs // ROW_TILE + N_EXPERTS
    eidx, rank, wtok, counts = _router(x, w_router_t, b_router)

    cnt = counts[:, 0].astype(I32)
    group_tiles = (cnt + ROW_TILE - 1) // ROW_TILE
    tile_end = jnp.cumsum(group_tiles)
    starts = (tile_end - group_tiles) * ROW_TILE
    tile_id = jnp.arange(tiles, dtype=I32)
    tile_expert = jnp.minimum(jnp.sum((tile_end[None, :] <= tile_id[:, None]).astype(I32), axis=1), N_EXPERTS - 1)

    slots_flat = _slots(starts, eidx, rank).reshape(1, pairs)
    xsa, xsb = _sc_dispatch([xpa, xpb], slots_flat, tiles * ROW_TILE)
    return dict(xsa=xsa, xsb=xsb, slots_flat=slots_flat, wtok=wtok, tile_expert=tile_expert,
                tiles_used=tile_end[N_EXPERTS - 1:])


class _IssueOrder:
    def __init__(self):
        self.last = None

    def before(self, operand):
        if self.last is None:
            return operand
        self.last, operand = lax.optimization_barrier((self.last, operand))
        return operand

    def after(self, result):
        self.last = result


def kernel(x, p, w_in, b_in, sg_ln_g, sg_ln_b, w_s, b_s, mh_g, w_pa, w_pb, w_out, ln1_g, ln1_b, w_router, b_router, w_gate_e, w_up_e, w_down_e, w_gate_s, w_up_s, w_down_s, w_pg, b_pg, w_p, ln2_g, ln2_b):
    batch, seq, _ = x.shape
    streams = STREAMS if batch % STREAMS == 0 else 1
    sb = batch // streams
    n = sb * seq
    x_all = x.reshape(batch * seq, D_MODEL)
    xf = [x_all] * streams
    x_row0 = [s * n for s in range(streams)]
    xb = [x_all[s * n:(s + 1) * n].astype(BF16) for s in range(streams)]
    p_all = p.reshape(DEPTH, batch * seq, P_DIM)

    wg_all = w_gate_e.reshape(DEPTH * N_EXPERTS, D_MODEL, D_EXPERT)
    wu_all = w_up_e.reshape(DEPTH * N_EXPERTS, D_MODEL, D_EXPERT)
    wd_all = w_down_e.reshape(DEPTH * N_EXPERTS, D_EXPERT, D_MODEL)

    c_uv = 2 * G_WIDTH
    c_qkv = c_uv + 2 * M_QK + M_V
    c_o = c_qkv + M_V
    c_if = c_o + 2 * M_HEADS
    c_gb = c_if + D_MODEL

    w_in_bf = w_in.astype(BF16)

    def layer_weights(l):
        w = w_in_bf[l]
        b = b_in[l][None, :]
        return dict(
            sgu=(w[:, :G_WIDTH], b[:, :G_WIDTH], w[:, G_WIDTH:c_uv], b[:, G_WIDTH:c_uv],
                 sg_ln_g[l][None, :], sg_ln_b[l][None, :], w_s[l], b_s[l].T),
            mproj=(w[:, c_uv:c_qkv], b[:, c_uv:c_qkv], w[:, c_qkv:c_o], b[:, c_qkv:c_o],
                   jnp.pad(w[:, c_o:c_if], ((0, 0), (0, LANES - 2 * M_HEADS))),
                   jnp.pad(b[:, c_o:c_if], ((0, 0), (0, LANES - 2 * M_HEADS)))),
            mhg=mh_g[l][None, :],
            merge=(w[:, c_if:c_gb], b[:, c_if:c_gb], w[:, c_gb:], b[:, c_gb:],
                   w_pa[l].astype(BF16), w_pb[l].astype(BF16), w_out[l].astype(BF16),
                   ln1_g[l][None, :], ln1_b[l][None, :]),
            router=(w_router[l].T, b_router[l][:, None]),
            finish=(w_gate_s[l].astype(BF16), w_up_s[l].astype(BF16), w_down_s[l].astype(BF16),
                    w_pg[l].astype(BF16), b_pg[l][None, :], w_p[l].astype(BF16),
                    ln2_g[l][None, :], ln2_b[l][None, :]))

    weights = [layer_weights(l) for l in range(DEPTH)]
    order = _IssueOrder()
    st = [dict() for _ in range(streams)]
    result = [None]

    def proj(s, l):
        st[s]["qkv"], st[s]["osig"], st[s]["gif"], st[s]["gif_rows"] = _mproj(
            order.before(xb[s]), *weights[l]["mproj"])
        order.after(st[s]["gif"])
        st[s]["a"] = _sgu(order.before(xb[s]), *weights[l]["sgu"])
        order.after(st[s]["a"])

    def recur(s, l):
        st[s]["mb"] = _mlstm(order.before(st[s]["qkv"]), st[s]["gif"], st[s]["gif_rows"], st[s]["osig"],
                             weights[l]["mhg"], sb, seq)
        order.after(st[s]["mb"])

    def merge(s, l):
        xf[s], xb[s], xpa, xpb = _merge(xf[s], order.before(xb[s]), st[s]["a"], st[s]["mb"], *weights[l]["merge"],
                                        x_row0=x_row0[s])
        x_row0[s] = 0
        st[s]["packed"] = (xpa, xpb)
        order.after(xpb)

    def route(s, l):
        st[s]["route"] = _route(order.before(xf[s]), *st[s]["packed"], *weights[l]["router"])
        order.after(st[s]["route"]["wtok"])

    def experts(s, l):
        r = st[s]["route"]
        ysa, ysb = _experts(r["tile_expert"], r["tiles_used"], order.before(r["xsa"]), r["xsb"], l,
                            wg_all, wu_all, wd_all)
        order.after(ysb)
        ga, gb = _sc_gather([ysa, ysb], r["slots_flat"])
        st[s]["gathered"] = (ga.reshape(TOP_K, n, PART), gb.reshape(TOP_K, n, PART))

    def finish(s, l):
        last = l == DEPTH - 1
        outs = _finish(order.before(xf[s]), p_all, l, s, *st[s]["gathered"], st[s]["route"]["wtok"],
                       *weights[l]["finish"], out_base=result[0] if last else None,
                       total_rows=batch * seq if last else None)
        if last:
            result[0], = outs
        else:
            xf[s], xb[s] = outs
        order.after(outs[0])

    if streams == 1:
        for l in range(DEPTH):
            proj(0, l), recur(0, l), merge(0, l), route(0, l), experts(0, l), finish(0, l)
    else:
        proj(0, 0), recur(0, 0), merge(0, 0), route(0, 0), proj(1, 0)
        for l in range(DEPTH):
            more = l + 1 < DEPTH
            experts(0, l), recur(1, l), merge(1, l)
            if more:
                finish(0, l), route(1, l), proj(0, l + 1)
            else:
                route(1, l), finish(0, l)
            experts(1, l)
            if more:
                recur(0, l + 1), merge(0, l + 1)
            finish(1, l)
            if more:
                route(0, l + 1), proj(1, l + 1)
    return result[0].reshape(batch, seq, D_MODEL)
```

```python
import functools

import jax
import jax.numpy as jnp
from jax import lax
from jax.experimental import pallas as pl
from jax.experimental.pallas import tpu as pltpu
from jax.experimental.pallas import tpu_sc as plsc

D_MODEL = 1024
DEPTH = 4
CHUNK = 64
P_DIM = 256
G_WIDTH = 1024
G_GROUPS = 8
G_GROUP_DIM = G_WIDTH // G_GROUPS
G_BLOCK = 128
M_HEADS = 4
M_QK_DIM = 128
M_V_DIM = 256
M_QK = M_HEADS * M_QK_DIM
M_V = M_HEADS * M_V_DIM
N_EXPERTS = 64
TOP_K = 8
N_GROUPS = 8
TOPK_GROUPS = 4
GROUP_SIZE = N_EXPERTS // N_GROUPS
D_EXPERT = 256
D_SHARED = 256
ROUTE_SCALE = 2.5
ALPHA = (2 * DEPTH) ** 0.25
LN_EPS = 1e-5
RMS_EPS = 1e-6

LANES = 128
VMEM_LIMIT = 56 * 1024 * 1024
HALF = D_MODEL // 2
PART = HALF // 2
ROW_TILE = 1024
STREAMS = 2
MERGE_PARTS = 2
ROUTER_PART = 512
MLSTM_CHUNK = 128
MLSTM_GROUP = 8
SC_WINDOW = 128

F32 = jnp.float32
BF16 = jnp.bfloat16
I32 = jnp.int32


def _params(*semantics):
    return pltpu.CompilerParams(dimension_semantics=semantics, vmem_limit_bytes=VMEM_LIMIT)


def _layer_norm(x, g, b):
    mu = jnp.mean(x, axis=-1, keepdims=True)
    xc = x - mu
    var = jnp.mean(xc * xc, axis=-1, keepdims=True)
    return xc * lax.rsqrt(var + LN_EPS) * g + b


def _gelu(x):
    return 0.5 * x * (1.0 + lax.erf(x * (2.0 ** -0.5)))


def _silu(x):
    return x * jax.nn.sigmoid(x)


def _pack_pairs(x):
    lo = lax.bitcast_convert_type(x[:, :PART].astype(BF16).astype(F32), I32)
    hi = lax.bitcast_convert_type(x[:, PART:].astype(BF16).astype(F32), I32)
    return lax.shift_right_logical(lo, 16) | (hi & jnp.int32(-65536))


def _pack_row(x):
    return _pack_pairs(x[:, :HALF]), _pack_pairs(x[:, HALF:])


def _unpack_pairs(w):
    lo = lax.bitcast_convert_type(lax.shift_left(w, 16), F32)
    hi = lax.bitcast_convert_type(w & jnp.int32(-65536), F32)
    return jnp.concatenate([lo, hi], axis=1)


def _mproj_kernel(x_ref, wqkv_ref, bqkv_ref, wo_ref, bo_ref, wif_ref, bif_ref, qkv_ref, osig_ref, gif_ref, gifr_ref):
    x = x_ref[...]
    qkv_ref[...] = (jnp.dot(x, wqkv_ref[...], preferred_element_type=F32) + bqkv_ref[...]).astype(qkv_ref.dtype)
    osig_ref[...] = jax.nn.sigmoid(
        jnp.dot(x, wo_ref[...], preferred_element_type=F32) + bo_ref[...]).astype(osig_ref.dtype)
    gif = jnp.dot(x, wif_ref[...], preferred_element_type=F32) + bif_ref[...]
    gif_ref[...] = gif
    gifr_ref[...] = gif.T[:2 * M_HEADS, :]


def _mproj(x, w_qkv, b_qkv, w_o, b_o, w_if, b_if, tm=1024):
    n, k = x.shape
    row = lambda i: (i, 0)
    const = lambda i: (0, 0)
    widths = (w_qkv.shape[1], w_o.shape[1], w_if.shape[1])
    in_specs = [pl.BlockSpec((tm, k), row)]
    for width in widths:
        in_specs += [pl.BlockSpec((k, width), const), pl.BlockSpec((1, width), const)]
    return pl.pallas_call(
        _mproj_kernel,
        grid=(n // tm,),
        in_specs=in_specs,
        out_specs=[pl.BlockSpec((tm, width), row) for width in widths]
        + [pl.BlockSpec((2 * M_HEADS, tm), lambda i: (0, i))],
        out_shape=[jax.ShapeDtypeStruct((n, widths[0]), BF16), jax.ShapeDtypeStruct((n, widths[1]), BF16),
                   jax.ShapeDtypeStruct((n, widths[2]), F32), jax.ShapeDtypeStruct((2 * M_HEADS, n), F32)],
        name="mlstm_proj",
        compiler_params=_params("parallel"),
    )(x, w_qkv, b_qkv, w_o, b_o, w_if, b_if)


def _sgu_kernel(x_ref, wu_ref, bu_ref, wv_ref, bv_ref, lng_ref, lnb_ref, ws_ref, bs_ref, a_ref, *, blocks):
    x = x_ref[...]
    v = _gelu(jnp.dot(x, wv_ref[...], preferred_element_type=F32) + bv_ref[...])
    vln = _layer_norm(v, lng_ref[...], lnb_ref[...]).astype(BF16)
    u = _gelu(jnp.dot(x, wu_ref[...], preferred_element_type=F32) + bu_ref[...])
    t_chunk = lax.broadcasted_iota(jnp.int32, (G_BLOCK, G_BLOCK), 0) // CHUNK
    s_chunk = lax.broadcasted_iota(jnp.int32, (G_BLOCK, G_BLOCK), 1) // CHUNK
    causal = s_chunk <= t_chunk
    for g in range(G_GROUPS):
        w = jnp.where(causal, ws_ref[g], 0.0).astype(BF16)
        bias = bs_ref[:, g:g + 1]
        cs = slice(g * G_GROUP_DIM, (g + 1) * G_GROUP_DIM)
        for blk in range(blocks):
            rs = slice(blk * G_BLOCK, (blk + 1) * G_BLOCK)
            mixed = jnp.dot(w, vln[rs, cs], preferred_element_type=F32) + bias
            a_ref[rs, cs] = (u[rs, cs] * mixed).astype(a_ref.dtype)


def _sgu(x, w_u, b_u, w_v, b_v, ln_g, ln_b, w_s, b_s_t, blocks=8):
    n = x.shape[0]
    tp = blocks * G_BLOCK
    const = lambda i: (0, 0)
    wspec = pl.BlockSpec((D_MODEL, G_WIDTH), const)
    vspec = pl.BlockSpec((1, G_WIDTH), const)
    return pl.pallas_call(
        functools.partial(_sgu_kernel, blocks=blocks),
        grid=(n // tp,),
        in_specs=[
            pl.BlockSpec((tp, D_MODEL), lambda i: (i, 0)),
            wspec, vspec, wspec, vspec, vspec, vspec,
            pl.BlockSpec((G_GROUPS, G_BLOCK, G_BLOCK), lambda i: (0, 0, 0)),
            pl.BlockSpec((G_BLOCK, G_GROUPS), const),
        ],
        out_specs=pl.BlockSpec((tp, G_WIDTH), lambda i: (i, 0)),
        out_shape=jax.ShapeDtypeStruct((n, G_WIDTH), BF16),
        name="spatial_gating",
        compiler_params=_params("parallel"),
    )(x, w_u, b_u, w_v, b_v, ln_g, ln_b, w_s, b_s_t)


def _segment_cumsum(x, axis, seg):
    pos = lax.broadcasted_iota(jnp.int32, x.shape, axis) % seg
    shift = 1
    while shift < seg:
        x = x + jnp.where(pos >= shift, pltpu.roll(x, shift, axis), 0.0)
        shift *= 2
    return x


def _mlstm_kernel(qkv_ref, gc_ref, gr_ref, osig_ref, mhg_ref, mb_ref, c_ref, n_ref, m_ref, *, chunks):
    @pl.when(pl.program_id(1) == 0)
    def _():
        c_ref[...] = jnp.zeros_like(c_ref)
        n_ref[...] = jnp.zeros_like(n_ref)
        m_ref[...] = jnp.zeros_like(m_ref)

    scale = M_QK_DIM ** -0.5
    gc = gc_ref[...]
    gr = gr_ref[...]
    b_cols = _segment_cumsum(jax.nn.log_sigmoid(gc), 0, MLSTM_CHUNK)
    b_rows = _segment_cumsum(jax.nn.log_sigmoid(gr), 1, MLSTM_CHUNK)
    t_idx = lax.broadcasted_iota(jnp.int32, (MLSTM_CHUNK, MLSTM_CHUNK), 0)
    s_idx = lax.broadcasted_iota(jnp.int32, (MLSTM_CHUNK, MLSTM_CHUNK), 1)
    tri = s_idx <= t_idx

    heads = range(M_HEADS)
    rows = lambda c: slice(c * MLSTM_CHUNK, (c + 1) * MLSTM_CHUNK)

    def local_terms(cs):
        units = [(c, h) for c in cs for h in heads]
        q = {u: qkv_ref[rows(u[0]), u[1] * M_QK_DIM:(u[1] + 1) * M_QK_DIM] for u in units}
        k = {u: qkv_ref[rows(u[0]), M_QK + u[1] * M_QK_DIM:M_QK + (u[1] + 1) * M_QK_DIM] for u in units}
        v = {u: qkv_ref[rows(u[0]), 2 * M_QK + u[1] * M_V_DIM:2 * M_QK + (u[1] + 1) * M_V_DIM] for u in units}
        i_col = {u: gc[rows(u[0]), u[1]:u[1] + 1] for u in units}
        b_col = {u: b_cols[rows(u[0]), M_HEADS + u[1]:M_HEADS + u[1] + 1] for u in units}
        ib_row = {u: (b_rows[M_HEADS + u[1]:M_HEADS + u[1] + 1, rows(u[0])]
                      - gr[u[1]:u[1] + 1, rows(u[0])]) for u in units}
        d = {u: jnp.where(tri, b_col[u] - ib_row[u], -jnp.inf) for u in units}
        d_max = {u: jnp.max(d[u], axis=1, keepdims=True) for u in units}
        qk = {u: lax.dot_general(q[u], k[u], (((1,), (1,)), ((), ())), preferred_element_type=F32) for u in units}
        s_loc = {u: qk[u] * scale * jnp.exp(d[u] - d_max[u]) for u in units}
        a_loc = {u: jnp.dot(s_loc[u].astype(BF16), v[u], preferred_element_type=F32) for u in units}
        r_loc = {u: jnp.sum(s_loc[u], axis=1, keepdims=True) for u in units}
        b_last = {u: b_col[u][MLSTM_CHUNK - 1:MLSTM_CHUNK, :] for u in units}
        g_col = {u: b_last[u] - b_col[u] + i_col[u] for u in units}
        g_max = {u: jnp.max(g_col[u], axis=0, keepdims=True) for u in units}
        wk = {u: jnp.exp(g_col[u] - g_max[u]) * k[u].astype(F32) for u in units}
        u_loc = {u: jnp.dot(wk[u].T.astype(BF16), v[u], preferred_element_type=F32) for u in units}
        nk_loc = {u: jnp.sum(wk[u], axis=0, keepdims=True) for u in units}
        return dict(q=q, b_col=b_col, d_max=d_max, a_loc=a_loc, r_loc=r_loc, b_last=b_last, g_max=g_max,
                    u_loc=u_loc, nk_loc=nk_loc)

    def carried_step(c, loc, c_state, n_state, m_state):
        inter = [loc["b_col"][(c, h)] + m_state[h] for h in heads]
        m_t = [jnp.maximum(inter[h], loc["d_max"][(c, h)]) for h in heads]
        w_intra = [jnp.exp(loc["d_max"][(c, h)] - m_t[h]) for h in heads]
        w_inter = [jnp.exp(inter[h] - m_t[h]) for h in heads]
        qc = [jnp.dot(loc["q"][(c, h)], c_state[h].astype(BF16), preferred_element_type=F32) * scale for h in heads]
        qn = [jnp.sum(loc["q"][(c, h)].astype(F32) * n_state[h], axis=1, keepdims=True) * scale for h in heads]
        m_new = [jnp.maximum(loc["b_last"][(c, h)] + m_state[h], loc["g_max"][(c, h)]) for h in heads]
        decay = [jnp.exp(loc["b_last"][(c, h)] + m_state[h] - m_new[h]) for h in heads]
        beta = [jnp.exp(loc["g_max"][(c, h)] - m_new[h]) for h in heads]
        c_next = [decay[h] * c_state[h] + beta[h] * loc["u_loc"][(c, h)] for h in heads]
        n_next = [decay[h] * n_state[h] + beta[h] * loc["nk_loc"][(c, h)] for h in heads]
        num = [w_intra[h] * loc["a_loc"][(c, h)] + w_inter[h] * qc[h] for h in heads]
        den = [w_intra[h] * loc["r_loc"][(c, h)] + w_inter[h] * qn[h] for h in heads]
        hv = [num[h] * (1.0 / jnp.maximum(jnp.abs(den[h]), jnp.exp(-m_t[h]))) for h in heads]
        hv = [hv[h] * lax.rsqrt(jnp.mean(hv[h] * hv[h], axis=1, keepdims=True) + RMS_EPS) for h in heads]
        for h in heads:
            vs = slice(h * M_V_DIM, (h + 1) * M_V_DIM)
            mb_ref[rows(c), vs] = (osig_ref[rows(c), vs].astype(F32) * (hv[h] * mhg_ref[:, vs])).astype(mb_ref.dtype)
        return c_next, n_next, m_new

    c_state = [c_ref[h] for h in heads]
    n_state = [n_ref[h] for h in heads]
    m_state = [m_ref[h][:, 0:1] for h in heads]
    groups = [list(range(g, min(g + MLSTM_GROUP, chunks))) for g in range(0, chunks, MLSTM_GROUP)]
    loc = local_terms(groups[0])
    for gi, group in enumerate(groups):
        loc_next = local_terms(groups[gi + 1]) if gi + 1 < len(groups) else None
        for c in group:
            c_state, n_state, m_state = carried_step(c, loc, c_state, n_state, m_state)
        loc = loc_next

    for h in heads:
        c_ref[h], n_ref[h] = c_state[h], n_state[h]
        m_ref[h] = jnp.broadcast_to(m_state[h], (1, LANES))


def _mlstm(qkv, gates_col, gates_row, osig_src, mh_g, batch, seq, chunks=8):
    ts = chunks * MLSTM_CHUNK
    tiles = seq // ts
    n = batch * seq
    return pl.pallas_call(
        functools.partial(_mlstm_kernel, chunks=chunks),
        grid=(batch, tiles),
        in_specs=[
            pl.BlockSpec((ts, 2 * M_QK + M_V), lambda b, t: (b * tiles + t, 0)),
            pl.BlockSpec((ts, LANES), lambda b, t: (b * tiles + t, 0)),
            pl.BlockSpec((2 * M_HEADS, ts), lambda b, t: (0, b * tiles + t)),
            pl.BlockSpec((ts, M_V), lambda b, t: (b * tiles + t, 0)),
            pl.BlockSpec((1, M_V), lambda b, t: (0, 0)),
        ],
        out_specs=pl.BlockSpec((ts, M_V), lambda b, t: (b * tiles + t, 0)),
        out_shape=jax.ShapeDtypeStruct((n, M_V), BF16),
        scratch_shapes=[
            pltpu.VMEM((M_HEADS, M_QK_DIM, M_V_DIM), F32),
            pltpu.VMEM((M_HEADS, 1, M_QK_DIM), F32),
            pltpu.VMEM((M_HEADS, 1, LANES), F32),
        ],
        name="mlstm",
        compiler_params=_params("parallel", "arbitrary"),
    )(qkv, gates_col, gates_row, osig_src, mh_g)


def _merge_kernel(x_ref, xin_ref, a_ref, mb_ref, wga_ref, bga_ref, wgb_ref, bgb_ref, wpa_ref, wpb_ref, wout_ref,
                  g_ref, b_ref, xo_ref, xpa_ref, xpb_ref):
    tm = x_ref.shape[0]
    halves = [slice(r, r + tm // MERGE_PARTS) for r in range(0, tm, tm // MERGE_PARTS)]
    dot = lambda lhs, w_ref: jnp.dot(lhs, w_ref[...], preferred_element_type=F32)
    ga = [jax.nn.sigmoid(dot(xin_ref[rs, :], wga_ref) + bga_ref[...]) for rs in halves]
    ya = [dot(a_ref[rs, :], wpa_ref) for rs in halves]
    gb = [jax.nn.sigmoid(dot(xin_ref[rs, :], wgb_ref) + bgb_ref[...]) for rs in halves]
    yb = [dot(mb_ref[rs, :], wpb_ref) for rs in halves]
    y = [(ga[h] * ya[h] + gb[h] * yb[h]).astype(BF16) for h in range(MERGE_PARTS)]
    mix = [dot(y[h], wout_ref) for h in range(MERGE_PARTS)]
    for h, rs in enumerate(halves):
        xn = _layer_norm(ALPHA * x_ref[rs, :] + mix[h], g_ref[...], b_ref[...])
        xo_ref[rs, :] = xn
        xpa_ref[rs, :], xpb_ref[rs, :] = _pack_row(xn)


def _merge(x, xin, a, mb, w_ga, b_ga, w_gb, b_gb, w_pa, w_pb, w_out, ln_g, ln_b, x_row0=0, tm=512):
    n = xin.shape[0]
    row = lambda i: (i, 0)
    const = lambda i: (0, 0)
    wspec = pl.BlockSpec((D_MODEL, D_MODEL), const)
    vspec = pl.BlockSpec((1, D_MODEL), const)
    return pl.pallas_call(
        _merge_kernel,
        grid=(n // tm,),
        in_specs=[
            pl.BlockSpec((tm, D_MODEL), lambda i: (x_row0 // tm + i, 0)),
            pl.BlockSpec((tm, D_MODEL), row),
            pl.BlockSpec((tm, G_WIDTH), row),
            pl.BlockSpec((tm, M_V), row),
            wspec, vspec, wspec, vspec, wspec, wspec, wspec, vspec, vspec,
        ],
        out_specs=[pl.BlockSpec((tm, D_MODEL), row), pl.BlockSpec((tm, PART), row), pl.BlockSpec((tm, PART), row)],
        out_shape=[jax.ShapeDtypeStruct((n, D_MODEL), F32),
                   jax.ShapeDtypeStruct((n, PART), I32), jax.ShapeDtypeStruct((n, PART), I32)],
        name="merge_ln",
        compiler_params=_params("parallel"),
    )(x, xin, a, mb, w_ga, b_ga, w_gb, b_gb, w_pa, w_pb, w_out, ln_g, ln_b)


def _first_max(v, idx, axes, sentinel):
    m = jnp.max(v, axis=axes, keepdims=True)
    first = jnp.min(jnp.where(v == m, idx, sentinel), axis=axes, keepdims=True)
    return m, first


def _router_kernel(x_ref, wr_ref, br_ref, upper_ref, eidx_ref, rank_ref, wtok_ref, cnt_ref, run_ref, *, tm):
    @pl.when(pl.program_id(0) == 0)
    def _():
        run_ref[...] = jnp.zeros_like(run_ref)

    tw = ROUTER_PART
    parts = range(tm // tw)
    shape3 = (N_GROUPS, GROUP_SIZE, tw)
    neg = -jnp.inf

    def split(v):
        hi = v.astype(BF16)
        return hi, (v - hi.astype(F32)).astype(BF16)

    nt = lambda a, b: lax.dot_general(a, b, (((1,), (1,)), ((), ())), preferred_element_type=F32)
    w_hi, w_lo = split(wr_ref[...])
    xs = [split(x_ref[i * tw:(i + 1) * tw, :]) for i in parts]
    logits = [nt(w_hi, x_hi) + (nt(w_hi, x_lo) + nt(w_lo, x_hi)) for x_hi, x_lo in xs]
    scores = [jax.nn.sigmoid(lg) for lg in logits]
    sel = [(sc + br_ref[...]).reshape(shape3) for sc in scores]
    scores3 = [sc.reshape(shape3) for sc in scores]
    member = lax.broadcasted_iota(jnp.int32, shape3, 1)
    group = lax.broadcasted_iota(jnp.int32, shape3, 0)

    best = [_first_max(v, member, 1, GROUP_SIZE) for v in sel]
    second = [jnp.max(jnp.where(member == f1, neg, v), axis=1, keepdims=True) for v, (_, f1) in zip(sel, best)]
    gscore = [m1 + m2 for (m1, _), m2 in zip(best, second)]
    gid = lax.broadcasted_iota(jnp.int32, gscore[0].shape, 0)
    gmask = [jnp.zeros(gscore[0].shape, dtype=jnp.bool_) for _ in parts]
    for _ in range(TOPK_GROUPS):
        hits = [gid == _first_max(g, gid, 0, N_GROUPS)[1] for g in gscore]
        gmask = [m | h for m, h in zip(gmask, hits)]
        gscore = [jnp.where(h, neg, g) for h, g in zip(hits, gscore)]

    cand = [jnp.where(m, v, neg) for m, v in zip(gmask, sel)]
    eid = group * GROUP_SIZE + member
    chosen = [jnp.zeros(shape3, dtype=jnp.bool_) for _ in parts]
    picks = [[] for _ in parts]
    for _ in range(TOP_K):
        firsts = [_first_max(c, eid, (0, 1), N_EXPERTS)[1] for c in cand]
        hits = [eid == fe for fe in firsts]
        chosen = [c | h for c, h in zip(chosen, hits)]
        cand = [jnp.where(h, neg, c) for h, c in zip(hits, cand)]
        picked = [jnp.sum(jnp.where(h, s3, 0.0), axis=(0, 1), keepdims=True)[0] for h, s3 in zip(hits, scores3)]
        for i in parts:
            picks[i].append((firsts[i][0], hits[i], picked[i]))

    chosen2 = [jnp.where(c, 1.0, 0.0).reshape(N_EXPERTS, tw) for c in chosen]
    counts = [jnp.sum(c2, axis=1, keepdims=True) for c2 in chosen2]
    bases = []
    base = run_ref[:, 0:1]
    for i in parts:
        bases.append(base)
        base = base + counts[i]
    prefix3 = [(jnp.dot(c2.astype(BF16), upper_ref[...], preferred_element_type=F32) + b0).reshape(shape3)
               for c2, b0 in zip(chosen2, bases)]

    for i in parts:
        cols = slice(i * tw, (i + 1) * tw)
        total = picks[i][0][2]
        for _, _, wk in picks[i][1:]:
            total = total + wk
        eidx_ref[:, cols] = jnp.concatenate([fe for fe, _, _ in picks[i]], axis=0)
        rank_ref[:, cols] = jnp.concatenate(
            [jnp.sum(jnp.where(hit, prefix3[i], 0.0), axis=(0, 1), keepdims=True)[0] for _, hit, _ in picks[i]],
            axis=0).astype(I32)
        w_rows = jnp.concatenate([wk / total * ROUTE_SCALE for _, _, wk in picks[i]]
                                 + [jnp.zeros((LANES - TOP_K, tw), F32)], axis=0)
        wtok_ref[cols, :] = w_rows.T

    run = run_ref[...] + (base - run_ref[:, 0:1])
    run_ref[...] = run
    cnt_ref[...] = run


def _router(x, wr_t, br, tm=1024):
    n = x.shape[0]
    tok = lambda i: (0, i)
    upper = (jnp.arange(ROUTER_PART)[:, None] < jnp.arange(ROUTER_PART)[None, :]).astype(BF16)
    return pl.pallas_call(
        functools.partial(_router_kernel, tm=tm),
        grid=(n // tm,),
        in_specs=[
            pl.BlockSpec((tm, D_MODEL), lambda i: (i, 0)),
            pl.BlockSpec((N_EXPERTS, D_MODEL), lambda i: (0, 0)),
            pl.BlockSpec((N_EXPERTS, 1), lambda i: (0, 0)),
            pl.BlockSpec((ROUTER_PART, ROUTER_PART), lambda i: (0, 0)),
        ],
        out_specs=[
            pl.BlockSpec((TOP_K, tm), tok),
            pl.BlockSpec((TOP_K, tm), tok),
            pl.BlockSpec((tm, LANES), lambda i: (i, 0)),
            pl.BlockSpec((N_EXPERTS, LANES), lambda i: (0, 0)),
        ],
        out_shape=[
            jax.ShapeDtypeStruct((TOP_K, n), I32),
            jax.ShapeDtypeStruct((TOP_K, n), I32),
            jax.ShapeDtypeStruct((n, LANES), F32),
            jax.ShapeDtypeStruct((N_EXPERTS, LANES), F32),
        ],
        scratch_shapes=[pltpu.VMEM((N_EXPERTS, LANES), F32)],
        name="router",
        compiler_params=_params("arbitrary"),
    )(x, wr_t, br, upper)


def _slots_kernel(starts_ref, eidx_ref, rank_ref, slot_ref):
    eidx = eidx_ref[...]
    slot = rank_ref[...]
    for e in range(N_EXPERTS):
        slot = slot + jnp.where(eidx == e, starts_ref[e], 0)
    slot_ref[...] = slot


def _slots(starts, eidx, rank, tn=2048):
    n = eidx.shape[1]
    tn = min(tn, n)
    spec = pl.BlockSpec((TOP_K, tn), lambda i, s: (0, i))
    return pl.pallas_call(
        _slots_kernel,
        grid_spec=pltpu.PrefetchScalarGridSpec(
            num_scalar_prefetch=1, grid=(n // tn,), in_specs=[spec, spec], out_specs=spec),
        out_shape=jax.ShapeDtypeStruct((TOP_K, n), I32),
        name="slots",
        compiler_params=_params("parallel"),
    )(starts, eidx, rank)


def _sc_mesh():
    return plsc.VectorSubcoreMesh(core_axis_name="c", subcore_axis_name="s")


def _sc_dispatch(parts, slots_flat, rows):
    n, width = parts[0].shape
    blocks = n // SC_WINDOW
    out_type = [jax.ShapeDtypeStruct((rows, width), part.dtype) for part in parts]

    @functools.partial(pl.kernel, out_type=out_type, mesh=_sc_mesh(), scratch_types=[], name="sc_dispatch")
    def run(*refs):
        i_hbm = refs[len(parts)]
        for x_hbm, o_hbm in zip(refs[:len(parts)], refs[len(parts) + 1:]):
            def body(x_vmem, i_vmem, o_hbm=o_hbm):
                pltpu.sync_copy(x_vmem, o_hbm.at[i_vmem.at[0]])

            pltpu.emit_pipeline(
                body,
                grid=(blocks, TOP_K),
                in_specs=[pl.BlockSpec((SC_WINDOW, width), lambda i, k: (i, 0)),
                          pl.BlockSpec((1, SC_WINDOW), lambda i, k: (0, k * blocks + i))],
                out_specs=[],
                core_axis_name=("c", "s"),
                dimension_semantics=(pltpu.PARALLEL, pltpu.ARBITRARY),
                trace_scopes=False,
            )(x_hbm, i_hbm)

    return run(*parts, slots_flat)


def _sc_gather(tables, idx_flat):
    count = idx_flat.shape[1]
    width = tables[0].shape[1]
    out_type = [jax.ShapeDtypeStruct((count, width), table.dtype) for table in tables]

    @functools.partial(pl.kernel, out_type=out_type, mesh=_sc_mesh(), scratch_types=[], name="sc_gather")
    def run(*refs):
        i_hbm = refs[len(tables)]
        for t_hbm, o_hbm in zip(refs[:len(tables)], refs[len(tables) + 1:]):
            def body(i_vmem, o_vmem, t_hbm=t_hbm):
                pltpu.sync_copy(t_hbm.at[i_vmem.at[0]], o_vmem)

            pltpu.emit_pipeline(
                body,
                grid=(count // SC_WINDOW,),
                in_specs=[pl.BlockSpec((1, SC_WINDOW), lambda i: (0, i))],
                out_specs=[pl.BlockSpec((SC_WINDOW, width), lambda i: (i, 0))],
                core_axis_name=("c", "s"),
                dimension_semantics=(pltpu.PARALLEL,),
                trace_scopes=False,
            )(i_hbm, o_hbm)

    return run(*tables, idx_flat)


def _expert_kernel(te_ref, used_ref, xa_ref, xb_ref, wg_ref, wu_ref, wd_ref, oa_ref, ob_ref, wgb_ref, wub_ref, wdb_ref):
    j = pl.program_id(0)

    @pl.when((j == 0) | (te_ref[j] != te_ref[jnp.maximum(j - 1, 0)]))
    def _():
        wgb_ref[...] = wg_ref[0].astype(BF16)
        wub_ref[...] = wu_ref[0].astype(BF16)
        wdb_ref[...] = wd_ref[0].astype(BF16)

    @pl.when(j < used_ref[0])
    def _():
        x = jnp.concatenate([_unpack_pairs(xa_ref[...]), _unpack_pairs(xb_ref[...])], axis=1).astype(BF16)
        hg = jnp.dot(x, wgb_ref[...], preferred_element_type=F32)
        hu = jnp.dot(x, wub_ref[...], preferred_element_type=F32)
        hid = (_silu(hg) * hu).astype(BF16)
        oa_ref[...], ob_ref[...] = _pack_row(jnp.dot(hid, wdb_ref[...], preferred_element_type=F32))


def _experts(tile_expert, tiles_used, xa, xb, layer, wg, wu, wd):
    rows = xa.shape[0]
    last = lambda j, used: jnp.minimum(j, used[0] - 1)
    row = lambda j, te, used: (last(j, used), 0)
    exp = lambda j, te, used: (layer * N_EXPERTS + te[last(j, used)], 0, 0)
    return pl.pallas_call(
        _expert_kernel,
        grid_spec=pltpu.PrefetchScalarGridSpec(
            num_scalar_prefetch=2,
            grid=(rows // ROW_TILE,),
            in_specs=[
                pl.BlockSpec((ROW_TILE, PART), row),
                pl.BlockSpec((ROW_TILE, PART), row),
                pl.BlockSpec((1, D_MODEL, D_EXPERT), exp),
                pl.BlockSpec((1, D_MODEL, D_EXPERT), exp),
                pl.BlockSpec((1, D_EXPERT, D_MODEL), exp),
            ],
            out_specs=[pl.BlockSpec((ROW_TILE, PART), row), pl.BlockSpec((ROW_TILE, PART), row)],
            scratch_shapes=[pltpu.VMEM((D_MODEL, D_EXPERT), BF16), pltpu.VMEM((D_MODEL, D_EXPERT), BF16),
                            pltpu.VMEM((D_EXPERT, D_MODEL), BF16)],
        ),
        out_shape=[jax.ShapeDtypeStruct((rows, PART), I32), jax.ShapeDtypeStruct((rows, PART), I32)],
        name="experts",
        compiler_params=_params("arbitrary"),
    )(tile_expert, tiles_used, xa, xb, wg, wu, wd)


def _finish_kernel(x_ref, p_ref, ga_ref, gb_ref, wtok_ref, wgs_ref, wus_ref, wds_ref, wpg_ref, bpg_ref, wp_ref,
                   lng_ref, lnb_ref, *out_refs):
    xb = x_ref[...].astype(BF16)
    hs = _silu(jnp.dot(xb, wgs_ref[...], preferred_element_type=F32)) * jnp.dot(
        xb, wus_ref[...], preferred_element_type=F32)
    shared = jnp.dot(hs.astype(BF16), wds_ref[...], preferred_element_type=F32)
    gate = jax.nn.sigmoid(jnp.dot(xb, wpg_ref[...], preferred_element_type=F32) + bpg_ref[...])
    ple = gate * jnp.dot(p_ref[0].astype(BF16), wp_ref[...], preferred_element_type=F32)

    wtok = wtok_ref[...]
    r_a = None
    r_b = None
    for k in range(TOP_K):
        wk = wtok[:, k:k + 1]
        a = wk * _unpack_pairs(ga_ref[k])
        b = wk * _unpack_pairs(gb_ref[k])
        r_a = a if r_a is None else r_a + a
        r_b = b if r_b is None else r_b + b
    routed = jnp.concatenate([r_a, r_b], axis=1)

    xn = _layer_norm(ALPHA * x_ref[...] + (routed + shared + ple), lng_ref[...], lnb_ref[...])
    if len(out_refs) == 2 and out_refs[1].dtype == BF16:
        out_refs[0][...] = xn
        out_refs[1][...] = xn.astype(BF16)
    else:
        out_refs[-1][...] = xn


def _finish(x, p_all, layer, stream, ga, gb, wtok, wgs, wus, wds, wpg, bpg, wp, ln_g, ln_b,
            out_base=None, total_rows=None, tm=512):
    n = x.shape[0]
    p_block = stream * (n // tm)
    row = lambda i: (i, 0)
    const = lambda i: (0, 0)
    if total_rows is None:
        out_specs = [pl.BlockSpec((tm, D_MODEL), row), pl.BlockSpec((tm, D_MODEL), row)]
        out_shape = [jax.ShapeDtypeStruct((n, D_MODEL), F32), jax.ShapeDtypeStruct((n, D_MODEL), BF16)]
    else:
        out_specs = [pl.BlockSpec((tm, D_MODEL), lambda i: (p_block + i, 0))]
        out_shape = [jax.ShapeDtypeStruct((total_rows, D_MODEL), F32)]
    extra_specs = [] if out_base is None else [pl.BlockSpec(memory_space=pl.ANY)]
    extra_args = [] if out_base is None else [out_base]
    aliases = {} if out_base is None else {13: 0}
    return pl.pallas_call(
        _finish_kernel,
        grid=(n // tm,),
        input_output_aliases=aliases,
        in_specs=[
            pl.BlockSpec((tm, D_MODEL), row),
            pl.BlockSpec((1, tm, P_DIM), lambda i: (layer, p_block + i, 0)),
            pl.BlockSpec((TOP_K, tm, PART), lambda i: (0, i, 0)),
            pl.BlockSpec((TOP_K, tm, PART), lambda i: (0, i, 0)),
            pl.BlockSpec((tm, LANES), row),
            pl.BlockSpec((D_MODEL, D_SHARED), const),
            pl.BlockSpec((D_MODEL, D_SHARED), const),
            pl.BlockSpec((D_SHARED, D_MODEL), const),
            pl.BlockSpec((D_MODEL, D_MODEL), const),
            pl.BlockSpec((1, D_MODEL), const),
            pl.BlockSpec((P_DIM, D_MODEL), const),
            pl.BlockSpec((1, D_MODEL), const),
            pl.BlockSpec((1, D_MODEL), const),
        ] + extra_specs,
        out_specs=out_specs,
        out_shape=out_shape,
        name="moe_finish_ln",
        compiler_params=_params("parallel"),
    )(x, p_all, ga, gb, wtok, wgs, wus, wds, wpg, bpg, wp, ln_g, ln_b, *extra_args)


def _route(x, xpa, xpb, w_router_t, b_router):
    n = x.shape[0]
    pairs = n * TOP_K
    tiles = pairs // ROW_TILE + N_EXPERTS
    eidx, rank, wtok, counts = _router(x, w_router_t, b_router)

    cnt = counts[:, 0].astype(I32)
    group_tiles = (cnt + ROW_TILE - 1) // ROW_TILE
    tile_end = jnp.cumsum(group_tiles)
    starts = (tile_end - group_tiles) * ROW_TILE
    tile_id = jnp.arange(tiles, dtype=I32)
    tile_expert = jnp.minimum(jnp.sum((tile_end[None, :] <= tile_id[:, None]).astype(I32), axis=1), N_EXPERTS - 1)

    slots_flat = _slots(starts, eidx, rank).reshape(1, pairs)
    xsa, xsb = _sc_dispatch([xpa, xpb], slots_flat, tiles * ROW_TILE)
    return dict(xsa=xsa, xsb=xsb, slots_flat=slots_flat, wtok=wtok, tile_expert=tile_expert,
                tiles_used=tile_end[N_EXPERTS - 1:])


class _IssueOrder:
    def __init__(self):
        self.last = None

    def before(self, operand):
        if self.last is None:
            return operand
        self.last, operand = lax.optimization_barrier((self.last, operand))
        return operand

    def after(self, result):
        self.last = result


def kernel(x, p, w_in, b_in, sg_ln_g, sg_ln_b, w_s, b_s, mh_g, w_pa, w_pb, w_out, ln1_g, ln1_b, w_router, b_router, w_gate_e, w_up_e, w_down_e, w_gate_s, w_up_s, w_down_s, w_pg, b_pg, w_p, ln2_g, ln2_b):
    batch, seq, _ = x.shape
    streams = STREAMS if batch % STREAMS == 0 else 1
    sb = batch // streams
    n = sb * seq
    x_all = x.reshape(batch * seq, D_MODEL)
    xf = [x_all] * streams
    x_row0 = [s * n for s in range(streams)]
    xb = [x_all[s * n:(s + 1) * n].astype(BF16) for s in range(streams)]
    p_all = p.reshape(DEPTH, batch * seq, P_DIM)

    wg_all = w_gate_e.reshape(DEPTH * N_EXPERTS, D_MODEL, D_EXPERT)
    wu_all = w_up_e.reshape(DEPTH * N_EXPERTS, D_MODEL, D_EXPERT)
    wd_all = w_down_e.reshape(DEPTH * N_EXPERTS, D_EXPERT, D_MODEL)

    c_uv = 2 * G_WIDTH
    c_qkv = c_uv + 2 * M_QK + M_V
    c_o = c_qkv + M_V
    c_if = c_o + 2 * M_HEADS
    c_gb = c_if + D_MODEL

    w_in_bf = w_in.astype(BF16)

    def layer_weights(l):
        w = w_in_bf[l]
        b = b_in[l][None, :]
        return dict(
            sgu=(w[:, :G_WIDTH], b[:, :G_WIDTH], w[:, G_WIDTH:c_uv], b[:, G_WIDTH:c_uv],
                 sg_ln_g[l][None, :], sg_ln_b[l][None, :], w_s[l], b_s[l].T),
            mproj=(w[:, c_uv:c_qkv], b[:, c_uv:c_qkv], w[:, c_qkv:c_o], b[:, c_qkv:c_o],
                   jnp.pad(w[:, c_o:c_if], ((0, 0), (0, LANES - 2 * M_HEADS))),
                   jnp.pad(b[:, c_o:c_if], ((0, 0), (0, LANES - 2 * M_HEADS)))),
            mhg=mh_g[l][None, :],
            merge=(w[:, c_if:c_gb], b[:, c_if:c_gb], w[:, c_gb:], b[:, c_gb:],
                   w_pa[l].astype(BF16), w_pb[l].astype(BF16), w_out[l].astype(BF16),
                   ln1_g[l][None, :], ln1_b[l][None, :]),
            router=(w_router[l].T, b_router[l][:, None]),
            finish=(w_gate_s[l].astype(BF16), w_up_s[l].astype(BF16), w_down_s[l].astype(BF16),
                    w_pg[l].astype(BF16), b_pg[l][None, :], w_p[l].astype(BF16),
                    ln2_g[l][None, :], ln2_b[l][None, :]))

    weights = [layer_weights(l) for l in range(DEPTH)]
    order = _IssueOrder()
    st = [dict() for _ in range(streams)]
    result = [None]

    def proj(s, l):
        st[s]["qkv"], st[s]["osig"], st[s]["gif"], st[s]["gif_rows"] = _mproj(
            order.before(xb[s]), *weights[l]["mproj"])
        order.after(st[s]["gif"])
        st[s]["a"] = _sgu(order.before(xb[s]), *weights[l]["sgu"])
        order.after(st[s]["a"])

    def recur(s, l):
        st[s]["mb"] = _mlstm(order.before(st[s]["qkv"]), st[s]["gif"], st[s]["gif_rows"], st[s]["osig"],
                             weights[l]["mhg"], sb, seq)
        order.after(st[s]["mb"])

    def merge(s, l):
        xf[s], xpa, xpb = _merge(xf[s], order.before(xb[s]), st[s]["a"], st[s]["mb"], *weights[l]["merge"],
                                        x_row0=x_row0[s])
        x_row0[s] = 0
        st[s]["packed"] = (xpa, xpb)
        order.after(xpb)

    def route(s, l):
        st[s]["route"] = _route(order.before(xf[s]), *st[s]["packed"], *weights[l]["router"])
        order.after(st[s]["route"]["wtok"])

    def experts(s, l):
        r = st[s]["route"]
        ysa, ysb = _experts(r["tile_expert"], r["tiles_used"], order.before(r["xsa"]), r["xsb"], l,
                            wg_all, wu_all, wd_all)
        order.after(ysb)
        ga, gb = _sc_gather([ysa, ysb], r["slots_flat"])
        st[s]["gathered"] = (ga.reshape(TOP_K, n, PART), gb.reshape(TOP_K, n, PART))

    def finish(s, l):
        last = l == DEPTH - 1
        outs = _finish(order.before(xf[s]), p_all, l, s, *st[s]["gathered"], st[s]["route"]["wtok"],
                       *weights[l]["finish"], out_base=result[0] if last else None,
                       total_rows=batch * seq if last else None)
        if last:
            result[0], = outs
        else:
            xf[s], xb[s] = outs
        order.after(outs[0])

    if streams == 1:
        for l in range(DEPTH):
            proj(0, l), recur(0, l), merge(0, l), route(0, l), experts(0, l), finish(0, l)
    else:
        proj(0, 0), recur(0, 0), merge(0, 0), route(0, 0), proj(1, 0)
        for l in range(DEPTH):
            more = l + 1 < DEPTH
            experts(0, l), recur(1, l), merge(1, l)
            if more:
                finish(0, l), route(1, l), proj(0, l + 1)
            else:
                route(1, l), finish(0, l)
            experts(1, l)
            if more:
                recur(0, l + 1), merge(0, l + 1)
            finish(1, l)
            if more:
                route(0, l + 1), proj(1, l + 1)
    return result[0].reshape(batch, seq, D_MODEL)
```
